```python
import math
import jax, jax.numpy as jnp
from jax import lax
import numpy as np

D_MODEL = 2048
BATCH = 8
SEQ = 4096
DEPTH = 4

GRID_W = 64
CTX_LEN = 256
N_MIXERS = 3
N_LAYERS_GLA = (DEPTH + 2) // 3
N_LAYERS_MLA = (DEPTH + 1) // 3
N_LAYERS_SWA = DEPTH // 3
EPS = 1e-6
ROPE_THETA = 10000.0
ROPE_DIM = 64
BLOCK = 128
D_FF = -(-8 * D_MODEL // (3 * 256)) * 256
GLA_HEADS = 4
GLA_DK = D_MODEL // 2 // GLA_HEADS
GLA_DV = D_MODEL // GLA_HEADS
GLA_QK = GLA_HEADS * GLA_DK
GLA_VD = GLA_HEADS * GLA_DV
GLA_GATE_RANK = 16
GLA_TAU = 16.0
GLA_CHUNK = 64
MLA_HEADS = D_MODEL // 128
MLA_Q_RANK = 512
MLA_KV_RANK = 512
MLA_NOPE = 128
MLA_ROPE = ROPE_DIM
MLA_V = 128
MLA_SCALE = (MLA_NOPE + MLA_ROPE) ** -0.5
SWA_HEADS = D_MODEL // 64
SWA_KV_HEADS = SWA_HEADS // 8
SWA_GROUP = SWA_HEADS // SWA_KV_HEADS
SWA_HEAD_DIM = 64
SWA_SCALE = SWA_HEAD_DIM ** -0.5
WINDOW = 128

kernel_name = "hybrid_gla_mla_swa_dit_trunk"


def rmsnorm(x, g):
    xf = x.astype(jnp.float32)
    y = xf * lax.rsqrt(jnp.mean(xf * xf, axis=-1, keepdims=True) + EPS)
    return (y * g.astype(jnp.float32)).astype(x.dtype)


def modulation(cond, w_ada, b_ada):
    m = jax.nn.silu(cond) @ w_ada + b_ada
    return jnp.split(m, 6, axis=-1)


def adaln_in(h, g, shift, scale):
    return rmsnorm(h, g) * (1 + scale) + shift


def axial_rope_tables(n_rows):
    row = jnp.repeat(jnp.arange(n_rows), GRID_W).astype(jnp.float32)
    col = jnp.tile(jnp.arange(GRID_W), n_rows).astype(jnp.float32)
    n_freq = ROPE_DIM // 4
    inv_freq = ROPE_THETA ** (-jnp.arange(n_freq, dtype=jnp.float32) / n_freq)
    ang = jnp.concatenate([row[:, None] * inv_freq, col[:, None] * inv_freq], axis=-1)
    return jnp.cos(ang), jnp.sin(ang)


def apply_rope(x, cos, sin):
    x1, x2 = jnp.split(x, 2, axis=-1)
    cos = cos.astype(x.dtype)
    sin = sin.astype(x.dtype)
    return jnp.concatenate([x1 * cos - x2 * sin, x2 * cos + x1 * sin], axis=-1)


def _split_heads(a, n):
    return a.reshape(a.shape[:-1] + (n, a.shape[-1] // n))


def _to_blocks(a):
    return jnp.moveaxis(a.reshape((a.shape[0], a.shape[1] // BLOCK, BLOCK) + a.shape[2:]), 1, 0)


def _from_blocks(o):
    o = jnp.moveaxis(o, 0, 1)
    return o.reshape(o.shape[0], o.shape[1] * o.shape[2], -1)


def swiglu(h, w_in, w_out):
    g, u = jnp.split(h @ w_in, 2, axis=-1)
    return (jax.nn.silu(g) * u) @ w_out


def gla_chunked(q, k, v, lg, s0):
    out_dtype = v.dtype
    causal = jnp.tril(jnp.ones((GLA_CHUNK, GLA_CHUNK), bool))

    def to_chunks(a):
        b_, h_, t_, d_ = a.shape
        return jnp.moveaxis(a.reshape(b_, h_, t_ // GLA_CHUNK, GLA_CHUNK, d_), 2, 0)

    def step(s, inp):
        qc, kc, vc, gc = [t.astype(jnp.float32) for t in inp]
        b = jnp.cumsum(gc, axis=2)
        o_inter = jnp.einsum('bhcd,bhde->bhce', qc * jnp.exp(b), s)
        diff = jnp.where(causal[:, :, None], b[:, :, :, None, :] - b[:, :, None, :, :], -jnp.inf)
        decay = jnp.exp(diff)
        attn = jnp.einsum('bhijd,bhjd->bhij', qc[:, :, :, None, :] * decay, kc)
        o_intra = jnp.einsum('bhij,bhje->bhie', attn, vc)
        b_last = b[:, :, -1:, :]
        k_dec = kc * jnp.exp(b_last - b)
        s_new = jnp.exp(b_last[:, :, 0, :])[..., None] * s + jnp.einsum('bhcd,bhce->bhde', k_dec, vc)
        return s_new, o_inter + o_intra

    s_fin, o = lax.scan(step, s0, (to_chunks(q), to_chunks(k), to_chunks(v), to_chunks(lg)))
    o = jnp.moveaxis(o, 0, 2)
    o = o.reshape(o.shape[0], o.shape[1], -1, o.shape[-1]).astype(out_dtype)
    return o, s_fin


def gla_mixer(h_ctx, h_lat, w_in, w_gate_down, w_gate_up, b_gate, g_head, w_out, with_ctx_out):
    def heads(a):
        return jnp.swapaxes(_split_heads(a, GLA_HEADS), 1, 2)

    def project(h):
        q, k, v, r = jnp.split(h @ w_in, [GLA_QK, 2 * GLA_QK, 2 * GLA_QK + GLA_VD], axis=-1)
        lg = [heads(jax.nn.log_sigmoid(((h @ w_gate_down[d]) @ w_gate_up[d] + b_gate[d]).astype(jnp.float32)) / GLA_TAU)
              for d in range(2)]
        return heads(q) * (GLA_DK ** -0.5), heads(k), heads(v), r, lg

    qc, kc, vc, rc, lgc = project(h_ctx)
    ql, kl, vl, rl, lgl = project(h_lat)
    s0 = jnp.zeros(qc.shape[:2] + (GLA_DK, GLA_DV), jnp.float32)
    flip = lambda a: jnp.flip(a, axis=2)
    oc_f, sc_f = gla_chunked(qc, kc, vc, lgc[0], s0)
    ol_f, _ = gla_chunked(ql, kl, vl, lgl[0], sc_f)
    oc_b, sc_b = gla_chunked(flip(qc), flip(kc), flip(vc), flip(lgc[1]), s0)
    ol_b, _ = gla_chunked(flip(ql), flip(kl), flip(vl), flip(lgl[1]), sc_b)

    def out(o_f, o_b, r):
        o = rmsnorm(o_f + flip(o_b), g_head)
        o = jnp.swapaxes(o, 1, 2).reshape(r.shape[:-1] + (GLA_VD,))
        return (o * jax.nn.silu(r)) @ w_out

    y_lat = out(ol_f, ol_b, rl)
    y_ctx = out(oc_f, oc_b, rc) if with_ctx_out else None
    return y_ctx, y_lat


def mla_mixer(h_ctx, h_lat, cos, sin, w_in, g_q, w_uq, g_kv, w_ukv, w_out, with_ctx_out):
    def project(h, rotate):
        cq, ckv, k_rope = jnp.split(h @ w_in, [MLA_Q_RANK, MLA_Q_RANK + MLA_KV_RANK], axis=-1)
        q = _split_heads(rmsnorm(cq, g_q) @ w_uq, MLA_HEADS)
        kv = _split_heads(rmsnorm(ckv, g_kv) @ w_ukv, MLA_HEADS)
        q_nope, q_rope = jnp.split(q, [MLA_NOPE], axis=-1)
        k_nope, v = jnp.split(kv, [MLA_NOPE], axis=-1)
        if rotate:
            q_rope = apply_rope(q_rope, cos[:, None, :], sin[:, None, :])
            k_rope = apply_rope(k_rope, cos, sin)
        return q_nope, q_rope, k_nope, k_rope, v

    def attend(qn, qr, kn, kr, v):
        s = jnp.einsum('bqhd,bkhd->bhqk', qn, kn) + jnp.einsum('bqhr,bkr->bhqk', qr, kr)
        p = jax.nn.softmax(s.astype(jnp.float32) * MLA_SCALE, axis=-1).astype(v.dtype)
        return jnp.einsum('bhqk,bkhd->bqhd', p, v)

    qn_c, qr_c, kn_c, kr_c, v_c = project(h_ctx, False)
    qn_l, qr_l, kn_l, kr_l, v_l = project(h_lat, True)
    kn_all = jnp.concatenate([kn_c, kn_l], axis=1)
    kr_all = jnp.concatenate([kr_c, kr_l], axis=1)
    v_all = jnp.concatenate([v_c, v_l], axis=1)
    o = lax.map(lambda qb: attend(qb[0], qb[1], kn_all, kr_all, v_all), (_to_blocks(qn_l), _to_blocks(qr_l)))
    y_lat = _from_blocks(o) @ w_out
    if with_ctx_out:
        o_c = attend(qn_c, qr_c, kn_c, kr_c, v_c)
        y_ctx = o_c.reshape(o_c.shape[0], o_c.shape[1], -1) @ w_out
    else:
        y_ctx = None
    return y_ctx, y_lat


def swa_mixer(h_ctx, h_lat, cos, sin, w_in, sinks, w_out, with_ctx_out):
    def project(h, rotate):
        q, k, v = jnp.split(h @ w_in, [SWA_HEADS * SWA_HEAD_DIM, (SWA_HEADS + SWA_KV_HEADS) * SWA_HEAD_DIM], axis=-1)
        q = q.reshape(q.shape[:-1] + (SWA_KV_HEADS, SWA_GROUP, SWA_HEAD_DIM))
        k = _split_heads(k, SWA_KV_HEADS)
        v = _split_heads(v, SWA_KV_HEADS)
        if rotate:
            q = apply_rope(q, cos[:, None, None, :], sin[:, None, None, :])
            k = apply_rope(k, cos[:, None, :], sin[:, None, :])
        return q, k, v

    sink = sinks.reshape(SWA_KV_HEADS, SWA_GROUP).astype(jnp.float32)

    def attend(q, k, v, mask):
        s = jnp.einsum('bqngd,bknd->bngqk', q, k).astype(jnp.float32) * SWA_SCALE
        if mask is not None:
            s = jnp.where(mask, s, -jnp.inf)
        sink_col = jnp.broadcast_to(sink[None, :, :, None, None], s.shape[:-1] + (1,))
        p = jax.nn.softmax(jnp.concatenate([s, sink_col], axis=-1), axis=-1)[..., :-1].astype(v.dtype)
        return jnp.einsum('bngqk,bknd->bqngd', p, v)

    q_c, k_c, v_c = project(h_ctx, False)
    q_l, k_l, v_l = project(h_lat, True)
    t_lat = h_lat.shape[1]
    pad = ((0, 0), (BLOCK, BLOCK), (0, 0), (0, 0))
    k_pad = jnp.pad(k_l, pad)
    v_pad = jnp.pad(v_l, pad)
    a_idx = jnp.arange(BLOCK)
    b_idx = jnp.arange(3 * BLOCK)
    band = jnp.abs(a_idx[:, None] - b_idx[None, :] + BLOCK) <= WINDOW
    ctx_cols = jnp.ones((BLOCK, k_c.shape[1]), bool)

    def block(args):
        qb, n = args
        kb = lax.dynamic_slice_in_dim(k_pad, n * BLOCK, 3 * BLOCK, axis=1)
        vb = lax.dynamic_slice_in_dim(v_pad, n * BLOCK, 3 * BLOCK, axis=1)
        kpos = (n - 1) * BLOCK + b_idx
        valid = band & ((kpos >= 0) & (kpos < t_lat))[None, :]
        mask = jnp.concatenate([ctx_cols, valid], axis=1)
        return attend(qb, jnp.concatenate([k_c, kb], axis=1), jnp.concatenate([v_c, vb], axis=1), mask)

    o = lax.map(block, (_to_blocks(q_l), jnp.arange(t_lat // BLOCK)))
    y_lat = _from_blocks(o) @ w_out
    if with_ctx_out:
        o_c = attend(q_c, k_c, v_c, None)
        y_ctx = o_c.reshape(o_c.shape[0], o_c.shape[1], -1) @ w_out
    else:
        y_ctx = None
    return y_ctx, y_lat


def setup_inputs(seed: int = 0) -> dict:
    key = jax.random.key(seed)
    ks = iter(jax.random.split(key, 32))
    D = D_MODEL

    def nrm(shape, scale=1.0):
        return jax.random.normal(next(ks), shape, jnp.float32) * scale

    def gain(shape):
        return 1.0 + nrm(shape, 0.02)

    return {
        "x": nrm((BATCH, SEQ, D)),
        "c": nrm((BATCH, D)),
        "ctx": nrm((BATCH, CTX_LEN, D)),
        "c_ctx": nrm((D,)),
        "w_ada": nrm((DEPTH, D, 6 * D), 0.5 * D ** -0.5),
        "b_ada": nrm((DEPTH, 6 * D), 0.01),
        "g_norm": gain((DEPTH, 4, D)),
        "w_ffn_in": nrm((DEPTH, D, 2 * D_FF), D ** -0.5),
        "w_ffn_out": nrm((DEPTH, D_FF, D), D_FF ** -0.5),
        "gla_w_in": nrm((N_LAYERS_GLA, D, 2 * GLA_QK + 2 * GLA_VD), D ** -0.5),
        "gla_w_gate_down": nrm((N_LAYERS_GLA, 2, D, GLA_GATE_RANK), D ** -0.5),
        "gla_w_gate_up": nrm((N_LAYERS_GLA, 2, GLA_GATE_RANK, GLA_QK), GLA_GATE_RANK ** -0.5),
        "gla_b_gate": nrm((N_LAYERS_GLA, 2, GLA_QK), 0.1),
        "gla_g_head": gain((N_LAYERS_GLA, GLA_DV)),
        "gla_w_out": nrm((N_LAYERS_GLA, GLA_VD, D), GLA_VD ** -0.5),
        "mla_w_in": nrm((N_LAYERS_MLA, D, MLA_Q_RANK + MLA_KV_RANK + MLA_ROPE), D ** -0.5),
        "mla_g_q": gain((N_LAYERS_MLA, MLA_Q_RANK)),
        "mla_w_uq": nrm((N_LAYERS_MLA, MLA_Q_RANK, MLA_HEADS * (MLA_NOPE + MLA_ROPE)), MLA_Q_RANK ** -0.5),
        "mla_g_kv": gain((N_LAYERS_MLA, MLA_KV_RANK)),
        "mla_w_ukv": nrm((N_LAYERS_MLA, MLA_KV_RANK, MLA_HEADS * (MLA_NOPE + MLA_V)), MLA_KV_RANK ** -0.5),
        "mla_w_out": nrm((N_LAYERS_MLA, MLA_HEADS * MLA_V, D), (MLA_HEADS * MLA_V) ** -0.5),
        "swa_w_in": nrm((N_LAYERS_SWA, D, (SWA_HEADS + 2 * SWA_KV_HEADS) * SWA_HEAD_DIM), D ** -0.5),
        "swa_sinks": nrm((N_LAYERS_SWA, SWA_HEADS)),
        "swa_w_out": nrm((N_LAYERS_SWA, SWA_HEADS * SWA_HEAD_DIM, D), (SWA_HEADS * SWA_HEAD_DIM) ** -0.5),
    }


def reference(x, c, ctx, c_ctx, w_ada, b_ada, g_norm, w_ffn_in, w_ffn_out,
              gla_w_in, gla_w_gate_down, gla_w_gate_up, gla_b_gate, gla_g_head, gla_w_out,
              mla_w_in, mla_g_q, mla_w_uq, mla_g_kv, mla_w_ukv, mla_w_out,
              swa_w_in, swa_sinks, swa_w_out):
    ROWS = x.shape[1] // GRID_W
    cos, sin = axial_rope_tables(ROWS)
    h_lat, h_ctx = x, ctx
    for i in range(DEPTH):
        kind, j = i % N_MIXERS, i // N_MIXERS
        last = i == DEPTH - 1
        ml = [t[:, None, :] for t in modulation(c, w_ada[i], b_ada[i])]
        mc = modulation(c_ctx, w_ada[i], b_ada[i])
        a_lat = adaln_in(h_lat, g_norm[i, 0], ml[0], ml[1])
        a_ctx = adaln_in(h_ctx, g_norm[i, 0], mc[0], mc[1])
        if kind == 0:
            y_ctx, y_lat = gla_mixer(a_ctx, a_lat, gla_w_in[j], gla_w_gate_down[j], gla_w_gate_up[j],
                                     gla_b_gate[j], gla_g_head[j], gla_w_out[j], not last)
        elif kind == 1:
            y_ctx, y_lat = mla_mixer(a_ctx, a_lat, cos, sin, mla_w_in[j], mla_g_q[j], mla_w_uq[j],
                                     mla_g_kv[j], mla_w_ukv[j], mla_w_out[j], not last)
        else:
            y_ctx, y_lat = swa_mixer(a_ctx, a_lat, cos, sin, swa_w_in[j], swa_sinks[j], swa_w_out[j], not last)
        h_lat = h_lat + ml[2] * rmsnorm(y_lat, g_norm[i, 1])
        f_lat = swiglu(adaln_in(h_lat, g_norm[i, 2], ml[3], ml[4]), w_ffn_in[i], w_ffn_out[i])
        h_lat = h_lat + ml[5] * rmsnorm(f_lat, g_norm[i, 3])
        if not last:
            h_ctx = h_ctx + mc[2] * rmsnorm(y_ctx, g_norm[i, 1])
            f_ctx = swiglu(adaln_in(h_ctx, g_norm[i, 2], mc[3], mc[4]), w_ffn_in[i], w_ffn_out[i])
            h_ctx = h_ctx + mc[5] * rmsnorm(f_ctx, g_norm[i, 3])
    return h_lat
```

```python
import functools

import numpy as np
import jax
import jax.numpy as jnp
from jax import lax
from jax.experimental import pallas as pl
from jax.experimental.pallas import tpu as pltpu

F32 = jnp.float32
BF16 = jnp.bfloat16

EPS = 1e-6
ROPE_THETA = 10000.0
ROPE_DIM = 64
GRID_W = 64
WINDOW = 128
GLA_HEADS = 4
GLA_GATE_RANK = 16
GLA_TAU = 16.0
MLA_Q_RANK = 512
MLA_KV_RANK = 512
MLA_NOPE = 128
MLA_V = 128
SWA_HEAD_DIM = 64
SWA_GROUP = 8

LANES = 128
SUBLANES = 8
VMEM_LIMIT = 56 * 1024 * 1024
ROW_TILE = 512
MOD_ROWS = 16
NEG_BIG = -1e30


def _cparams(sem):
    return pltpu.CompilerParams(dimension_semantics=sem, vmem_limit_bytes=VMEM_LIMIT)


def _dot(a, b):
    return jnp.dot(a, b, preferred_element_type=F32)


def _dot_nt(a, b):
    return lax.dot_general(a, b, (((1,), (1,)), ((), ())), preferred_element_type=F32)


def _dot_tn(a, b):
    return lax.dot_general(a, b, (((0,), (0,)), ((), ())), preferred_element_type=F32)


def _sigmoid(x):
    return 1.0 / (1.0 + jnp.exp(-x))


def _rms(x):
    return x * lax.rsqrt(jnp.mean(x * x, axis=-1, keepdims=True) + EPS)


def _mod_kernel(c_ref, w_ref, b_ref, o_ref):
    c = c_ref[...]
    s = (c * _sigmoid(c)).astype(BF16)
    o_ref[0] = _dot(s, w_ref[0].astype(BF16)) + b_ref[0]


def _modulation(cond, w_ada, b_ada):
    depth, d, n = w_ada.shape
    tn = 1024
    return pl.pallas_call(
        _mod_kernel,
        grid=(depth, n // tn),
        in_specs=[
            pl.BlockSpec((MOD_ROWS, d), lambda l, j: (0, 0)),
            pl.BlockSpec((1, d, tn), lambda l, j: (l, 0, j)),
            pl.BlockSpec((1, 1, tn), lambda l, j: (l, 0, j)),
        ],
        out_specs=pl.BlockSpec((1, MOD_ROWS, tn), lambda l, j: (l, 0, j)),
        out_shape=jax.ShapeDtypeStruct((depth, MOD_ROWS, n), F32),
        compiler_params=_cparams(("parallel", "parallel")),
        name="modulation",
    )(cond, w_ada, b_ada.reshape(depth, 1, n))


def _nm_matmul_kernel(*refs, modulate, shift_idx, scale_idx, d):
    if modulate:
        x_ref, g_ref, mod_ref, w_ref, o_ref, a_scr = refs
    else:
        x_ref, g_ref, w_ref, o_ref, a_scr = refs

    @pl.when(pl.program_id(1) == 0)
    def _():
        y = _rms(x_ref[...].astype(F32)) * g_ref[...]
        if modulate:
            m = mod_ref[0]
            y = y * (1.0 + m[:, scale_idx * d:(scale_idx + 1) * d]) + m[:, shift_idx * d:(shift_idx + 1) * d]
        a_scr[...] = y.astype(BF16)

    o_ref[...] = _dot(a_scr[...], w_ref[...]).astype(o_ref.dtype)


def _nm_matmul(x, xcol, k, g, w, tn, out_dtype, name, mod=None, shift_idx=0, scale_idx=1, tiles_per_batch=1, nb=0):
    n_rows = x.shape[0]
    n = w.shape[1]
    tm = ROW_TILE
    modulate = mod is not None
    in_specs = [pl.BlockSpec((tm, k), lambda i, j: (i, xcol)),
                pl.BlockSpec((1, k), lambda i, j: (0, 0))]
    args = [x, g.reshape(1, k)]
    if modulate:
        in_specs.append(pl.BlockSpec((1, 1, mod.shape[-1]),
                                     lambda i, j: (jnp.minimum(i // tiles_per_batch, nb), 0, 0)))
        args.append(mod)
    in_specs.append(pl.BlockSpec((k, tn), lambda i, j: (0, j)))
    args.append(w)
    return pl.pallas_call(
        functools.partial(_nm_matmul_kernel, modulate=modulate, shift_idx=shift_idx, scale_idx=scale_idx, d=k),
        grid=(n_rows // tm, n // tn),
        in_specs=in_specs,
        out_specs=pl.BlockSpec((tm, tn), lambda i, j: (i, j)),
        out_shape=jax.ShapeDtypeStruct((n_rows, n), out_dtype),
        scratch_shapes=[pltpu.VMEM((tm, k), BF16)],
        compiler_params=_cparams(("parallel", "arbitrary")),
        name=name,
    )(*args)


def _proj_resid_kernel(a_ref, w_ref, h_ref, g_ref, mod_ref, o_ref, *, gate_idx, d):
    y = _dot(a_ref[...].astype(BF16), w_ref[...])
    gate = mod_ref[0][:, gate_idx * d:(gate_idx + 1) * d]
    o_ref[...] = h_ref[...] + gate * (_rms(y) * g_ref[...])


def _proj_resid(a, w, hs, g, mod, gate_idx, n_rows, tiles_per_batch, nb, name):
    k, d = w.shape
    tm = ROW_TILE
    return pl.pallas_call(
        functools.partial(_proj_resid_kernel, gate_idx=gate_idx, d=d),
        grid=(n_rows // tm,),
        in_specs=[
            pl.BlockSpec((tm, k), lambda i: (i, 0)),
            pl.BlockSpec((k, d), lambda i: (0, 0)),
            pl.BlockSpec((tm, d), lambda i: (i, 0)),
            pl.BlockSpec((1, d), lambda i: (0, 0)),
            pl.BlockSpec((1, 1, mod.shape[-1]), lambda i: (jnp.minimum(i // tiles_per_batch, nb), 0, 0)),
        ],
        out_specs=pl.BlockSpec((tm, d), lambda i: (i, 0)),
        out_shape=jax.ShapeDtypeStruct((n_rows, d), F32),
        compiler_params=_cparams(("parallel",)),
        name=name,
    )(a, w, hs, g.reshape(1, d), mod)


def _ffn_kernel(h_ref, g2_ref, g3_ref, mod_ref, wg_ref, wu_ref, wo_ref, o_ref, a_scr, acc_scr, *, d):
    j = pl.program_id(1)

    @pl.when(j == 0)
    def _():
        m = mod_ref[0]
        y = _rms(h_ref[...]) * g2_ref[...]
        y = y * (1.0 + m[:, 4 * d:5 * d]) + m[:, 3 * d:4 * d]
        a_scr[...] = y.astype(BF16)
        acc_scr[...] = jnp.zeros_like(acc_scr)

    a = a_scr[...]
    gt = _dot(a, wg_ref[...])
    up = _dot(a, wu_ref[...])
    act = (gt * _sigmoid(gt) * up).astype(BF16)
    acc_scr[...] += _dot(act, wo_ref[...])

    @pl.when(j == pl.num_programs(1) - 1)
    def _():
        m = mod_ref[0]
        o_ref[...] = h_ref[...] + m[:, 5 * d:6 * d] * (_rms(acc_scr[...]) * g3_ref[...])


def _ffn(hs, g2, g3, mod, w_in, w_out, n_rows, tiles_per_batch, nb, name):
    d = hs.shape[1]
    f = w_out.shape[0]
    tm = ROW_TILE
    tf = 512
    nf = f // tf
    return pl.pallas_call(
        functools.partial(_ffn_kernel, d=d),
        grid=(n_rows // tm, nf),
        in_specs=[
            pl.BlockSpec((tm, d), lambda i, j: (i, 0)),
            pl.BlockSpec((1, d), lambda i, j: (0, 0)),
            pl.BlockSpec((1, d), lambda i, j: (0, 0)),
            pl.BlockSpec((1, 1, mod.shape[-1]), lambda i, j: (jnp.minimum(i // tiles_per_batch, nb), 0, 0)),
            pl.BlockSpec((d, tf), lambda i, j: (0, j)),
            pl.BlockSpec((d, tf), lambda i, j: (0, nf + j)),
            pl.BlockSpec((tf, d), lambda i, j: (j, 0)),
        ],
        out_specs=pl.BlockSpec((tm, d), lambda i, j: (i, 0)),
        out_shape=jax.ShapeDtypeStruct((n_rows, d), F32),
        scratch_shapes=[pltpu.VMEM((tm, d), BF16), pltpu.VMEM((tm, d), F32)],
        compiler_params=_cparams(("parallel", "arbitrary")),
        name=name,
    )(hs, g2.reshape(1, d), g3.reshape(1, d), mod, w_in, w_in, w_out)


GLA_CHUNK = 128


def _gla_structure(cs, reverse):
    idx = np.arange(cs)
    ip = cs - 1 - idx if reverse else idx
    ii, jj = ip[:, None], ip[None, :]
    levels = []
    s = cs // 2
    while s >= 1:
        levels.append(((ii // (2 * s)) == (jj // (2 * s))) & ((ii & s) != 0) & ((jj & s) == 0))
        s //= 2
    levels.append(ii == jj)
    return np.stack(levels).astype(np.float32), (jj <= ii).astype(np.float32)


def _gla_scan_kernel(q_ref, k_ref, v_ref, gd_ref, wu_ref, bg_ref, msk_ref, tri_ref, o_ref, st_scr, b_scr,
                     *, cs, dk, reverse):
    @pl.when(pl.program_id(2) == 0)
    def _():
        st_scr[...] = jnp.zeros_like(st_scr)

    q = q_ref[...] * (dk ** -0.5)
    k = k_ref[...]
    v_b = v_ref[...].astype(BF16)

    z = _dot(gd_ref[...].astype(BF16), wu_ref[...]) + bg_ref[...]
    lg = (jnp.minimum(z, 0.0) - jnp.log1p(jnp.exp(-jnp.abs(z)))) * (1.0 / GLA_TAU)

    hi = lg.astype(BF16)
    r1 = lg - hi.astype(F32)
    mid = r1.astype(BF16)
    lo = (r1 - mid.astype(F32)).astype(BF16)
    tri = tri_ref[...]
    b = _dot(tri, hi) + _dot(tri, mid) + _dot(tri, lo)
    b_scr[...] = b
    last = 0 if reverse else cs - 1
    b_last = b_scr[pl.ds(last, 1), :]

    st = st_scr[...]
    o = _dot_nt((q * jnp.exp(b)).astype(BF16), st.astype(BF16))

    row = lax.broadcasted_iota(jnp.int32, (cs, 1), 0)
    ip = (cs - 1 - row) if reverse else row

    def orig(p):
        return cs - 1 - p if reverse else p

    attn = jnp.zeros((cs, cs), F32)
    lvl = 0
    s = cs // 2
    while s >= 1:
        later = (ip & s) != 0
        if s == 1:
            eq = jnp.where(later, lg, 0.0)
            ql = (q * jnp.exp(eq)).astype(BF16)
            kl = k.astype(BF16)
        else:
            if 2 * s >= SUBLANES:
                nblk = cs // (2 * s)
                pieces = []
                for jb in range(nblk):
                    m = nblk - 1 - jb if reverse else jb
                    p = orig(m * 2 * s + s - 1)
                    pieces.append(jnp.broadcast_to(b_scr[pl.ds(p, 1), :], (2 * s, dk)))
                ref = pieces[0] if nblk == 1 else jnp.concatenate(pieces, axis=0)
            else:
                ngrp = cs // SUBLANES
                pa, pb = [], []
                for jg in range(ngrp):
                    gp = ngrp - 1 - jg if reverse else jg
                    pa.append(jnp.broadcast_to(b_scr[pl.ds(orig(gp * SUBLANES + 1), 1), :], (SUBLANES, dk)))
                    pb.append(jnp.broadcast_to(b_scr[pl.ds(orig(gp * SUBLANES + 5), 1), :], (SUBLANES, dk)))
                ref = jnp.where((ip & 4) == 0, jnp.concatenate(pa, axis=0), jnp.concatenate(pb, axis=0))
            dlt = b - ref
            ql = (q * jnp.exp(jnp.where(later, dlt, 0.0))).astype(BF16)
            kl = (k * jnp.exp(jnp.where(later, 0.0, -dlt))).astype(BF16)
        attn = attn + msk_ref[lvl] * _dot_nt(ql, kl)
        lvl += 1
        s //= 2
    attn = attn + msk_ref[lvl] * _dot_nt(q.astype(BF16), k.astype(BF16))

    o_ref[...] = o + _dot(attn.astype(BF16), v_b)

    k_dec = (k * jnp.exp(b_last - b)).astype(BF16)
    st_scr[...] = st * jnp.exp(b_last) + _dot_tn(v_b, k_dec)


def _gla_scan(p, wu_ext, bg, b_sz, t_lat, t_ctx, reverse, name):
    cs = GLA_CHUNK
    d = 2048
    dk = d // 2 // GLA_HEADS
    dv = d // GLA_HEADS
    n_tok = p.shape[0]
    ncc, ncl = t_ctx // cs, t_lat // cs
    ctx_blk0 = (b_sz * t_lat) // cs
    masks, tri = _gla_structure(cs, reverse)
    nlev = masks.shape[0]

    def rowblk(b, s):
        if reverse:
            cc, lc = ncc - 1 - s, ncl - 1 - (s - ncc)
        else:
            cc, lc = s, s - ncc
        return jnp.where(s < ncc, ctx_blk0 + b * ncc + cc, b * ncl + lc)

    kcol0 = (GLA_HEADS * dk) // dk
    vcol0 = (2 * GLA_HEADS * dk) // dv
    gcol = (2 * GLA_HEADS * dk + 2 * GLA_HEADS * dv) // LANES
    return pl.pallas_call(
        functools.partial(_gla_scan_kernel, cs=cs, dk=dk, reverse=reverse),
        grid=(b_sz, GLA_HEADS, ncc + ncl),
        in_specs=[
            pl.BlockSpec((cs, dk), lambda b, h, s: (rowblk(b, s), h)),
            pl.BlockSpec((cs, dk), lambda b, h, s: (rowblk(b, s), kcol0 + h)),
            pl.BlockSpec((cs, dv), lambda b, h, s: (rowblk(b, s), vcol0 + h)),
            pl.BlockSpec((cs, LANES), lambda b, h, s: (rowblk(b, s), gcol)),
            pl.BlockSpec((LANES, dk), lambda b, h, s: (0, h)),
            pl.BlockSpec((1, dk), lambda b, h, s: (0, h)),
            pl.BlockSpec((nlev, cs, cs), lambda b, h, s: (0, 0, 0)),
            pl.BlockSpec((cs, cs), lambda b, h, s: (0, 0)),
        ],
        out_specs=pl.BlockSpec((cs, dv), lambda b, h, s: (rowblk(b, s), h)),
        out_shape=jax.ShapeDtypeStruct((n_tok, GLA_HEADS * dv), F32),
        scratch_shapes=[pltpu.VMEM((dv, dk), F32), pltpu.VMEM((cs, dk), F32)],
        compiler_params=_cparams(("parallel", "parallel", "arbitrary")),
        name=name,
    )(p, p, p, p, wu_ext, bg, jnp.asarray(masks), jnp.asarray(tri, dtype=BF16))


def _gla_combine_kernel(of_ref, ob_ref, r_ref, gh_ref, o_ref, *, dv):
    for h in range(GLA_HEADS):
        sl = slice(h * dv, (h + 1) * dv)
        y = _rms(of_ref[:, sl] + ob_ref[:, sl]) * gh_ref[...]
        r = r_ref[:, sl]
        o_ref[:, sl] = (y * (r * _sigmoid(r))).astype(o_ref.dtype)


def _gla_combine(o_f, o_b, p, g_head, n_rows):
    d = o_f.shape[1]
    dv = d // GLA_HEADS
    tm = ROW_TILE
    rcol = (p.shape[1] // d) - 1
    return pl.pallas_call(
        functools.partial(_gla_combine_kernel, dv=dv),
        grid=(n_rows // tm,),
        in_specs=[
            pl.BlockSpec((tm, d), lambda i: (i, 0)),
            pl.BlockSpec((tm, d), lambda i: (i, 0)),
            pl.BlockSpec((tm, d), lambda i: (i, rcol)),
            pl.BlockSpec((1, dv), lambda i: (0, 0)),
        ],
        out_specs=pl.BlockSpec((tm, d), lambda i: (i, 0)),
        out_shape=jax.ShapeDtypeStruct((n_rows, d), BF16),
        compiler_params=_cparams(("parallel",)),
        name="gla_combine",
    )(o_f, o_b, p, g_head.reshape(1, dv))


def _rope_kernel(x_ref, cos_ref, sin_ref, o_ref, *, n_lat_tiles):
    is_lat = pl.program_id(0) < n_lat_tiles
    cosf = cos_ref[...]
    sinf = sin_ref[...]
    lane = lax.broadcasted_iota(jnp.int32, cosf.shape, 1)
    first = (lane % ROPE_DIM) < (ROPE_DIM // 2)
    for c in range(x_ref.shape[1] // LANES):
        x = x_ref[:, c * LANES:(c + 1) * LANES].astype(F32)
        partner = jnp.where(first, pltpu.roll(x, LANES - ROPE_DIM // 2, 1), pltpu.roll(x, ROPE_DIM // 2, 1))
        o_ref[:, c * LANES:(c + 1) * LANES] = jnp.where(is_lat, x * cosf + partner * sinf, x).astype(o_ref.dtype)


def _rope(x, col, width, cosf, sinf, n_lat_tiles, tiles_per_batch, name):
    n_rows = x.shape[0]
    tm = ROW_TILE
    return pl.pallas_call(
        functools.partial(_rope_kernel, n_lat_tiles=n_lat_tiles),
        grid=(n_rows // tm,),
        in_specs=[
            pl.BlockSpec((tm, width), lambda i: (i, col)),
            pl.BlockSpec((tm, LANES), lambda i: (i % tiles_per_batch, 0)),
            pl.BlockSpec((tm, LANES), lambda i: (i % tiles_per_batch, 0)),
        ],
        out_specs=pl.BlockSpec((tm, width), lambda i: (i, 0)),
        out_shape=jax.ShapeDtypeStruct((n_rows, width), BF16),
        compiler_params=_cparams(("parallel",)),
        name=name,
    )(x, cosf, sinf)


def _rope_tables(t_lat):
    t = jnp.arange(t_lat)
    row = (t // GRID_W).astype(F32)
    col = (t % GRID_W).astype(F32)
    n_freq = ROPE_DIM // 4
    inv_freq = ROPE_THETA ** (-jnp.arange(n_freq, dtype=F32) / n_freq)
    ang = jnp.concatenate([row[:, None] * inv_freq, col[:, None] * inv_freq], axis=-1)
    cos, sin = jnp.cos(ang), jnp.sin(ang)
    reps = LANES // ROPE_DIM
    return jnp.tile(jnp.concatenate([cos, cos], axis=-1), (1, reps)), jnp.tile(jnp.concatenate([-sin, sin], axis=-1), (1, reps))


MLA_TQ = 256
MLA_HEADS_PER_STEP = 2


def _mla_attn_kernel(qn_ref, qr_ref, knc_ref, vc_ref, krc_ref, knl_ref, vl_ref, krl_ref, o_ref, kcat, vs,
                     *, tc, tl, scale):
    t = pl.program_id(2)
    nh = MLA_HEADS_PER_STEP
    rd = ROPE_DIM

    @pl.when(t == 0)
    def _():
        for h in range(nh):
            kcat[h, 0:tc, 0:MLA_NOPE] = knc_ref[:, h * MLA_NOPE:(h + 1) * MLA_NOPE].astype(BF16)
            kcat[h, 0:tc, MLA_NOPE:MLA_NOPE + rd] = krc_ref[:, 0:rd]
            kcat[h, tc:tc + tl, 0:MLA_NOPE] = knl_ref[:, h * MLA_NOPE:(h + 1) * MLA_NOPE].astype(BF16)
            kcat[h, tc:tc + tl, MLA_NOPE:MLA_NOPE + rd] = krl_ref[:, 0:rd]
        vs[0:tc, :] = vc_ref[...].astype(BF16)
        vs[tc:tc + tl, :] = vl_ref[...].astype(BF16)

    def attend(nk):
        for h in range(nh):
            qc = jnp.concatenate([qn_ref[:, h * MLA_NOPE:(h + 1) * MLA_NOPE].astype(BF16),
                                  qr_ref[:, h * rd:(h + 1) * rd]], axis=1)
            s = _dot_nt(qc, kcat[h, 0:nk, :]) * scale
            p = jnp.exp(s - jnp.max(s, axis=-1, keepdims=True))
            l = jnp.sum(p, axis=-1, keepdims=True)
            o = _dot(p.astype(BF16), vs[0:nk, h * MLA_V:(h + 1) * MLA_V]) / l
            o_ref[:, h * MLA_V:(h + 1) * MLA_V] = o.astype(o_ref.dtype)

    @pl.when(t == 0)
    def _():
        attend(tc)

    @pl.when(t > 0)
    def _():
        attend(tc + tl)


def _mla_attention(qn, qr, kv, kr, b_sz, t_lat, t_ctx):
    n_tok, hd = qn.shape
    nh = MLA_HEADS_PER_STEP
    n_heads = hd // MLA_NOPE
    tq = MLA_TQ
    assert t_ctx == tq and t_lat % tq == 0
    nlt = t_lat // tq
    ctx_blk0 = (b_sz * t_lat) // tq
    vcol0 = n_heads // nh
    scale = (MLA_NOPE + ROPE_DIM) ** -0.5

    def qrow(b, t):
        return jnp.where(t == 0, ctx_blk0 + b, b * nlt + t - 1)

    wn, wr = nh * MLA_NOPE, nh * ROPE_DIM
    return pl.pallas_call(
        functools.partial(_mla_attn_kernel, tc=t_ctx, tl=t_lat, scale=scale),
        grid=(b_sz, n_heads // nh, 1 + nlt),
        in_specs=[
            pl.BlockSpec((tq, wn), lambda b, h, t: (qrow(b, t), h)),
            pl.BlockSpec((tq, wr), lambda b, h, t: (qrow(b, t), h)),
            pl.BlockSpec((t_ctx, wn), lambda b, h, t: (ctx_blk0 + b, h)),
            pl.BlockSpec((t_ctx, wn), lambda b, h, t: (ctx_blk0 + b, vcol0 + h)),
            pl.BlockSpec((t_ctx, LANES), lambda b, h, t: (ctx_blk0 + b, 0)),
            pl.BlockSpec((t_lat, wn), lambda b, h, t: (b, h)),
            pl.BlockSpec((t_lat, wn), lambda b, h, t: (b, vcol0 + h)),
            pl.BlockSpec((t_lat, LANES), lambda b, h, t: (b, 0)),
        ],
        out_specs=pl.BlockSpec((tq, nh * MLA_V), lambda b, h, t: (qrow(b, t), h)),
        out_shape=jax.ShapeDtypeStruct((n_tok, n_heads * MLA_V), BF16),
        scratch_shapes=[pltpu.VMEM((nh, t_ctx + t_lat, MLA_NOPE + ROPE_DIM), BF16),
                        pltpu.VMEM((t_ctx + t_lat, nh * MLA_V), BF16)],
        compiler_params=_cparams(("parallel", "parallel", "arbitrary")),
        name="mla_attention",
    )(qn, qr, kv, kv, kr, kv, kv, kr)


def _swa_attn_kernel(sink_ref, q_ref, kc_ref, vc_ref, kp_ref, kc0_ref, kn_ref, vp_ref, vc0_ref, vn_ref, o_ref,
                     *, nct, nb, n_kv, scale):
    t = pl.program_id(1)
    n = t - nct
    blk = WINDOW
    hd = SWA_HEAD_DIM
    tc = kc_ref.shape[0]

    qi = lax.broadcasted_iota(jnp.int32, (blk, 3 * blk), 0)
    kj = lax.broadcasted_iota(jnp.int32, (blk, 3 * blk), 1)
    lo = jnp.where(n > 0, 0, blk)
    hi = jnp.where(t < nct, 0, jnp.where(n < nb - 1, 3 * blk, 2 * blk))
    win_ok = (jnp.abs(qi - kj + blk) <= WINDOW) & (kj >= lo) & (kj < hi)

    for kvh in range(n_kv):
        ks = slice(kvh * hd, (kvh + 1) * hd)
        k_all = jnp.concatenate([kc_ref[:, ks], kp_ref[:, ks], kc0_ref[:, ks], kn_ref[:, ks]], axis=0)
        v_all = jnp.concatenate([vc_ref[:, ks], vp_ref[:, ks], vc0_ref[:, ks], vn_ref[:, ks]], axis=0).astype(BF16)
        outs = []
        for g in range(SWA_GROUP):
            h = kvh * SWA_GROUP + g
            sink = sink_ref[h]
            s = _dot_nt(q_ref[:, h * hd:(h + 1) * hd], k_all) * scale
            s_c = s[:, 0:tc]
            s_w = jnp.where(win_ok, s[:, tc:], NEG_BIG)
            m = jnp.maximum(jnp.maximum(jnp.max(s_c, axis=-1, keepdims=True), jnp.max(s_w, axis=-1, keepdims=True)), sink)
            p_c = jnp.exp(s_c - m)
            p_w = jnp.exp(s_w - m)
            l = jnp.sum(p_c, axis=-1, keepdims=True) + jnp.sum(p_w, axis=-1, keepdims=True) + jnp.exp(sink - m)
            o = _dot(p_c.astype(BF16), v_all[0:tc]) + _dot(p_w.astype(BF16), v_all[tc:])
            outs.append(o / l)
        for g in range(0, SWA_GROUP, 2):
            h = kvh * SWA_GROUP + g
            o_ref[:, h * hd:(h + 2) * hd] = jnp.concatenate(outs[g:g + 2], axis=1).astype(o_ref.dtype)


def _swa_attention(qk, p3, sinks, b_sz, t_lat, t_ctx):
    n_tok = qk.shape[0]
    blk = WINDOW
    hd = SWA_HEAD_DIM
    n_q = sinks.shape[0]
    n_kv = n_q // SWA_GROUP
    kvw = n_kv * hd
    kcol = (n_q * hd) // kvw
    vcol = (n_q * hd + kvw) // kvw
    nct, nb = t_ctx // blk, t_lat // blk
    ctx_blk0 = (b_sz * t_lat) // blk
    ctx_row0 = (b_sz * t_lat) // t_ctx

    def qrow(b, t):
        return jnp.where(t < nct, ctx_blk0 + b * nct + t, b * nb + t - nct)

    def krow(b, t, off):
        return b * nb + jnp.clip(t - nct + off, 0, nb - 1)

    kspec = lambda off: pl.BlockSpec((blk, kvw), lambda b, t: (krow(b, t, off), kcol))
    vspec = lambda off: pl.BlockSpec((blk, kvw), lambda b, t: (krow(b, t, off), vcol))
    return pl.pallas_call(
        functools.partial(_swa_attn_kernel, nct=nct, nb=nb, n_kv=n_kv, scale=hd ** -0.5),
        grid=(b_sz, nct + nb),
        in_specs=[
            pl.BlockSpec(memory_space=pltpu.SMEM),
            pl.BlockSpec((blk, n_q * hd), lambda b, t: (qrow(b, t), 0)),
            pl.BlockSpec((t_ctx, kvw), lambda b, t: (ctx_row0 + b, kcol)),
            pl.BlockSpec((t_ctx, kvw), lambda b, t: (ctx_row0 + b, vcol)),
            kspec(-1), kspec(0), kspec(1), vspec(-1), vspec(0), vspec(1),
        ],
        out_specs=pl.BlockSpec((blk, n_q * hd), lambda b, t: (qrow(b, t), 0)),
        out_shape=jax.ShapeDtypeStruct((n_tok, n_q * hd), BF16),
        compiler_params=_cparams(("parallel", "arbitrary")),
        name="swa_attention",
    )(sinks, qk, qk, p3, qk, qk, qk, p3, p3, p3)


def kernel(x, c, ctx, c_ctx, w_ada, b_ada, g_norm, w_ffn_in, w_ffn_out, gla_w_in, gla_w_gate_down, gla_w_gate_up, gla_b_gate, gla_g_head, gla_w_out, mla_w_in, mla_g_q, mla_w_uq, mla_g_kv, mla_w_ukv, mla_w_out, swa_w_in, swa_sinks, swa_w_out):
    b_sz, t_lat, d = x.shape
    t_ctx = ctx.shape[1]
    depth = w_ada.shape[0]
    n_lat, n_ctx = b_sz * t_lat, b_sz * t_ctx
    n_tok = n_lat + n_ctx
    tm = ROW_TILE
    assert t_lat % tm == 0 and n_ctx % tm == 0 and b_sz < MOD_ROWS
    tpb = t_lat // tm
    n_lat_tiles = n_lat // tm

    hs = jnp.concatenate([x.reshape(n_lat, d), ctx.reshape(n_ctx, d)], axis=0)
    cond = jnp.zeros((MOD_ROWS, d), F32).at[:b_sz].set(c).at[b_sz].set(c_ctx)
    mod_all = _modulation(cond, w_ada, b_ada)
    cosf, sinf = _rope_tables(t_lat)
    common = dict(tiles_per_batch=tpb, nb=b_sz)

    for i in range(depth):
        kind, j = i % 3, i // 3
        last = i == depth - 1
        n_out = n_lat if last else n_tok
        mod = mod_all[i].reshape(MOD_ROWS, 1, 6 * d)

        if kind == 0:
            rank = GLA_GATE_RANK
            qk_w = gla_w_gate_up.shape[-1]
            pad = LANES - 2 * rank
            w_ext = jnp.concatenate([gla_w_in[j], gla_w_gate_down[j, 0], gla_w_gate_down[j, 1],
                                     jnp.zeros((d, pad), F32)], axis=1).astype(BF16)
            p = _nm_matmul(hs, 0, d, g_norm[i, 0], w_ext, 896, F32, "gla_in_proj", mod=mod, **common)
            outs = []
            for dr in range(2):
                wu_ext = jnp.zeros((LANES, qk_w), F32).at[dr * rank:(dr + 1) * rank].set(gla_w_gate_up[j, dr]).astype(BF16)
                outs.append(_gla_scan(p, wu_ext, gla_b_gate[j, dr].reshape(1, qk_w), b_sz, t_lat, t_ctx,
                                      reverse=bool(dr), name="gla_scan_bwd" if dr else "gla_scan_fwd"))
            a = _gla_combine(outs[0], outs[1], p, gla_g_head[j], n_out)
            w_o = gla_w_out[j]
        elif kind == 1:
            n_heads = mla_w_out.shape[1] // MLA_V
            w_in = jnp.concatenate([mla_w_in[j], jnp.zeros((d, LANES - ROPE_DIM), F32)], axis=1).astype(BF16)
            p1 = _nm_matmul(hs, 0, d, g_norm[i, 0], w_in, w_in.shape[1], F32, "mla_in_proj", mod=mod, **common)
            w_uq = mla_w_uq[j].reshape(MLA_Q_RANK, n_heads, MLA_NOPE + ROPE_DIM)
            w_uq_n = w_uq[:, :, :MLA_NOPE].reshape(MLA_Q_RANK, n_heads * MLA_NOPE).astype(BF16)
            w_uq_r = w_uq[:, :, MLA_NOPE:].reshape(MLA_Q_RANK, n_heads * ROPE_DIM).astype(BF16)
            w_ukv = mla_w_ukv[j].reshape(MLA_KV_RANK, n_heads, MLA_NOPE + MLA_V)
            w_ukv = jnp.concatenate([w_ukv[:, :, :MLA_NOPE].reshape(MLA_KV_RANK, -1),
                                     w_ukv[:, :, MLA_NOPE:].reshape(MLA_KV_RANK, -1)], axis=1).astype(BF16)
            qn = _nm_matmul(p1, 0, MLA_Q_RANK, mla_g_q[j], w_uq_n, 1024, BF16, "mla_q_nope")
            qr = _nm_matmul(p1, 0, MLA_Q_RANK, mla_g_q[j], w_uq_r, 1024, F32, "mla_q_rope")
            kv = _nm_matmul(p1, 1, MLA_KV_RANK, mla_g_kv[j], w_ukv, 1024, BF16, "mla_kv")
            qr = _rope(qr, 0, qr.shape[1], cosf, sinf, n_lat_tiles, tpb, "mla_rope_q")
            kr = _rope(p1, (MLA_Q_RANK + MLA_KV_RANK) // LANES, LANES, cosf, sinf, n_lat_tiles, tpb, "mla_rope_k")
            a = _mla_attention(qn, qr, kv, kr, b_sz, t_lat, t_ctx)
            w_o = mla_w_out[j]
        else:
            n_q = swa_sinks.shape[1]
            qkw = (n_q + n_q // SWA_GROUP) * SWA_HEAD_DIM
            p3 = _nm_matmul(hs, 0, d, g_norm[i, 0], swa_w_in[j].astype(BF16), 1280, F32, "swa_in_proj", mod=mod, **common)
            qk = _rope(p3, 0, qkw, cosf, sinf, n_lat_tiles, tpb, "swa_rope")
            a = _swa_attention(qk, p3, swa_sinks[j], b_sz, t_lat, t_ctx)
            w_o = swa_w_out[j]

        hs = _proj_resid(a, w_o.astype(BF16), hs, g_norm[i, 1], mod, 2, n_out, name="mixer_out_proj", **common)
        hs = _ffn(hs, g_norm[i, 2], g_norm[i, 3], mod, w_ffn_in[i].astype(BF16), w_ffn_out[i].astype(BF16),
                  n_out, name="ffn", **common)

    return hs.reshape(b_sz, t_lat, d)
```

```python
import functools

import numpy as np
import jax
import jax.numpy as jnp
from jax import lax
from jax.experimental import pallas as pl
from jax.experimental.pallas import tpu as pltpu

F32 = jnp.float32
BF16 = jnp.bfloat16

EPS = 1e-6
ROPE_THETA = 10000.0
ROPE_DIM = 64
GRID_W = 64
WINDOW = 128
GLA_HEADS = 4
GLA_GATE_RANK = 16
GLA_TAU = 16.0
MLA_Q_RANK = 512
MLA_KV_RANK = 512
MLA_NOPE = 128
MLA_V = 128
SWA_HEAD_DIM = 64
SWA_GROUP = 8

LANES = 128
SUBLANES = 8
VMEM_LIMIT = 56 * 1024 * 1024
ROW_TILE = 512
MOD_ROWS = 16
NEG_BIG = -1e30


def _cparams(sem):
    return pltpu.CompilerParams(dimension_semantics=sem, vmem_limit_bytes=VMEM_LIMIT)


def _dot(a, b):
    return jnp.dot(a, b, preferred_element_type=F32)


def _dot_nt(a, b):
    return lax.dot_general(a, b, (((1,), (1,)), ((), ())), preferred_element_type=F32)


def _dot_tn(a, b):
    return lax.dot_general(a, b, (((0,), (0,)), ((), ())), preferred_element_type=F32)


def _sigmoid(x):
    return 1.0 / (1.0 + jnp.exp(-x))


def _rms(x):
    return x * lax.rsqrt(jnp.mean(x * x, axis=-1, keepdims=True) + EPS)


def _mod_kernel(c_ref, w_ref, b_ref, o_ref):
    c = c_ref[...]
    s = (c * _sigmoid(c)).astype(BF16)
    o_ref[0] = _dot(s, w_ref[0].astype(BF16)) + b_ref[0]


def _modulation(cond, w_ada, b_ada):
    depth, d, n = w_ada.shape
    tn = 1024
    return pl.pallas_call(
        _mod_kernel,
        grid=(depth, n // tn),
        in_specs=[
            pl.BlockSpec((MOD_ROWS, d), lambda l, j: (0, 0)),
            pl.BlockSpec((1, d, tn), lambda l, j: (l, 0, j)),
            pl.BlockSpec((1, 1, tn), lambda l, j: (l, 0, j)),
        ],
        out_specs=pl.BlockSpec((1, MOD_ROWS, tn), lambda l, j: (l, 0, j)),
        out_shape=jax.ShapeDtypeStruct((depth, MOD_ROWS, n), F32),
        compiler_params=_cparams(("parallel", "parallel")),
        name="modulation",
    )(cond, w_ada, b_ada.reshape(depth, 1, n))


def _nm_matmul_kernel(*refs, modulate, shift_idx, scale_idx, d, out_scale):
    if modulate:
        x_ref, g_ref, mod_ref, w_ref, o_ref, a_scr = refs
    else:
        x_ref, g_ref, w_ref, o_ref, a_scr = refs

    @pl.when(pl.program_id(1) == 0)
    def _():
        y = _rms(x_ref[...].astype(F32)) * g_ref[...]
        if modulate:
            m = mod_ref[0]
            y = y * (1.0 + m[:, scale_idx * d:(scale_idx + 1) * d]) + m[:, shift_idx * d:(shift_idx + 1) * d]
        a_scr[...] = y.astype(BF16)

    o = _dot(a_scr[...], w_ref[...])
    if out_scale != 1.0:
        o = o * out_scale
    o_ref[...] = o.astype(o_ref.dtype)


def _nm_matmul(x, xcol, k, g, w, tn, out_dtype, name, mod=None, shift_idx=0, scale_idx=1, tiles_per_batch=1, nb=0,
               out_scale=1.0):
    n_rows = x.shape[0]
    n = w.shape[1]
    tm = ROW_TILE
    modulate = mod is not None
    in_specs = [pl.BlockSpec((tm, k), lambda i, j: (i, xcol)),
                pl.BlockSpec((1, k), lambda i, j: (0, 0))]
    args = [x, g.reshape(1, k)]
    if modulate:
        in_specs.append(pl.BlockSpec((1, 1, mod.shape[-1]),
                                     lambda i, j: (jnp.minimum(i // tiles_per_batch, nb), 0, 0)))
        args.append(mod)
    in_specs.append(pl.BlockSpec((k, tn), lambda i, j: (0, j)))
    args.append(w)
    return pl.pallas_call(
        functools.partial(_nm_matmul_kernel, modulate=modulate, shift_idx=shift_idx, scale_idx=scale_idx, d=k,
                          out_scale=out_scale),
        grid=(n_rows // tm, n // tn),
        in_specs=in_specs,
        out_specs=pl.BlockSpec((tm, tn), lambda i, j: (i, j)),
        out_shape=jax.ShapeDtypeStruct((n_rows, n), out_dtype),
        scratch_shapes=[pltpu.VMEM((tm, k), BF16)],
        compiler_params=_cparams(("parallel", "arbitrary")),
        name=name,
    )(*args)


def _proj_resid_kernel(a_ref, w_ref, h_ref, g_ref, mod_ref, o_ref, *, gate_idx, d):
    y = _dot(a_ref[...].astype(BF16), w_ref[...])
    gate = mod_ref[0][:, gate_idx * d:(gate_idx + 1) * d]
    o_ref[...] = h_ref[...] + gate * (_rms(y) * g_ref[...])


def _proj_resid(a, w, hs, g, mod, gate_idx, n_rows, tiles_per_batch, nb, name):
    k, d = w.shape
    tm = ROW_TILE
    return pl.pallas_call(
        functools.partial(_proj_resid_kernel, gate_idx=gate_idx, d=d),
        grid=(n_rows // tm,),
        in_specs=[
            pl.BlockSpec((tm, k), lambda i: (i, 0)),
            pl.BlockSpec((k, d), lambda i: (0, 0)),
            pl.BlockSpec((tm, d), lambda i: (i, 0)),
            pl.BlockSpec((1, d), lambda i: (0, 0)),
            pl.BlockSpec((1, 1, mod.shape[-1]), lambda i: (jnp.minimum(i // tiles_per_batch, nb), 0, 0)),
        ],
        out_specs=pl.BlockSpec((tm, d), lambda i: (i, 0)),
        out_shape=jax.ShapeDtypeStruct((n_rows, d), F32),
        compiler_params=_cparams(("parallel",)),
        name=name,
    )(a, w, hs, g.reshape(1, d), mod)


def _ffn_kernel(h_ref, g2_ref, g3_ref, mod_ref, wg_ref, wu_ref, wo_ref, o_ref, a_scr, acc_scr, *, d):
    j = pl.program_id(1)

    @pl.when(j == 0)
    def _():
        m = mod_ref[0]
        y = _rms(h_ref[...]) * g2_ref[...]
        y = y * (1.0 + m[:, 4 * d:5 * d]) + m[:, 3 * d:4 * d]
        a_scr[...] = y.astype(BF16)
        acc_scr[...] = jnp.zeros_like(acc_scr)

    a = a_scr[...]
    gt = _dot(a, wg_ref[...])
    up = _dot(a, wu_ref[...])
    act = (gt * _sigmoid(gt) * up).astype(BF16)
    acc_scr[...] += _dot(act, wo_ref[...])

    @pl.when(j == pl.num_programs(1) - 1)
    def _():
        m = mod_ref[0]
        o_ref[...] = h_ref[...] + m[:, 5 * d:6 * d] * (_rms(acc_scr[...]) * g3_ref[...])


def _ffn(hs, g2, g3, mod, w_in, w_out, n_rows, tiles_per_batch, nb, name):
    d = hs.shape[1]
    f = w_out.shape[0]
    tm = ROW_TILE
    tf = 512
    nf = f // tf
    return pl.pallas_call(
        functools.partial(_ffn_kernel, d=d),
        grid=(n_rows // tm, nf),
        in_specs=[
            pl.BlockSpec((tm, d), lambda i, j: (i, 0)),
            pl.BlockSpec((1, d), lambda i, j: (0, 0)),
            pl.BlockSpec((1, d), lambda i, j: (0, 0)),
            pl.BlockSpec((1, 1, mod.shape[-1]), lambda i, j: (jnp.minimum(i // tiles_per_batch, nb), 0, 0)),
            pl.BlockSpec((d, tf), lambda i, j: (0, j)),
            pl.BlockSpec((d, tf), lambda i, j: (0, nf + j)),
            pl.BlockSpec((tf, d), lambda i, j: (j, 0)),
        ],
        out_specs=pl.BlockSpec((tm, d), lambda i, j: (i, 0)),
        out_shape=jax.ShapeDtypeStruct((n_rows, d), F32),
        scratch_shapes=[pltpu.VMEM((tm, d), BF16), pltpu.VMEM((tm, d), F32)],
        compiler_params=_cparams(("parallel", "arbitrary")),
        name=name,
    )(hs, g2.reshape(1, d), g3.reshape(1, d), mod, w_in, w_in, w_out)


GLA_CHUNK = 128


def _gla_structure(cs, reverse):
    idx = np.arange(cs)
    ip = cs - 1 - idx if reverse else idx
    ii, jj = ip[:, None], ip[None, :]
    levels = []
    s = cs // 2
    while s >= 1:
        levels.append(((ii // (2 * s)) == (jj // (2 * s))) & ((ii & s) != 0) & ((jj & s) == 0))
        s //= 2
    levels.append(ii == jj)
    return np.stack(levels).astype(np.float32), (jj <= ii).astype(np.float32)


def _gla_chunk(q, k, v_b, lg, msk_ref, tri, st_ref, b_ref, *, cs, dk, reverse):
    hi = lg.astype(BF16)
    r1 = lg - hi.astype(F32)
    mid = r1.astype(BF16)
    lo = (r1 - mid.astype(F32)).astype(BF16)
    b = _dot(tri, hi) + _dot(tri, mid) + _dot(tri, lo)
    b_ref[...] = b
    last = 0 if reverse else cs - 1
    b_last = b_ref[pl.ds(last, 1), :]

    st = st_ref[...]
    o = _dot_nt((q * jnp.exp(b)).astype(BF16), st.astype(BF16))

    row = lax.broadcasted_iota(jnp.int32, (cs, 1), 0)
    ip = (cs - 1 - row) if reverse else row

    def orig(p):
        return cs - 1 - p if reverse else p

    attn = jnp.zeros((cs, cs), F32)
    lvl = 0
    s = cs // 2
    while s >= 1:
        later = (ip & s) != 0
        if s == 1:
            eq = jnp.where(later, lg, 0.0)
            ql = (q * jnp.exp(eq)).astype(BF16)
            kl = k.astype(BF16)
        else:
            if 2 * s >= SUBLANES:
                nblk = cs // (2 * s)
                pieces = []
                for jb in range(nblk):
                    m = nblk - 1 - jb if reverse else jb
                    p = orig(m * 2 * s + s - 1)
                    pieces.append(jnp.broadcast_to(b_ref[pl.ds(p, 1), :], (2 * s, dk)))
                ref = pieces[0] if nblk == 1 else jnp.concatenate(pieces, axis=0)
            else:
                ngrp = cs // SUBLANES
                pa, pb = [], []
                for jg in range(ngrp):
                    gp = ngrp - 1 - jg if reverse else jg
                    pa.append(jnp.broadcast_to(b_ref[pl.ds(orig(gp * SUBLANES + 1), 1), :], (SUBLANES, dk)))
                    pb.append(jnp.broadcast_to(b_ref[pl.ds(orig(gp * SUBLANES + 5), 1), :], (SUBLANES, dk)))
                ref = jnp.where((ip & 4) == 0, jnp.concatenate(pa, axis=0), jnp.concatenate(pb, axis=0))
            dlt = b - ref
            ql = (q * jnp.exp(jnp.where(later, dlt, 0.0))).astype(BF16)
            kl = (k * jnp.exp(jnp.where(later, 0.0, -dlt))).astype(BF16)
        attn = attn + msk_ref[lvl] * _dot_nt(ql, kl)
        lvl += 1
        s //= 2
    attn = attn + msk_ref[lvl] * _dot_nt(q.astype(BF16), k.astype(BF16))

    k_dec = (k * jnp.exp(b_last - b)).astype(BF16)
    st_ref[...] = st * jnp.exp(b_last) + _dot_tn(v_b, k_dec)
    return o + _dot(attn.astype(BF16), v_b)


def _gla_scan_kernel(qf_ref, kf_ref, vf_ref, gf_ref, qb_ref, kb_ref, vb_ref, gb_ref, wu_ref, bg_ref,
                     msk_ref, tri_ref, of_ref, ob_ref, st_scr, b_scr, *, cs, dk, dv):
    @pl.when(pl.program_id(1) == 0)
    def _():
        st_scr[...] = jnp.zeros_like(st_scr)

    dirs = ((qf_ref, kf_ref, vf_ref, gf_ref, of_ref), (qb_ref, kb_ref, vb_ref, gb_ref, ob_ref))
    for dr, (q_ref, k_ref, v_ref, gd_ref, o_ref) in enumerate(dirs):
        z = _dot(gd_ref[...].astype(BF16), wu_ref[dr]) + bg_ref[dr]
        lg_all = (jnp.minimum(z, 0.0) - jnp.log1p(jnp.exp(-jnp.abs(z)))) * (1.0 / GLA_TAU)
        for h in range(GLA_HEADS):
            ks = slice(h * dk, (h + 1) * dk)
            vs = slice(h * dv, (h + 1) * dv)
            o_ref[:, vs] = _gla_chunk(q_ref[:, ks] * (dk ** -0.5), k_ref[:, ks], v_ref[:, vs].astype(BF16),
                                      lg_all[:, ks], msk_ref.at[dr], tri_ref[dr], st_scr.at[dr, h], b_scr.at[dr, h],
                                      cs=cs, dk=dk, reverse=bool(dr))


def _gla_scan(p, wu_ext, bg, b_sz, t_lat, t_ctx):
    cs = GLA_CHUNK
    qkw = wu_ext.shape[-1]
    dk = qkw // GLA_HEADS
    vw = 2 * qkw
    dv = vw // GLA_HEADS
    n_tok = p.shape[0]
    ncc, ncl = t_ctx // cs, t_lat // cs
    ctx_blk0 = (b_sz * t_lat) // cs
    structs = [_gla_structure(cs, rev) for rev in (False, True)]
    masks = np.stack([m for m, _ in structs])
    tri = np.stack([t for _, t in structs])

    def rowblk(rev):
        def f(b, s):
            if rev:
                cc, lc = ncc - 1 - s, ncl - 1 - (s - ncc)
            else:
                cc, lc = s, s - ncc
            return jnp.where(s < ncc, ctx_blk0 + b * ncc + cc, b * ncl + lc)
        return f

    gcol = (2 * qkw + 2 * vw) // LANES

    def dir_specs(rev):
        rb = rowblk(rev)
        return [pl.BlockSpec((cs, qkw), lambda b, s: (rb(b, s), 0)),
                pl.BlockSpec((cs, qkw), lambda b, s: (rb(b, s), 1)),
                pl.BlockSpec((cs, vw), lambda b, s: (rb(b, s), 1)),
                pl.BlockSpec((cs, LANES), lambda b, s: (rb(b, s), gcol))]

    return pl.pallas_call(
        functools.partial(_gla_scan_kernel, cs=cs, dk=dk, dv=dv),
        grid=(b_sz, ncc + ncl),
        in_specs=dir_specs(False) + dir_specs(True) + [
            pl.BlockSpec((2, LANES, qkw), lambda b, s: (0, 0, 0)),
            pl.BlockSpec((2, 1, qkw), lambda b, s: (0, 0, 0)),
            pl.BlockSpec(masks.shape, lambda b, s: (0, 0, 0, 0)),
            pl.BlockSpec(tri.shape, lambda b, s: (0, 0, 0)),
        ],
        out_specs=[pl.BlockSpec((cs, vw), lambda b, s: (rowblk(False)(b, s), 0)),
                   pl.BlockSpec((cs, vw), lambda b, s: (rowblk(True)(b, s), 0))],
        out_shape=[jax.ShapeDtypeStruct((n_tok, vw), F32)] * 2,
        scratch_shapes=[pltpu.VMEM((2, GLA_HEADS, dv, dk), F32), pltpu.VMEM((2, GLA_HEADS, cs, dk), F32)],
        compiler_params=_cparams(("parallel", "arbitrary")),
        name="gla_scan",
    )(p, p, p, p, p, p, p, p, wu_ext, bg, jnp.asarray(masks), jnp.asarray(tri, dtype=BF16))


def _gla_combine_kernel(of_ref, ob_ref, r_ref, gh_ref, o_ref, *, dv):
    for h in range(GLA_HEADS):
        sl = slice(h * dv, (h + 1) * dv)
        y = _rms(of_ref[:, sl] + ob_ref[:, sl]) * gh_ref[...]
        r = r_ref[:, sl]
        o_ref[:, sl] = (y * (r * _sigmoid(r))).astype(o_ref.dtype)


def _gla_combine(o_f, o_b, p, g_head, n_rows):
    d = o_f.shape[1]
    dv = d // GLA_HEADS
    tm = ROW_TILE
    rcol = (p.shape[1] // d) - 1
    return pl.pallas_call(
        functools.partial(_gla_combine_kernel, dv=dv),
        grid=(n_rows // tm,),
        in_specs=[
            pl.BlockSpec((tm, d), lambda i: (i, 0)),
            pl.BlockSpec((tm, d), lambda i: (i, 0)),
            pl.BlockSpec((tm, d), lambda i: (i, rcol)),
            pl.BlockSpec((1, dv), lambda i: (0, 0)),
        ],
        out_specs=pl.BlockSpec((tm, d), lambda i: (i, 0)),
        out_shape=jax.ShapeDtypeStruct((n_rows, d), BF16),
        compiler_params=_cparams(("parallel",)),
        name="gla_combine",
    )(o_f, o_b, p, g_head.reshape(1, dv))


def _rope_kernel(x_ref, cos_ref, sin_ref, o_ref, *, n_lat_tiles, n_roped, n_scaled, out_scale):
    is_lat = pl.program_id(0) < n_lat_tiles
    cosf = cos_ref[...]
    sinf = sin_ref[...]
    lane = lax.broadcasted_iota(jnp.int32, cosf.shape, 1)
    first = (lane % ROPE_DIM) < (ROPE_DIM // 2)
    for c in range(x_ref.shape[1] // LANES):
        y = x_ref[:, c * LANES:(c + 1) * LANES].astype(F32)
        if c < n_roped:
            partner = jnp.where(first, pltpu.roll(y, LANES - ROPE_DIM // 2, 1), pltpu.roll(y, ROPE_DIM // 2, 1))
            y = jnp.where(is_lat, y * cosf + partner * sinf, y)
        if c < n_scaled:
            y = y * out_scale
        o_ref[:, c * LANES:(c + 1) * LANES] = y.astype(o_ref.dtype)


def _rope(x, col, width, cosf, sinf, n_lat_tiles, tiles_per_batch, name, n_roped=None, n_scaled=0, out_scale=1.0):
    n_rows = x.shape[0]
    tm = ROW_TILE
    n_roped = width // LANES if n_roped is None else n_roped
    return pl.pallas_call(
        functools.partial(_rope_kernel, n_lat_tiles=n_lat_tiles, n_roped=n_roped, n_scaled=n_scaled,
                          out_scale=out_scale),
        grid=(n_rows // tm,),
        in_specs=[
            pl.BlockSpec((tm, width), lambda i: (i, col)),
            pl.BlockSpec((tm, LANES), lambda i: (i % tiles_per_batch, 0)),
            pl.BlockSpec((tm, LANES), lambda i: (i % tiles_per_batch, 0)),
        ],
        out_specs=pl.BlockSpec((tm, width), lambda i: (i, 0)),
        out_shape=jax.ShapeDtypeStruct((n_rows, width), BF16),
        compiler_params=_cparams(("parallel",)),
        name=name,
    )(x, cosf, sinf)


def _rope_tables(t_lat):
    t = jnp.arange(t_lat)
    row = (t // GRID_W).astype(F32)
    col = (t % GRID_W).astype(F32)
    n_freq = ROPE_DIM // 4
    inv_freq = ROPE_THETA ** (-jnp.arange(n_freq, dtype=F32) / n_freq)
    ang = jnp.concatenate([row[:, None] * inv_freq, col[:, None] * inv_freq], axis=-1)
    cos, sin = jnp.cos(ang), jnp.sin(ang)
    reps = LANES // ROPE_DIM
    return jnp.tile(jnp.concatenate([cos, cos], axis=-1), (1, reps)), jnp.tile(jnp.concatenate([-sin, sin], axis=-1), (1, reps))


MLA_TQ = 512
MLA_TK = 512
MLA_HEADS_PER_STEP = 2
LOG2E = 1.4426950408889634


def _mla_attn_kernel(qn_ref, qr_ref, qnc_ref, qrc_ref, knc_ref, vc_ref, krc_ref, knl_ref, vl_ref, krl_ref,
                     o_ref, oc_ref, kcat, vext, *, tc, tl):
    t = pl.program_id(2)
    nh = MLA_HEADS_PER_STEP
    rd = ROPE_DIM
    vw = MLA_V + LANES

    def qcat(n_ref, r_ref, h):
        return jnp.concatenate([n_ref[:, h * MLA_NOPE:(h + 1) * MLA_NOPE], r_ref[:, h * rd:(h + 1) * rd]], axis=1)

    def finish(acc):
        return acc[:, 0:MLA_V] / acc[:, MLA_V:vw]

    @pl.when(t == 0)
    def _():
        for h in range(nh):
            ns = slice(h * MLA_NOPE, (h + 1) * MLA_NOPE)
            kcat[h, 0:tc, 0:MLA_NOPE] = knc_ref[:, ns]
            kcat[h, 0:tc, MLA_NOPE:MLA_NOPE + rd] = krc_ref[:, 0:rd]
            kcat[h, tc:tc + tl, 0:MLA_NOPE] = knl_ref[:, ns]
            kcat[h, tc:tc + tl, MLA_NOPE:MLA_NOPE + rd] = krl_ref[:, 0:rd]
            vext[0:tc, h * vw:h * vw + MLA_V] = vc_ref[:, h * MLA_V:(h + 1) * MLA_V]
            vext[tc:tc + tl, h * vw:h * vw + MLA_V] = vl_ref[:, h * MLA_V:(h + 1) * MLA_V]
            vext[:, h * vw + MLA_V:(h + 1) * vw] = jnp.ones((tc + tl, LANES), BF16)
        for h in range(nh):
            s = _dot_nt(qcat(qnc_ref, qrc_ref, h), kcat[h, 0:tc, :])
            p = jnp.exp2(s - jnp.max(s, axis=-1, keepdims=True))
            acc = _dot(p.astype(BF16), vext[0:tc, h * vw:(h + 1) * vw])
            oc_ref[:, h * MLA_V:(h + 1) * MLA_V] = finish(acc).astype(oc_ref.dtype)

    for h in range(nh):
        qc = qcat(qn_ref, qr_ref, h)
        s = _dot_nt(qc, kcat[h, 0:tc, :])
        m = jnp.max(s, axis=-1, keepdims=True)
        acc = _dot(jnp.exp2(s - m).astype(BF16), vext[0:tc, h * vw:(h + 1) * vw])
        for c0 in range(tc, tc + tl, MLA_TK):
            s = _dot_nt(qc, kcat[h, c0:c0 + MLA_TK, :])
            m_new = jnp.maximum(m, jnp.max(s, axis=-1, keepdims=True))
            acc = jnp.exp2(m - m_new) * acc + _dot(jnp.exp2(s - m_new).astype(BF16),
                                                   vext[c0:c0 + MLA_TK, h * vw:(h + 1) * vw])
            m = m_new
        o_ref[:, h * MLA_V:(h + 1) * MLA_V] = finish(acc).astype(o_ref.dtype)


def _mla_attention(qn, qr, kv, kr, b_sz, t_lat, t_ctx):
    hd = qn.shape[1]
    nh = MLA_HEADS_PER_STEP
    n_heads = hd // MLA_NOPE
    tq = MLA_TQ
    assert t_lat % tq == 0 and t_lat % MLA_TK == 0
    nlt = t_lat // tq
    ctx_blk0 = (b_sz * t_lat) // t_ctx
    vcol0 = n_heads // nh
    wn, wr = nh * MLA_NOPE, nh * ROPE_DIM
    return pl.pallas_call(
        functools.partial(_mla_attn_kernel, tc=t_ctx, tl=t_lat),
        grid=(b_sz, n_heads // nh, nlt),
        in_specs=[
            pl.BlockSpec((tq, wn), lambda b, h, t: (b * nlt + t, h)),
            pl.BlockSpec((tq, wr), lambda b, h, t: (b * nlt + t, h)),
            pl.BlockSpec((t_ctx, wn), lambda b, h, t: (ctx_blk0 + b, h)),
            pl.BlockSpec((t_ctx, wr), lambda b, h, t: (ctx_blk0 + b, h)),
            pl.BlockSpec((t_ctx, wn), lambda b, h, t: (ctx_blk0 + b, h)),
            pl.BlockSpec((t_ctx, wn), lambda b, h, t: (ctx_blk0 + b, vcol0 + h)),
            pl.BlockSpec((t_ctx, LANES), lambda b, h, t: (ctx_blk0 + b, 0)),
            pl.BlockSpec((t_lat, wn), lambda b, h, t: (b, h)),
            pl.BlockSpec((t_lat, wn), lambda b, h, t: (b, vcol0 + h)),
            pl.BlockSpec((t_lat, LANES), lambda b, h, t: (b, 0)),
        ],
        out_specs=[pl.BlockSpec((tq, nh * MLA_V), lambda b, h, t: (b * nlt + t, h)),
                   pl.BlockSpec((t_ctx, nh * MLA_V), lambda b, h, t: (b, h))],
        out_shape=[jax.ShapeDtypeStruct((b_sz * t_lat, n_heads * MLA_V), BF16),
                   jax.ShapeDtypeStruct((b_sz * t_ctx, n_heads * MLA_V), BF16)],
        scratch_shapes=[pltpu.VMEM((nh, t_ctx + t_lat, MLA_NOPE + ROPE_DIM), BF16),
                        pltpu.VMEM((t_ctx + t_lat, nh * (MLA_V + LANES)), BF16)],
        compiler_params=_cparams(("parallel", "parallel", "arbitrary")),
        name="mla_attention",
    )(qn, qr, qn, qr, kv, kv, kr, kv, kv, kr)


def _swa_attn_kernel(sink_ref, q_ref, kc_ref, vc_ref, kp_ref, kc0_ref, kn_ref, vp_ref, vc0_ref, vn_ref, o_ref,
                     *, nct, nb, n_kv):
    t = pl.program_id(1)
    n = t - nct
    blk = WINDOW
    hd = SWA_HEAD_DIM
    tc = kc_ref.shape[0]
    nk = tc + 3 * blk
    npair = SWA_GROUP // 2

    qi = lax.broadcasted_iota(jnp.int32, (blk, 3 * blk), 0)
    kj = lax.broadcasted_iota(jnp.int32, (blk, 3 * blk), 1)
    lo = jnp.where(n > 0, 0, blk)
    hi = jnp.where(t < nct, 0, jnp.where(n < nb - 1, 3 * blk, 2 * blk))
    win_ok = (jnp.abs(qi - kj + blk) <= WINDOW) & (kj >= lo) & (kj < hi)
    bias = jnp.concatenate([jnp.zeros((blk, tc), F32), jnp.where(win_ok, 0.0, NEG_BIG)], axis=1)
    bias = jnp.concatenate([bias] * npair, axis=0)
    lane = lax.broadcasted_iota(jnp.int32, (blk, 2 * hd), 1)
    zeros = jnp.zeros((nk, hd), BF16)
    ones = jnp.ones((nk, hd), BF16)

    for kvh in range(n_kv):
        ks = slice(kvh * hd, (kvh + 1) * hd)
        k_all = jnp.concatenate([kc_ref[:, ks], kp_ref[:, ks], kc0_ref[:, ks], kn_ref[:, ks]], axis=0)
        v_all = jnp.concatenate([vc_ref[:, ks], vp_ref[:, ks], vc0_ref[:, ks], vn_ref[:, ks]], axis=0)
        k_bd = jnp.concatenate([jnp.concatenate([k_all, zeros], axis=1),
                                jnp.concatenate([zeros, k_all], axis=1)], axis=0)
        h0 = kvh * SWA_GROUP
        q_st = jnp.concatenate([q_ref[:, (h0 + 2 * j) * hd:(h0 + 2 * j + 2) * hd] for j in range(npair)], axis=0)
        s = _dot_nt(q_st, k_bd)
        ps, ms = [], []
        for e in range(2):
            sink_col = jnp.concatenate([jnp.full((blk, 1), sink_ref[h0 + 2 * j + e] * LOG2E, F32)
                                        for j in range(npair)], axis=0)
            s_e = s[:, e * nk:(e + 1) * nk] + bias
            m = jnp.maximum(jnp.max(s_e, axis=-1, keepdims=True), sink_col)
            p = jnp.exp2(s_e - m).astype(BF16)
            ps += [p[:, 0:tc], p[:, tc:]]
            ms.append(sink_col - m)

        def v_ext(r0, r1, e):
            v, z, o = v_all[r0:r1], zeros[r0:r1], ones[r0:r1]
            return jnp.concatenate([z, v, z, o] if e else [v, z, o, z], axis=1)

        tail = tc + 2 * blk
        acc = _dot(jnp.concatenate([ps[1][:, 2 * blk:], ps[3][:, 2 * blk:]], axis=1),
                   jnp.concatenate([v_ext(tail, nk, 0), v_ext(tail, nk, 1)], axis=0))
        for e in range(2):
            acc = acc + _dot(ps[2 * e], v_ext(0, tc, e)) + _dot(ps[2 * e + 1][:, 0:2 * blk], v_ext(tc, tail, e))
        for j in range(npair):
            r = slice(j * blk, (j + 1) * blk)
            sink_term = jnp.exp2(jnp.where(lane < hd, ms[0][r], ms[1][r]))
            o = acc[r, 0:2 * hd] / (acc[r, 2 * hd:4 * hd] + sink_term)
            o_ref[:, (h0 + 2 * j) * hd:(h0 + 2 * j + 2) * hd] = o.astype(o_ref.dtype)


def _swa_attention(qkv, sinks, b_sz, t_lat, t_ctx):
    n_tok = qkv.shape[0]
    blk = WINDOW
    hd = SWA_HEAD_DIM
    n_q = sinks.shape[0]
    n_kv = n_q // SWA_GROUP
    kvw = n_kv * hd
    kcol = (n_q * hd) // kvw
    vcol = kcol + 1
    nct, nb = t_ctx // blk, t_lat // blk
    ctx_blk0 = (b_sz * t_lat) // blk
    ctx_row0 = (b_sz * t_lat) // t_ctx

    def qrow(b, t):
        return jnp.where(t < nct, ctx_blk0 + b * nct + t, b * nb + t - nct)

    def krow(b, t, off):
        return b * nb + jnp.clip(t - nct + off, 0, nb - 1)

    kspec = lambda off: pl.BlockSpec((blk, kvw), lambda b, t: (krow(b, t, off), kcol))
    vspec = lambda off: pl.BlockSpec((blk, kvw), lambda b, t: (krow(b, t, off), vcol))
    return pl.pallas_call(
        functools.partial(_swa_attn_kernel, nct=nct, nb=nb, n_kv=n_kv),
        grid=(b_sz, nct + nb),
        in_specs=[
            pl.BlockSpec(memory_space=pltpu.SMEM),
            pl.BlockSpec((blk, n_q * hd), lambda b, t: (qrow(b, t), 0)),
            pl.BlockSpec((t_ctx, kvw), lambda b, t: (ctx_row0 + b, kcol)),
            pl.BlockSpec((t_ctx, kvw), lambda b, t: (ctx_row0 + b, vcol)),
            kspec(-1), kspec(0), kspec(1), vspec(-1), vspec(0), vspec(1),
        ],
        out_specs=pl.BlockSpec((blk, n_q * hd), lambda b, t: (qrow(b, t), 0)),
        out_shape=jax.ShapeDtypeStruct((n_tok, n_q * hd), BF16),
        compiler_params=_cparams(("parallel", "arbitrary")),
        name="swa_attention",
    )(sinks, qkv, qkv, qkv, qkv, qkv, qkv, qkv, qkv, qkv)


def kernel(x, c, ctx, c_ctx, w_ada, b_ada, g_norm, w_ffn_in, w_ffn_out, gla_w_in, gla_w_gate_down, gla_w_gate_up, gla_b_gate, gla_g_head, gla_w_out, mla_w_in, mla_g_q, mla_w_uq, mla_g_kv, mla_w_ukv, mla_w_out, swa_w_in, swa_sinks, swa_w_out):
    b_sz, t_lat, d = x.shape
    t_ctx = ctx.shape[1]
    depth = w_ada.shape[0]
    n_lat, n_ctx = b_sz * t_lat, b_sz * t_ctx
    n_tok = n_lat + n_ctx
    tm = ROW_TILE
    assert t_lat % tm == 0 and n_ctx % tm == 0 and b_sz < MOD_ROWS
    tpb = t_lat // tm
    n_lat_tiles = n_lat // tm

    hs = jnp.concatenate([x.reshape(n_lat, d), ctx.reshape(n_ctx, d)], axis=0)
    cond = jnp.zeros((MOD_ROWS, d), F32).at[:b_sz].set(c).at[b_sz].set(c_ctx)
    mod_all = _modulation(cond, w_ada, b_ada)
    cosf, sinf = _rope_tables(t_lat)
    common = dict(tiles_per_batch=tpb, nb=b_sz)

    for i in range(depth):
        kind, j = i % 3, i // 3
        last = i == depth - 1
        n_out = n_lat if last else n_tok
        mod = mod_all[i].reshape(MOD_ROWS, 1, 6 * d)

        if kind == 0:
            rank = GLA_GATE_RANK
            qk_w = gla_w_gate_up.shape[-1]
            pad = LANES - 2 * rank
            w_ext = jnp.concatenate([gla_w_in[j], gla_w_gate_down[j, 0], gla_w_gate_down[j, 1],
                                     jnp.zeros((d, pad), F32)], axis=1).astype(BF16)
            p = _nm_matmul(hs, 0, d, g_norm[i, 0], w_ext, 896, F32, "gla_in_proj", mod=mod, **common)
            wu_ext = jnp.zeros((2, LANES, qk_w), F32)
            for dr in range(2):
                wu_ext = wu_ext.at[dr, dr * rank:(dr + 1) * rank].set(gla_w_gate_up[j, dr])
            o_f, o_b = _gla_scan(p, wu_ext.astype(BF16), gla_b_gate[j].reshape(2, 1, qk_w), b_sz, t_lat, t_ctx)
            a = _gla_combine(o_f, o_b, p, gla_g_head[j], n_out)
            w_o = gla_w_out[j]
        elif kind == 1:
            n_heads = mla_w_out.shape[1] // MLA_V
            w_in = jnp.concatenate([mla_w_in[j], jnp.zeros((d, LANES - ROPE_DIM), F32)], axis=1).astype(BF16)
            p1 = _nm_matmul(hs, 0, d, g_norm[i, 0], w_in, w_in.shape[1], F32, "mla_in_proj", mod=mod, **common)
            w_uq = mla_w_uq[j].reshape(MLA_Q_RANK, n_heads, MLA_NOPE + ROPE_DIM)
            w_uq_n = w_uq[:, :, :MLA_NOPE].reshape(MLA_Q_RANK, n_heads * MLA_NOPE).astype(BF16)
            w_uq_r = w_uq[:, :, MLA_NOPE:].reshape(MLA_Q_RANK, n_heads * ROPE_DIM).astype(BF16)
            w_ukv = mla_w_ukv[j].reshape(MLA_KV_RANK, n_heads, MLA_NOPE + MLA_V)
            w_ukv = jnp.concatenate([w_ukv[:, :, :MLA_NOPE].reshape(MLA_KV_RANK, -1),
                                     w_ukv[:, :, MLA_NOPE:].reshape(MLA_KV_RANK, -1)], axis=1).astype(BF16)
            q_scale = (MLA_NOPE + ROPE_DIM) ** -0.5 * LOG2E
            qn = _nm_matmul(p1, 0, MLA_Q_RANK, mla_g_q[j], w_uq_n, 1024, BF16, "mla_q_nope", out_scale=q_scale)
            qr = _nm_matmul(p1, 0, MLA_Q_RANK, mla_g_q[j], w_uq_r, 1024, F32, "mla_q_rope")
            kv = _nm_matmul(p1, 1, MLA_KV_RANK, mla_g_kv[j], w_ukv, 1024, BF16, "mla_kv")
            qr = _rope(qr, 0, qr.shape[1], cosf, sinf, n_lat_tiles, tpb, "mla_rope_q",
                       n_scaled=qr.shape[1] // LANES, out_scale=q_scale)
            kr = _rope(p1, (MLA_Q_RANK + MLA_KV_RANK) // LANES, LANES, cosf, sinf, n_lat_tiles, tpb, "mla_rope_k")
            a = jnp.concatenate(_mla_attention(qn, qr, kv, kr, b_sz, t_lat, t_ctx), axis=0)
            w_o = mla_w_out[j]
        else:
            n_q = swa_sinks.shape[1]
            qkw = (n_q + n_q // SWA_GROUP) * SWA_HEAD_DIM
            p3 = _nm_matmul(hs, 0, d, g_norm[i, 0], swa_w_in[j].astype(BF16), 1280, F32, "swa_in_proj", mod=mod, **common)
            qkv = _rope(p3, 0, p3.shape[1], cosf, sinf, n_lat_tiles, tpb, "swa_rope", n_roped=qkw // LANES,
                        n_scaled=(n_q * SWA_HEAD_DIM) // LANES, out_scale=SWA_HEAD_DIM ** -0.5 * LOG2E)
            a = _swa_attention(qkv, swa_sinks[j], b_sz, t_lat, t_ctx)
            w_o = swa_w_out[j]

        hs = _proj_resid(a, w_o.astype(BF16), hs, g_norm[i, 1], mod, 2, n_out, name="mixer_out_proj", **common)
        hs = _ffn(hs, g_norm[i, 2], g_norm[i, 3], mod, w_ffn_in[i].astype(BF16), w_ffn_out[i].astype(BF16),
                  n_out, name="ffn", **common)

    return hs.reshape(b_sz, t_lat, d)
```

```python
import functools

import numpy as np
import jax
import jax.numpy as jnp
from jax import lax
from jax.experimental import pallas as pl
from jax.experimental.pallas import tpu as pltpu

F32 = jnp.float32
BF16 = jnp.bfloat16

EPS = 1e-6
ROPE_THETA = 10000.0
ROPE_DIM = 64
GRID_W = 64
WINDOW = 128
GLA_HEADS = 4
GLA_GATE_RANK = 16
GLA_TAU = 16.0
MLA_Q_RANK = 512
MLA_KV_RANK = 512
MLA_NOPE = 128
MLA_V = 128
SWA_HEAD_DIM = 64
SWA_GROUP = 8

LANES = 128
SUBLANES = 8
VMEM_LIMIT = 56 * 1024 * 1024
ROW_TILE = 512
MOD_ROWS = 16
NEG_BIG = -1e30
LOG2E = 1.4426950408889634


def _cparams(sem):
    return pltpu.CompilerParams(dimension_semantics=sem, vmem_limit_bytes=VMEM_LIMIT)


def _dot(a, b):
    return jnp.dot(a, b, preferred_element_type=F32)


def _dot_nt(a, b):
    return lax.dot_general(a, b, (((1,), (1,)), ((), ())), preferred_element_type=F32)


def _dot_tn(a, b):
    return lax.dot_general(a, b, (((0,), (0,)), ((), ())), preferred_element_type=F32)


def _sigmoid(x):
    return 1.0 / (1.0 + jnp.exp(-x))


def _rms(x):
    return x * lax.rsqrt(jnp.mean(x * x, axis=-1, keepdims=True) + EPS)


def _mod_kernel(c_ref, w_ref, b_ref, o_ref):
    c = c_ref[...]
    s = (c * _sigmoid(c)).astype(BF16)
    o_ref[0] = _dot(s, w_ref[0].astype(BF16)) + b_ref[0]


def _modulation(cond, w_ada, b_ada):
    depth, d, n = w_ada.shape
    tn = 1024
    return pl.pallas_call(
        _mod_kernel,
        grid=(depth, n // tn),
        in_specs=[
            pl.BlockSpec((MOD_ROWS, d), lambda l, j: (0, 0)),
            pl.BlockSpec((1, d, tn), lambda l, j: (l, 0, j)),
            pl.BlockSpec((1, 1, tn), lambda l, j: (l, 0, j)),
        ],
        out_specs=pl.BlockSpec((1, MOD_ROWS, tn), lambda l, j: (l, 0, j)),
        out_shape=jax.ShapeDtypeStruct((depth, MOD_ROWS, n), F32),
        compiler_params=_cparams(("parallel", "parallel")),
        name="modulation",
    )(cond, w_ada, b_ada.reshape(depth, 1, n))


def _nm_matmul_kernel(*refs, modulate, side, shift_idx, scale_idx, d, out_scale):
    x_ref, g_ref = refs[0:2]
    mod_ref = refs[2] if modulate else None
    w_ref = refs[2 + modulate]
    ws_ref = refs[3 + modulate] if side else None
    o_ref = refs[3 + modulate + side]
    os_ref = refs[4 + modulate + side] if side else None
    a_scr = refs[-1]

    @pl.when(pl.program_id(1) == 0)
    def _():
        y = _rms(x_ref[...].astype(F32)) * g_ref[...]
        if modulate:
            m = mod_ref[0]
            y = y * (1.0 + m[:, scale_idx * d:(scale_idx + 1) * d]) + m[:, shift_idx * d:(shift_idx + 1) * d]
        a_scr[...] = y.astype(BF16)
        if side:
            os_ref[...] = _dot(a_scr[...], ws_ref[...])

    o = _dot(a_scr[...], w_ref[...])
    if out_scale != 1.0:
        o = o * out_scale
    o_ref[...] = o.astype(o_ref.dtype)


def _nm_matmul(x, xcol, k, g, w, tn, out_dtype, name, mod=None, shift_idx=0, scale_idx=1, tiles_per_batch=1, nb=0,
               out_scale=1.0, w_side=None):
    n_rows = x.shape[0]
    n = w.shape[1]
    tm = ROW_TILE
    modulate = mod is not None
    side = w_side is not None
    in_specs = [pl.BlockSpec((tm, k), lambda i, j: (i, xcol)),
                pl.BlockSpec((1, k), lambda i, j: (0, 0))]
    args = [x, g.reshape(1, k)]
    if modulate:
        in_specs.append(pl.BlockSpec((1, 1, mod.shape[-1]),
                                     lambda i, j: (jnp.minimum(i // tiles_per_batch, nb), 0, 0)))
        args.append(mod)
    in_specs.append(pl.BlockSpec((k, tn), lambda i, j: (0, j)))
    args.append(w)
    out_specs = [pl.BlockSpec((tm, tn), lambda i, j: (i, j))]
    out_shape = [jax.ShapeDtypeStruct((n_rows, n), out_dtype)]
    if side:
        n_side = w_side.shape[1]
        in_specs.append(pl.BlockSpec((k, n_side), lambda i, j: (0, 0)))
        args.append(w_side)
        out_specs.append(pl.BlockSpec((tm, n_side), lambda i, j: (i, 0)))
        out_shape.append(jax.ShapeDtypeStruct((n_rows, n_side), F32))
    outs = pl.pallas_call(
        functools.partial(_nm_matmul_kernel, modulate=modulate, side=side, shift_idx=shift_idx, scale_idx=scale_idx,
                          d=k, out_scale=out_scale),
        grid=(n_rows // tm, n // tn),
        in_specs=in_specs,
        out_specs=out_specs,
        out_shape=out_shape,
        scratch_shapes=[pltpu.VMEM((tm, k), BF16)],
        compiler_params=_cparams(("parallel", "arbitrary")),
        name=name,
    )(*args)
    return outs if side else outs[0]


def _proj_resid_kernel(a_ref, w_ref, h_ref, g_ref, mod_ref, o_ref, *, gate_idx, d):
    y = _dot(a_ref[...].astype(BF16), w_ref[...])
    gate = mod_ref[0][:, gate_idx * d:(gate_idx + 1) * d]
    o_ref[...] = h_ref[...] + gate * (_rms(y) * g_ref[...])


def _proj_resid(a, w, hs, g, mod, gate_idx, n_rows, tiles_per_batch, nb, name):
    k, d = w.shape
    tm = ROW_TILE
    return pl.pallas_call(
        functools.partial(_proj_resid_kernel, gate_idx=gate_idx, d=d),
        grid=(n_rows // tm,),
        in_specs=[
            pl.BlockSpec((tm, k), lambda i: (i, 0)),
            pl.BlockSpec((k, d), lambda i: (0, 0)),
            pl.BlockSpec((tm, d), lambda i: (i, 0)),
            pl.BlockSpec((1, d), lambda i: (0, 0)),
            pl.BlockSpec((1, 1, mod.shape[-1]), lambda i: (jnp.minimum(i // tiles_per_batch, nb), 0, 0)),
        ],
        out_specs=pl.BlockSpec((tm, d), lambda i: (i, 0)),
        out_shape=jax.ShapeDtypeStruct((n_rows, d), F32),
        compiler_params=_cparams(("parallel",)),
        name=name,
    )(a, w, hs, g.reshape(1, d), mod)


def _ffn_kernel(h_ref, g2_ref, g3_ref, mod_ref, wg_ref, wu_ref, wo_ref, o_ref, a_scr, acc_scr, *, d):
    j = pl.program_id(1)

    @pl.when(j == 0)
    def _():
        m = mod_ref[0]
        y = _rms(h_ref[...]) * g2_ref[...]
        y = y * (1.0 + m[:, 4 * d:5 * d]) + m[:, 3 * d:4 * d]
        a_scr[...] = y.astype(BF16)
        acc_scr[...] = jnp.zeros_like(acc_scr)

    a = a_scr[...]
    gt = _dot(a, wg_ref[...])
    up = _dot(a, wu_ref[...])
    act = (gt * _sigmoid(gt) * up).astype(BF16)
    acc_scr[...] += _dot(act, wo_ref[...])

    @pl.when(j == pl.num_programs(1) - 1)
    def _():
        m = mod_ref[0]
        o_ref[...] = h_ref[...] + m[:, 5 * d:6 * d] * (_rms(acc_scr[...]) * g3_ref[...])


def _ffn(hs, g2, g3, mod, w_in, w_out, n_rows, tiles_per_batch, nb, name):
    d = hs.shape[1]
    f = w_out.shape[0]
    tm = ROW_TILE
    tf = 512
    nf = f // tf
    return pl.pallas_call(
        functools.partial(_ffn_kernel, d=d),
        grid=(n_rows // tm, nf),
        in_specs=[
            pl.BlockSpec((tm, d), lambda i, j: (i, 0)),
            pl.BlockSpec((1, d), lambda i, j: (0, 0)),
            pl.BlockSpec((1, d), lambda i, j: (0, 0)),
            pl.BlockSpec((1, 1, mod.shape[-1]), lambda i, j: (jnp.minimum(i // tiles_per_batch, nb), 0, 0)),
            pl.BlockSpec((d, tf), lambda i, j: (0, j)),
            pl.BlockSpec((d, tf), lambda i, j: (0, nf + j)),
            pl.BlockSpec((tf, d), lambda i, j: (j, 0)),
        ],
        out_specs=pl.BlockSpec((tm, d), lambda i, j: (i, 0)),
        out_shape=jax.ShapeDtypeStruct((n_rows, d), F32),
        scratch_shapes=[pltpu.VMEM((tm, d), BF16), pltpu.VMEM((tm, d), F32)],
        compiler_params=_cparams(("parallel", "arbitrary")),
        name=name,
    )(hs, g2.reshape(1, d), g3.reshape(1, d), mod, w_in, w_in, w_out)


GLA_CHUNK = 128


def _gla_structure(cs, reverse):
    idx = np.arange(cs)
    ip = cs - 1 - idx if reverse else idx
    ii, jj = ip[:, None], ip[None, :]
    levels = []
    s = cs // 2
    while s >= 1:
        levels.append(((ii // (2 * s)) == (jj // (2 * s))) & ((ii & s) != 0) & ((jj & s) == 0))
        s //= 2
    levels.append(ii == jj)
    return np.stack(levels).astype(np.float32), (jj <= ii).astype(np.float32)


def _gla_chunk(q, k, v_b, lg, msk_ref, tri, st_ref, b_ref, *, cs, dk, reverse):
    hi = lg.astype(BF16)
    r1 = lg - hi.astype(F32)
    mid = r1.astype(BF16)
    lo = (r1 - mid.astype(F32)).astype(BF16)
    b = _dot(tri, hi) + _dot(tri, mid) + _dot(tri, lo)
    b_ref[...] = b
    yield
    last = 0 if reverse else cs - 1
    b_last = b_ref[pl.ds(last, 1), :]

    st = st_ref[...]
    q_b = q.astype(BF16)
    k_b = k.astype(BF16)
    o = _dot_nt(q_b * jnp.exp2(b).astype(BF16), st.astype(BF16))
    yield

    row = lax.broadcasted_iota(jnp.int32, (cs, 1), 0)
    ip = (cs - 1 - row) if reverse else row

    def orig(p):
        return cs - 1 - p if reverse else p

    attn = jnp.zeros((cs, cs), F32)
    lvl = 0
    s = cs // 2
    while s >= 1:
        if s == 1:
            ql = q_b * jnp.exp2(lg).astype(BF16)
            kl = k_b
        else:
            if 2 * s >= SUBLANES:
                nblk = cs // (2 * s)
                pieces = []
                for jb in range(nblk):
                    m = nblk - 1 - jb if reverse else jb
                    p = orig(m * 2 * s + s - 1)
                    pieces.append(jnp.broadcast_to(b_ref[pl.ds(p, 1), :], (2 * s, dk)))
                ref = pieces[0] if nblk == 1 else jnp.concatenate(pieces, axis=0)
            else:
                ngrp = cs // SUBLANES
                pa, pb = [], []
                for jg in range(ngrp):
                    gp = ngrp - 1 - jg if reverse else jg
                    pa.append(jnp.broadcast_to(b_ref[pl.ds(orig(gp * SUBLANES + 1), 1), :], (SUBLANES, dk)))
                    pb.append(jnp.broadcast_to(b_ref[pl.ds(orig(gp * SUBLANES + 5), 1), :], (SUBLANES, dk)))
                ref = jnp.where((ip & 4) == 0, jnp.concatenate(pa, axis=0), jnp.concatenate(pb, axis=0))
            fac = jnp.exp2(-jnp.abs(b - ref)).astype(BF16)
            ql = q_b * fac
            kl = k_b * fac
        attn = attn + msk_ref[lvl] * _dot_nt(ql, kl)
        yield
        lvl += 1
        s //= 2
    attn = attn + msk_ref[lvl] * _dot_nt(q_b, k_b)

    k_dec = k_b * jnp.exp2(b_last - b).astype(BF16)
    st_ref[...] = st * jnp.exp2(b_last) + _dot_tn(v_b, k_dec)
    yield
    return o + _dot(attn.astype(BF16), v_b)


def _interleave(gens):
    results = [None] * len(gens)
    active = list(range(len(gens)))
    while active:
        for idx in list(active):
            try:
                next(gens[idx])
            except StopIteration as stop:
                results[idx] = stop.value
                active.remove(idx)
    return results


def _gla_scan_kernel(qf_ref, kf_ref, vf_ref, gf_ref, qb_ref, kb_ref, vb_ref, gb_ref, wu_ref, bg_ref,
                     msk_ref, tri_ref, of_ref, ob_ref, st_scr, b_scr, *, cs, dk, dv):
    @pl.when(pl.program_id(1) == 0)
    def _():
        st_scr[...] = jnp.zeros_like(st_scr)

    dirs = ((qf_ref, kf_ref, vf_ref, gf_ref, of_ref), (qb_ref, kb_ref, vb_ref, gb_ref, ob_ref))
    chains, dests = [], []
    for dr, (q_ref, k_ref, v_ref, gd_ref, o_ref) in enumerate(dirs):
        z = _dot(gd_ref[...].astype(BF16), wu_ref[dr]) + bg_ref[dr]
        lg_all = (jnp.minimum(z, 0.0) - jnp.log1p(jnp.exp(-jnp.abs(z)))) * (LOG2E / GLA_TAU)
        for h in range(GLA_HEADS):
            ks = slice(h * dk, (h + 1) * dk)
            vs = slice(h * dv, (h + 1) * dv)
            chains.append(_gla_chunk(q_ref[:, ks] * (dk ** -0.5), k_ref[:, ks], v_ref[:, vs].astype(BF16),
                                     lg_all[:, ks], msk_ref.at[dr], tri_ref[dr], st_scr.at[dr, h], b_scr.at[dr, h],
                                     cs=cs, dk=dk, reverse=bool(dr)))
            dests.append((o_ref, vs))
    for (o_ref, vs), o in zip(dests, _interleave(chains)):
        o_ref[:, vs] = o


def _gla_scan(p, gd, wu_ext, bg, b_sz, t_lat, t_ctx):
    cs = GLA_CHUNK
    qkw = wu_ext.shape[-1]
    dk = qkw // GLA_HEADS
    vw = 2 * qkw
    dv = vw // GLA_HEADS
    n_tok = p.shape[0]
    ncc, ncl = t_ctx // cs, t_lat // cs
    ctx_blk0 = (b_sz * t_lat) // cs
    structs = [_gla_structure(cs, rev) for rev in (False, True)]
    masks = np.stack([m for m, _ in structs])
    tri = np.stack([t for _, t in structs])

    def rowblk(rev):
        def f(b, s):
            if rev:
                cc, lc = ncc - 1 - s, ncl - 1 - (s - ncc)
            else:
                cc, lc = s, s - ncc
            return jnp.where(s < ncc, ctx_blk0 + b * ncc + cc, b * ncl + lc)
        return f

    def dir_specs(rev):
        rb = rowblk(rev)
        return [pl.BlockSpec((cs, qkw), lambda b, s: (rb(b, s), 0)),
                pl.BlockSpec((cs, qkw), lambda b, s: (rb(b, s), 1)),
                pl.BlockSpec((cs, vw), lambda b, s: (rb(b, s), 1)),
                pl.BlockSpec((cs, LANES), lambda b, s: (rb(b, s), 0))]

    return pl.pallas_call(
        functools.partial(_gla_scan_kernel, cs=cs, dk=dk, dv=dv),
        grid=(b_sz, ncc + ncl),
        in_specs=dir_specs(False) + dir_specs(True) + [
            pl.BlockSpec((2, LANES, qkw), lambda b, s: (0, 0, 0)),
            pl.BlockSpec((2, 1, qkw), lambda b, s: (0, 0, 0)),
            pl.BlockSpec(masks.shape, lambda b, s: (0, 0, 0, 0)),
            pl.BlockSpec(tri.shape, lambda b, s: (0, 0, 0)),
        ],
        out_specs=[pl.BlockSpec((cs, vw), lambda b, s: (rowblk(False)(b, s), 0)),
                   pl.BlockSpec((cs, vw), lambda b, s: (rowblk(True)(b, s), 0))],
        out_shape=[jax.ShapeDtypeStruct((n_tok, vw), F32)] * 2,
        scratch_shapes=[pltpu.VMEM((2, GLA_HEADS, dv, dk), F32), pltpu.VMEM((2, GLA_HEADS, cs, dk), F32)],
        compiler_params=_cparams(("parallel", "arbitrary")),
        name="gla_scan",
    )(p, p, p, gd, p, p, p, gd, wu_ext, bg, jnp.asarray(masks), jnp.asarray(tri, dtype=BF16))


def _gla_combine_kernel(of_ref, ob_ref, r_ref, gh_ref, o_ref, *, dv):
    for h in range(GLA_HEADS):
        sl = slice(h * dv, (h + 1) * dv)
        y = _rms(of_ref[:, sl] + ob_ref[:, sl]) * gh_ref[...]
        r = r_ref[:, sl]
        o_ref[:, sl] = (y * (r * _sigmoid(r))).astype(o_ref.dtype)


def _gla_combine(o_f, o_b, p, g_head, n_rows):
    d = o_f.shape[1]
    dv = d // GLA_HEADS
    tm = ROW_TILE
    rcol = (p.shape[1] // d) - 1
    return pl.pallas_call(
        functools.partial(_gla_combine_kernel, dv=dv),
        grid=(n_rows // tm,),
        in_specs=[
            pl.BlockSpec((tm, d), lambda i: (i, 0)),
            pl.BlockSpec((tm, d), lambda i: (i, 0)),
            pl.BlockSpec((tm, d), lambda i: (i, rcol)),
            pl.BlockSpec((1, dv), lambda i: (0, 0)),
        ],
        out_specs=pl.BlockSpec((tm, d), lambda i: (i, 0)),
        out_shape=jax.ShapeDtypeStruct((n_rows, d), BF16),
        compiler_params=_cparams(("parallel",)),
        name="gla_combine",
    )(o_f, o_b, p, g_head.reshape(1, dv))


def _rope_kernel(x_ref, cos_ref, sin_ref, o_ref, *, n_lat_tiles, n_roped, n_scaled, out_scale):
    is_lat = pl.program_id(0) < n_lat_tiles
    cosf = cos_ref[...]
    sinf = sin_ref[...]
    lane = lax.broadcasted_iota(jnp.int32, cosf.shape, 1)
    first = (lane % ROPE_DIM) < (ROPE_DIM // 2)
    for c in range(x_ref.shape[1] // LANES):
        y = x_ref[:, c * LANES:(c + 1) * LANES].astype(F32)
        if c < n_roped:
            partner = jnp.where(first, pltpu.roll(y, LANES - ROPE_DIM // 2, 1), pltpu.roll(y, ROPE_DIM // 2, 1))
            y = jnp.where(is_lat, y * cosf + partner * sinf, y)
        if c < n_scaled:
            y = y * out_scale
        o_ref[:, c * LANES:(c + 1) * LANES] = y.astype(o_ref.dtype)


def _rope(x, col, width, cosf, sinf, n_lat_tiles, tiles_per_batch, name, n_roped=None, n_scaled=0, out_scale=1.0):
    n_rows = x.shape[0]
    tm = ROW_TILE
    n_roped = width // LANES if n_roped is None else n_roped
    return pl.pallas_call(
        functools.partial(_rope_kernel, n_lat_tiles=n_lat_tiles, n_roped=n_roped, n_scaled=n_scaled,
                          out_scale=out_scale),
        grid=(n_rows // tm,),
        in_specs=[
            pl.BlockSpec((tm, width), lambda i: (i, col)),
            pl.BlockSpec((tm, LANES), lambda i: (i % tiles_per_batch, 0)),
            pl.BlockSpec((tm, LANES), lambda i: (i % tiles_per_batch, 0)),
        ],
        out_specs=pl.BlockSpec((tm, width), lambda i: (i, 0)),
        out_shape=jax.ShapeDtypeStruct((n_rows, width), BF16),
        compiler_params=_cparams(("parallel",)),
        name=name,
    )(x, cosf, sinf)


def _rope_tables(t_lat):
    t = jnp.arange(t_lat)
    row = (t // GRID_W).astype(F32)
    col = (t % GRID_W).astype(F32)
    n_freq = ROPE_DIM // 4
    inv_freq = ROPE_THETA ** (-jnp.arange(n_freq, dtype=F32) / n_freq)
    ang = jnp.concatenate([row[:, None] * inv_freq, col[:, None] * inv_freq], axis=-1)
    cos, sin = jnp.cos(ang), jnp.sin(ang)
    reps = LANES // ROPE_DIM
    return jnp.tile(jnp.concatenate([cos, cos], axis=-1), (1, reps)), jnp.tile(jnp.concatenate([-sin, sin], axis=-1), (1, reps))


MLA_TQ = 512
MLA_TK = 512
MLA_HEADS_PER_STEP = 2


def _mla_attn_kernel(qn_ref, qr_ref, qnc_ref, qrc_ref, knc_ref, vc_ref, krc_ref, knl_ref, vl_ref, krl_ref,
                     o_ref, oc_ref, kcat, vext, *, tc, tl):
    t = pl.program_id(2)
    nh = MLA_HEADS_PER_STEP
    rd = ROPE_DIM
    vw = MLA_V + LANES

    def qcat(n_ref, r_ref, h):
        return jnp.concatenate([n_ref[:, h * MLA_NOPE:(h + 1) * MLA_NOPE], r_ref[:, h * rd:(h + 1) * rd]], axis=1)

    def finish(acc):
        return acc[:, 0:MLA_V] / acc[:, MLA_V:vw]

    @pl.when(t == 0)
    def _():
        for h in range(nh):
            ns = slice(h * MLA_NOPE, (h + 1) * MLA_NOPE)
            kcat[h, 0:tc, 0:MLA_NOPE] = knc_ref[:, ns]
            kcat[h, 0:tc, MLA_NOPE:MLA_NOPE + rd] = krc_ref[:, 0:rd]
            kcat[h, tc:tc + tl, 0:MLA_NOPE] = knl_ref[:, ns]
            kcat[h, tc:tc + tl, MLA_NOPE:MLA_NOPE + rd] = krl_ref[:, 0:rd]
            vext[0:tc, h * vw:h * vw + MLA_V] = vc_ref[:, h * MLA_V:(h + 1) * MLA_V]
            vext[tc:tc + tl, h * vw:h * vw + MLA_V] = vl_ref[:, h * MLA_V:(h + 1) * MLA_V]
            vext[:, h * vw + MLA_V:(h + 1) * vw] = jnp.ones((tc + tl, LANES), BF16)
        for h in range(nh):
            s = _dot_nt(qcat(qnc_ref, qrc_ref, h), kcat[h, 0:tc, :])
            p = jnp.exp2(s - jnp.max(s, axis=-1, keepdims=True))
            acc = _dot(p.astype(BF16), vext[0:tc, h * vw:(h + 1) * vw])
            oc_ref[:, h * MLA_V:(h + 1) * MLA_V] = finish(acc).astype(oc_ref.dtype)

    for h in range(nh):
        qc = qcat(qn_ref, qr_ref, h)
        s = _dot_nt(qc, kcat[h, 0:tc, :])
        m = jnp.max(s, axis=-1, keepdims=True)
        acc = _dot(jnp.exp2(s - m).astype(BF16), vext[0:tc, h * vw:(h + 1) * vw])
        for c0 in range(tc, tc + tl, MLA_TK):
            s = _dot_nt(qc, kcat[h, c0:c0 + MLA_TK, :])
            m_new = jnp.maximum(m, jnp.max(s, axis=-1, keepdims=True))
            acc = jnp.exp2(m - m_new) * acc + _dot(jnp.exp2(s - m_new).astype(BF16),
                                                   vext[c0:c0 + MLA_TK, h * vw:(h + 1) * vw])
            m = m_new
        o_ref[:, h * MLA_V:(h + 1) * MLA_V] = finish(acc).astype(o_ref.dtype)


def _mla_attention(qn, qr, kv, kr, b_sz, t_lat, t_ctx):
    hd = qn.shape[1]
    nh = MLA_HEADS_PER_STEP
    n_heads = hd // MLA_NOPE
    tq = MLA_TQ
    assert t_lat % tq == 0 and t_lat % MLA_TK == 0
    nlt = t_lat // tq
    ctx_blk0 = (b_sz * t_lat) // t_ctx
    vcol0 = n_heads // nh
    wn, wr = nh * MLA_NOPE, nh * ROPE_DIM
    return pl.pallas_call(
        functools.partial(_mla_attn_kernel, tc=t_ctx, tl=t_lat),
        grid=(b_sz, n_heads // nh, nlt),
        in_specs=[
            pl.BlockSpec((tq, wn), lambda b, h, t: (b * nlt + t, h)),
            pl.BlockSpec((tq, wr), lambda b, h, t: (b * nlt + t, h)),
            pl.BlockSpec((t_ctx, wn), lambda b, h, t: (ctx_blk0 + b, h)),
            pl.BlockSpec((t_ctx, wr), lambda b, h, t: (ctx_blk0 + b, h)),
            pl.BlockSpec((t_ctx, wn), lambda b, h, t: (ctx_blk0 + b, h)),
            pl.BlockSpec((t_ctx, wn), lambda b, h, t: (ctx_blk0 + b, vcol0 + h)),
            pl.BlockSpec((t_ctx, LANES), lambda b, h, t: (ctx_blk0 + b, 0)),
            pl.BlockSpec((t_lat, wn), lambda b, h, t: (b, h)),
            pl.BlockSpec((t_lat, wn), lambda b, h, t: (b, vcol0 + h)),
            pl.BlockSpec((t_lat, LANES), lambda b, h, t: (b, 0)),
        ],
        out_specs=[pl.BlockSpec((tq, nh * MLA_V), lambda b, h, t: (b * nlt + t, h)),
                   pl.BlockSpec((t_ctx, nh * MLA_V), lambda b, h, t: (b, h))],
        out_shape=[jax.ShapeDtypeStruct((b_sz * t_lat, n_heads * MLA_V), BF16),
                   jax.ShapeDtypeStruct((b_sz * t_ctx, n_heads * MLA_V), BF16)],
        scratch_shapes=[pltpu.VMEM((nh, t_ctx + t_lat, MLA_NOPE + ROPE_DIM), BF16),
                        pltpu.VMEM((t_ctx + t_lat, nh * (MLA_V + LANES)), BF16)],
        compiler_params=_cparams(("parallel", "parallel", "arbitrary")),
        name="mla_attention",
    )(qn, qr, qn, qr, kv, kv, kr, kv, kv, kr)


def _swa_attn_kernel(sink_ref, q_ref, kc_ref, vc_ref, kp_ref, kc0_ref, kn_ref, vp_ref, vc0_ref, vn_ref, o_ref,
                     *, nct, nb, n_kv):
    t = pl.program_id(1)
    n = t - nct
    blk = WINDOW
    hd = SWA_HEAD_DIM
    tc = kc_ref.shape[0]
    nk = tc + 3 * blk
    npair = SWA_GROUP // 2

    qi = lax.broadcasted_iota(jnp.int32, (blk, 3 * blk), 0)
    kj = lax.broadcasted_iota(jnp.int32, (blk, 3 * blk), 1)
    lo = jnp.where(n > 0, 0, blk)
    hi = jnp.where(t < nct, 0, jnp.where(n < nb - 1, 3 * blk, 2 * blk))
    win_ok = (jnp.abs(qi - kj + blk) <= WINDOW) & (kj >= lo) & (kj < hi)
    bias = jnp.concatenate([jnp.zeros((blk, tc), F32), jnp.where(win_ok, 0.0, NEG_BIG)], axis=1)
    bias = jnp.concatenate([bias] * npair, axis=0)
    lane = lax.broadcasted_iota(jnp.int32, (blk, 2 * hd), 1)
    zeros = jnp.zeros((nk, hd), BF16)
    ones = jnp.ones((nk, hd), BF16)

    for kvh in range(n_kv):
        ks = slice(kvh * hd, (kvh + 1) * hd)
        k_all = jnp.concatenate([kc_ref[:, ks], kp_ref[:, ks], kc0_ref[:, ks], kn_ref[:, ks]], axis=0)
        v_all = jnp.concatenate([vc_ref[:, ks], vp_ref[:, ks], vc0_ref[:, ks], vn_ref[:, ks]], axis=0)
        k_bd = jnp.concatenate([jnp.concatenate([k_all, zeros], axis=1),
                                jnp.concatenate([zeros, k_all], axis=1)], axis=0)
        h0 = kvh * SWA_GROUP
        q_st = jnp.concatenate([q_ref[:, (h0 + 2 * j) * hd:(h0 + 2 * j + 2) * hd] for j in range(npair)], axis=0)
        s = _dot_nt(q_st, k_bd)
        ps, ms = [], []
        for e in range(2):
            sink_col = jnp.concatenate([jnp.full((blk, 1), sink_ref[h0 + 2 * j + e] * LOG2E, F32)
                                        for j in range(npair)], axis=0)
            s_e = s[:, e * nk:(e + 1) * nk] + bias
            m = jnp.maximum(jnp.max(s_e, axis=-1, keepdims=True), sink_col)
            p = jnp.exp2(s_e - m).astype(BF16)
            ps += [p[:, 0:tc], p[:, tc:]]
            ms.append(sink_col - m)

        def v_ext(r0, r1, e):
            v, z, o = v_all[r0:r1], zeros[r0:r1], ones[r0:r1]
            return jnp.concatenate([z, v, z, o] if e else [v, z, o, z], axis=1)

        tail = tc + 2 * blk
        acc = _dot(jnp.concatenate([ps[1][:, 2 * blk:], ps[3][:, 2 * blk:]], axis=1),
                   jnp.concatenate([v_ext(tail, nk, 0), v_ext(tail, nk, 1)], axis=0))
        for e in range(2):
            acc = acc + _dot(ps[2 * e], v_ext(0, tc, e)) + _dot(ps[2 * e + 1][:, 0:2 * blk], v_ext(tc, tail, e))
        for j in range(npair):
            r = slice(j * blk, (j + 1) * blk)
            sink_term = jnp.exp2(jnp.where(lane < hd, ms[0][r], ms[1][r]))
            o = acc[r, 0:2 * hd] / (acc[r, 2 * hd:4 * hd] + sink_term)
            o_ref[:, (h0 + 2 * j) * hd:(h0 + 2 * j + 2) * hd] = o.astype(o_ref.dtype)


def _swa_attention(qkv, sinks, b_sz, t_lat, t_ctx):
    n_tok = qkv.shape[0]
    blk = WINDOW
    hd = SWA_HEAD_DIM
    n_q = sinks.shape[0]
    n_kv = n_q // SWA_GROUP
    kvw = n_kv * hd
    kcol = (n_q * hd) // kvw
    vcol = kcol + 1
    nct, nb = t_ctx // blk, t_lat // blk
    ctx_blk0 = (b_sz * t_lat) // blk
    ctx_row0 = (b_sz * t_lat) // t_ctx

    def qrow(b, t):
        return jnp.where(t < nct, ctx_blk0 + b * nct + t, b * nb + t - nct)

    def krow(b, t, off):
        return b * nb + jnp.clip(t - nct + off, 0, nb - 1)

    kspec = lambda off: pl.BlockSpec((blk, kvw), lambda b, t: (krow(b, t, off), kcol))
    vspec = lambda off: pl.BlockSpec((blk, kvw), lambda b, t: (krow(b, t, off), vcol))
    return pl.pallas_call(
        functools.partial(_swa_attn_kernel, nct=nct, nb=nb, n_kv=n_kv),
        grid=(b_sz, nct + nb),
        in_specs=[
            pl.BlockSpec(memory_space=pltpu.SMEM),
            pl.BlockSpec((blk, n_q * hd), lambda b, t: (qrow(b, t), 0)),
            pl.BlockSpec((t_ctx, kvw), lambda b, t: (ctx_row0 + b, kcol)),
            pl.BlockSpec((t_ctx, kvw), lambda b, t: (ctx_row0 + b, vcol)),
            kspec(-1), kspec(0), kspec(1), vspec(-1), vspec(0), vspec(1),
        ],
        out_specs=pl.BlockSpec((blk, n_q * hd), lambda b, t: (qrow(b, t), 0)),
        out_shape=jax.ShapeDtypeStruct((n_tok, n_q * hd), BF16),
        compiler_params=_cparams(("parallel", "arbitrary")),
        name="swa_attention",
    )(sinks, qkv, qkv, qkv, qkv, qkv, qkv, qkv, qkv, qkv)


def kernel(x, c, ctx, c_ctx, w_ada, b_ada, g_norm, w_ffn_in, w_ffn_out, gla_w_in, gla_w_gate_down, gla_w_gate_up, gla_b_gate, gla_g_head, gla_w_out, mla_w_in, mla_g_q, mla_w_uq, mla_g_kv, mla_w_ukv, mla_w_out, swa_w_in, swa_sinks, swa_w_out):
    b_sz, t_lat, d = x.shape
    t_ctx = ctx.shape[1]
    depth = w_ada.shape[0]
    n_lat, n_ctx = b_sz * t_lat, b_sz * t_ctx
    n_tok = n_lat + n_ctx
    tm = ROW_TILE
    assert t_lat % tm == 0 and n_ctx % tm == 0 and b_sz < MOD_ROWS
    tpb = t_lat // tm
    n_lat_tiles = n_lat // tm

    hs = jnp.concatenate([x.reshape(n_lat, d), ctx.reshape(n_ctx, d)], axis=0)
    cond = jnp.zeros((MOD_ROWS, d), F32).at[:b_sz].set(c).at[b_sz].set(c_ctx)
    mod_all = _modulation(cond, w_ada, b_ada)
    cosf, sinf = _rope_tables(t_lat)
    common = dict(tiles_per_batch=tpb, nb=b_sz)

    for i in range(depth):
        kind, j = i % 3, i // 3
        last = i == depth - 1
        n_out = n_lat if last else n_tok
        mod = mod_all[i].reshape(MOD_ROWS, 1, 6 * d)

        if kind == 0:
            rank = GLA_GATE_RANK
            qk_w = gla_w_gate_up.shape[-1]
            pad = LANES - 2 * rank
            w_gd = jnp.concatenate([gla_w_gate_down[j, 0], gla_w_gate_down[j, 1], jnp.zeros((d, pad), F32)],
                                   axis=1).astype(BF16)
            p, gd = _nm_matmul(hs, 0, d, g_norm[i, 0], gla_w_in[j].astype(BF16), 1024, F32, "gla_in_proj", mod=mod,
                               w_side=w_gd, **common)
            wu_ext = jnp.zeros((2, LANES, qk_w), F32)
            for dr in range(2):
                wu_ext = wu_ext.at[dr, dr * rank:(dr + 1) * rank].set(gla_w_gate_up[j, dr])
            o_f, o_b = _gla_scan(p, gd, wu_ext.astype(BF16), gla_b_gate[j].reshape(2, 1, qk_w), b_sz, t_lat, t_ctx)
            a = _gla_combine(o_f, o_b, p, gla_g_head[j], n_out)
            w_o = gla_w_out[j]
        elif kind == 1:
            n_heads = mla_w_out.shape[1] // MLA_V
            w_in = jnp.concatenate([mla_w_in[j], jnp.zeros((d, LANES - ROPE_DIM), F32)], axis=1).astype(BF16)
            p1 = _nm_matmul(hs, 0, d, g_norm[i, 0], w_in, w_in.shape[1], F32, "mla_in_proj", mod=mod, **common)
            w_uq = mla_w_uq[j].reshape(MLA_Q_RANK, n_heads, MLA_NOPE + ROPE_DIM)
            w_uq_n = w_uq[:, :, :MLA_NOPE].reshape(MLA_Q_RANK, n_heads * MLA_NOPE).astype(BF16)
            w_uq_r = w_uq[:, :, MLA_NOPE:].reshape(MLA_Q_RANK, n_heads * ROPE_DIM).astype(BF16)
            w_ukv = mla_w_ukv[j].reshape(MLA_KV_RANK, n_heads, MLA_NOPE + MLA_V)
            w_ukv = jnp.concatenate([w_ukv[:, :, :MLA_NOPE].reshape(MLA_KV_RANK, -1),
                                     w_ukv[:, :, MLA_NOPE:].reshape(MLA_KV_RANK, -1)], axis=1).astype(BF16)
            q_scale = (MLA_NOPE + ROPE_DIM) ** -0.5 * LOG2E
            qn = _nm_matmul(p1, 0, MLA_Q_RANK, mla_g_q[j], w_uq_n, 1024, BF16, "mla_q_nope", out_scale=q_scale)
            qr = _nm_matmul(p1, 0, MLA_Q_RANK, mla_g_q[j], w_uq_r, 1024, F32, "mla_q_rope")
            kv = _nm_matmul(p1, 1, MLA_KV_RANK, mla_g_kv[j], w_ukv, 1024, BF16, "mla_kv")
            qr = _rope(qr, 0, qr.shape[1], cosf, sinf, n_lat_tiles, tpb, "mla_rope_q",
                       n_scaled=qr.shape[1] // LANES, out_scale=q_scale)
            kr = _rope(p1, (MLA_Q_RANK + MLA_KV_RANK) // LANES, LANES, cosf, sinf, n_lat_tiles, tpb, "mla_rope_k")
            a = jnp.concatenate(_mla_attention(qn, qr, kv, kr, b_sz, t_lat, t_ctx), axis=0)
            w_o = mla_w_out[j]
        else:
            n_q = swa_sinks.shape[1]
            qkw = (n_q + n_q // SWA_GROUP) * SWA_HEAD_DIM
            p3 = _nm_matmul(hs, 0, d, g_norm[i, 0], swa_w_in[j].astype(BF16), 1280, F32, "swa_in_proj", mod=mod, **common)
            qkv = _rope(p3, 0, p3.shape[1], cosf, sinf, n_lat_tiles, tpb, "swa_rope", n_roped=qkw // LANES,
                        n_scaled=(n_q * SWA_HEAD_DIM) // LANES, out_scale=SWA_HEAD_DIM ** -0.5 * LOG2E)
            a = _swa_attention(qkv, swa_sinks[j], b_sz, t_lat, t_ctx)
            w_o = swa_w_out[j]

        hs = _proj_resid(a, w_o.astype(BF16), hs, g_norm[i, 1], mod, 2, n_out, name="mixer_out_proj", **common)
        hs = _ffn(hs, g_norm[i, 2], g_norm[i, 3], mod, w_ffn_in[i].astype(BF16), w_ffn_out[i].astype(BF16),
                  n_out, name="ffn", **common)

    return hs.reshape(b_sz, t_lat, d)
```

```python
import functools

import numpy as np
import jax
import jax.numpy as jnp
from jax import lax
from jax.experimental import pallas as pl
from jax.experimental.pallas import tpu as pltpu

F32 = jnp.float32
BF16 = jnp.bfloat16

EPS = 1e-6
ROPE_THETA = 10000.0
ROPE_DIM = 64
GRID_W = 64
WINDOW = 128
GLA_HEADS = 4
GLA_GATE_RANK = 16
GLA_TAU = 16.0
MLA_Q_RANK = 512
MLA_KV_RANK = 512
MLA_NOPE = 128
MLA_V = 128
SWA_HEAD_DIM = 64
SWA_GROUP = 8

LANES = 128
SUBLANES = 8
VMEM_LIMIT = 56 * 1024 * 1024
ROW_TILE = 512
MOD_ROWS = 16
NEG_BIG = -1e30
LOG2E = 1.4426950408889634


def _cparams(sem):
    return pltpu.CompilerParams(dimension_semantics=sem, vmem_limit_bytes=VMEM_LIMIT)


def _dot(a, b):
    return jnp.dot(a, b, preferred_element_type=F32)


def _dot_nt(a, b):
    return lax.dot_general(a, b, (((1,), (1,)), ((), ())), preferred_element_type=F32)


def _dot_tn(a, b):
    return lax.dot_general(a, b, (((0,), (0,)), ((), ())), preferred_element_type=F32)


def _sigmoid(x):
    return 1.0 / (1.0 + jnp.exp(-x))


def _rms(x):
    return x * lax.rsqrt(jnp.mean(x * x, axis=-1, keepdims=True) + EPS)


def _mod_kernel(c_ref, w_ref, b_ref, o_ref):
    c = c_ref[...]
    s = (c * _sigmoid(c)).astype(BF16)
    o_ref[0] = _dot(s, w_ref[0].astype(BF16)) + b_ref[0]


def _modulation(cond, w_ada, b_ada):
    depth, d, n = w_ada.shape
    tn = 1024
    return pl.pallas_call(
        _mod_kernel,
        grid=(depth, n // tn),
        in_specs=[
            pl.BlockSpec((MOD_ROWS, d), lambda l, j: (0, 0)),
            pl.BlockSpec((1, d, tn), lambda l, j: (l, 0, j)),
            pl.BlockSpec((1, 1, tn), lambda l, j: (l, 0, j)),
        ],
        out_specs=pl.BlockSpec((1, MOD_ROWS, tn), lambda l, j: (l, 0, j)),
        out_shape=jax.ShapeDtypeStruct((depth, MOD_ROWS, n), F32),
        compiler_params=_cparams(("parallel", "parallel")),
        name="modulation",
    )(cond, w_ada, b_ada.reshape(depth, 1, n))


def _adaln(x, g_ref, mod_ref, shift_idx, scale_idx, d):
    m = mod_ref[0]
    return (_rms(x) * g_ref[...]) * (1.0 + m[:, scale_idx * d:(scale_idx + 1) * d]) + m[:, shift_idx * d:(shift_idx + 1) * d]


def _rope_chunk(y, cosf, sinf, first, is_lat):
    partner = jnp.where(first, pltpu.roll(y, LANES - ROPE_DIM // 2, 1), pltpu.roll(y, ROPE_DIM // 2, 1))
    return jnp.where(is_lat, y * cosf + partner * sinf, y)


def _first_half_lanes(shape):
    lane = lax.broadcasted_iota(jnp.int32, shape, 1)
    return (lane % ROPE_DIM) < (ROPE_DIM // 2)


def _mod_spec(mod, tiles_per_batch, nb):
    return pl.BlockSpec((1, 1, mod.shape[-1]), lambda i, *_: (jnp.minimum(i // tiles_per_batch, nb), 0, 0))


def _gla_in_proj_kernel(x_ref, g_ref, mod_ref, w_ref, wgd_ref, qkv_ref, r_ref, gd_ref, a_scr, *, d, n_qkv_tiles):
    j = pl.program_id(1)

    @pl.when(j == 0)
    def _():
        a_scr[...] = _adaln(x_ref[...], g_ref, mod_ref, 0, 1, d).astype(BF16)
        gd_ref[...] = _dot(a_scr[...], wgd_ref[...])

    o = _dot(a_scr[...], w_ref[...])

    @pl.when(j < n_qkv_tiles)
    def _():
        qkv_ref[...] = o.astype(qkv_ref.dtype)

    @pl.when(j >= n_qkv_tiles)
    def _():
        r_ref[...] = o


def _gla_in_proj(hs, g, mod, w, w_gd, n_qkv, tm, tiles_per_batch, nb):
    n_rows, d = hs.shape
    n = w.shape[1]
    tn = 1024
    nq = n_qkv // tn
    n_side = w_gd.shape[1]
    return pl.pallas_call(
        functools.partial(_gla_in_proj_kernel, d=d, n_qkv_tiles=nq),
        grid=(n_rows // tm, n // tn),
        in_specs=[
            pl.BlockSpec((tm, d), lambda i, j: (i, 0)),
            pl.BlockSpec((1, d), lambda i, j: (0, 0)),
            _mod_spec(mod, tiles_per_batch, nb),
            pl.BlockSpec((d, tn), lambda i, j: (0, j)),
            pl.BlockSpec((d, n_side), lambda i, j: (0, 0)),
        ],
        out_specs=[pl.BlockSpec((tm, tn), lambda i, j: (i, jnp.minimum(j, nq - 1))),
                   pl.BlockSpec((tm, tn), lambda i, j: (i, jnp.maximum(j - nq, 0))),
                   pl.BlockSpec((tm, n_side), lambda i, j: (i, 0))],
        out_shape=[jax.ShapeDtypeStruct((n_rows, n_qkv), BF16),
                   jax.ShapeDtypeStruct((n_rows, n - n_qkv), F32),
                   jax.ShapeDtypeStruct((n_rows, n_side), F32)],
        scratch_shapes=[pltpu.VMEM((tm, d), BF16)],
        compiler_params=_cparams(("parallel", "arbitrary")),
        name="gla_in_proj",
    )(hs, g.reshape(1, d), mod, w, w_gd)


def _swa_in_proj_kernel(x_ref, g_ref, mod_ref, cos_ref, sin_ref, w_ref, o_ref, *, d, n_lat_tiles, n_roped, n_scaled,
                        out_scale):
    a = _adaln(x_ref[...], g_ref, mod_ref, 0, 1, d).astype(BF16)
    y_all = _dot(a, w_ref[...])
    is_lat = pl.program_id(0) < n_lat_tiles
    cosf, sinf = cos_ref[...], sin_ref[...]
    first = _first_half_lanes(cosf.shape)
    for c in range(y_all.shape[1] // LANES):
        y = y_all[:, c * LANES:(c + 1) * LANES]
        if c < n_roped:
            y = _rope_chunk(y, cosf, sinf, first, is_lat)
        if c < n_scaled:
            y = y * out_scale
        o_ref[:, c * LANES:(c + 1) * LANES] = y.astype(o_ref.dtype)


def _swa_in_proj(hs, g, mod, w, cosf, sinf, n_roped, n_scaled, out_scale, n_lat_tiles, tiles_per_batch, nb):
    n_rows, d = hs.shape
    n = w.shape[1]
    tm = ROW_TILE
    return pl.pallas_call(
        functools.partial(_swa_in_proj_kernel, d=d, n_lat_tiles=n_lat_tiles, n_roped=n_roped, n_scaled=n_scaled,
                          out_scale=out_scale),
        grid=(n_rows // tm,),
        in_specs=[
            pl.BlockSpec((tm, d), lambda i: (i, 0)),
            pl.BlockSpec((1, d), lambda i: (0, 0)),
            _mod_spec(mod, tiles_per_batch, nb),
            pl.BlockSpec((tm, LANES), lambda i: (i % tiles_per_batch, 0)),
            pl.BlockSpec((tm, LANES), lambda i: (i % tiles_per_batch, 0)),
            pl.BlockSpec((d, n), lambda i: (0, 0)),
        ],
        out_specs=pl.BlockSpec((tm, n), lambda i: (i, 0)),
        out_shape=jax.ShapeDtypeStruct((n_rows, n), BF16),
        compiler_params=_cparams(("parallel",)),
        name="swa_in_proj",
    )(hs, g.reshape(1, d), mod, cosf, sinf, w)


def _mla_in_proj_kernel(x_ref, g_ref, mod_ref, cos_ref, sin_ref, win_ref, gq_ref, gkv_ref, wuq_ref, wukv_ref,
                        qn_ref, qr_ref, kv_ref, kr_ref, *, d, n_lat_tiles, q_scale):
    a = _adaln(x_ref[...], g_ref, mod_ref, 0, 1, d).astype(BF16)
    p1 = _dot(a, win_ref[...])
    cq = (_rms(p1[:, 0:MLA_Q_RANK]) * gq_ref[...]).astype(BF16)
    ckv = (_rms(p1[:, MLA_Q_RANK:MLA_Q_RANK + MLA_KV_RANK]) * gkv_ref[...]).astype(BF16)
    q = _dot(cq, wuq_ref[...]) * q_scale
    n_nope = qn_ref.shape[1]
    qn_ref[...] = q[:, 0:n_nope].astype(qn_ref.dtype)
    kv_ref[...] = _dot(ckv, wukv_ref[...]).astype(kv_ref.dtype)
    is_lat = pl.program_id(0) < n_lat_tiles
    cosf, sinf = cos_ref[...], sin_ref[...]
    first = _first_half_lanes(cosf.shape)
    for c in range(qr_ref.shape[1] // LANES):
        y = q[:, n_nope + c * LANES:n_nope + (c + 1) * LANES]
        qr_ref[:, c * LANES:(c + 1) * LANES] = _rope_chunk(y, cosf, sinf, first, is_lat).astype(qr_ref.dtype)
    k_rope = p1[:, MLA_Q_RANK + MLA_KV_RANK:MLA_Q_RANK + MLA_KV_RANK + LANES]
    kr_ref[...] = _rope_chunk(k_rope, cosf, sinf, first, is_lat).astype(kr_ref.dtype)


def _mla_in_proj(hs, g, mod, w_in, g_q, w_uq, g_kv, w_ukv, n_nope, q_scale, cosf, sinf, n_lat_tiles, tiles_per_batch,
                 nb):
    n_rows, d = hs.shape
    tm = ROW_TILE
    n_q, n_kv = w_uq.shape[1], w_ukv.shape[1]
    const = lambda shape: pl.BlockSpec(shape, lambda i: (0, 0))
    row = lambda w: pl.BlockSpec((tm, w), lambda i: (i, 0))
    return pl.pallas_call(
        functools.partial(_mla_in_proj_kernel, d=d, n_lat_tiles=n_lat_tiles, q_scale=q_scale),
        grid=(n_rows // tm,),
        in_specs=[
            row(d), const((1, d)), _mod_spec(mod, tiles_per_batch, nb),
            pl.BlockSpec((tm, LANES), lambda i: (i % tiles_per_batch, 0)),
            pl.BlockSpec((tm, LANES), lambda i: (i % tiles_per_batch, 0)),
            const(w_in.shape), const((1, MLA_Q_RANK)), const((1, MLA_KV_RANK)), const(w_uq.shape), const(w_ukv.shape),
        ],
        out_specs=[row(n_nope), row(n_q - n_nope), row(n_kv), row(LANES)],
        out_shape=[jax.ShapeDtypeStruct((n_rows, n_nope), BF16), jax.ShapeDtypeStruct((n_rows, n_q - n_nope), BF16),
                   jax.ShapeDtypeStruct((n_rows, n_kv), BF16), jax.ShapeDtypeStruct((n_rows, LANES), BF16)],
        compiler_params=_cparams(("parallel",)),
        name="mla_in_proj",
    )(hs, g.reshape(1, d), mod, cosf, sinf, w_in, g_q.reshape(1, -1), g_kv.reshape(1, -1), w_uq, w_ukv)


def _proj_resid_kernel(a_ref, a2_ref, w_ref, h_ref, g_ref, mod_ref, o_ref, *, gate_idx, d, n1):
    def body(src_ref):
        y = _dot(src_ref[...].astype(BF16), w_ref[...])
        gate = mod_ref[0][:, gate_idx * d:(gate_idx + 1) * d]
        o_ref[...] = h_ref[...] + gate * (_rms(y) * g_ref[...])

    pl.when(pl.program_id(0) < n1)(lambda: body(a_ref))
    pl.when(pl.program_id(0) >= n1)(lambda: body(a2_ref))


def _proj_resid(a, a2, w, hs, g, mod, gate_idx, n_rows, tiles_per_batch, nb, name):
    k, d = w.shape
    tm = ROW_TILE
    n1 = a.shape[0] // tm
    return pl.pallas_call(
        functools.partial(_proj_resid_kernel, gate_idx=gate_idx, d=d, n1=n1),
        grid=(n_rows // tm,),
        in_specs=[
            pl.BlockSpec((tm, k), lambda i: (jnp.minimum(i, n1 - 1), 0)),
            pl.BlockSpec((tm, k), lambda i: (jnp.maximum(i - n1, 0), 0)),
            pl.BlockSpec((k, d), lambda i: (0, 0)),
            pl.BlockSpec((tm, d), lambda i: (i, 0)),
            pl.BlockSpec((1, d), lambda i: (0, 0)),
            _mod_spec(mod, tiles_per_batch, nb),
        ],
        out_specs=pl.BlockSpec((tm, d), lambda i: (i, 0)),
        out_shape=jax.ShapeDtypeStruct((n_rows, d), F32),
        compiler_params=_cparams(("parallel",)),
        name=name,
    )(a, a2, w, hs, g.reshape(1, d), mod)


def _ffn_kernel(h_ref, g2_ref, g3_ref, mod_ref, wg_ref, wu_ref, wo_ref, o_ref, a_scr, acc_scr, *, d):
    j = pl.program_id(1)

    @pl.when(j == 0)
    def _():
        m = mod_ref[0]
        y = _rms(h_ref[...]) * g2_ref[...]
        y = y * (1.0 + m[:, 4 * d:5 * d]) + m[:, 3 * d:4 * d]
        a_scr[...] = y.astype(BF16)
        acc_scr[...] = jnp.zeros_like(acc_scr)

    a = a_scr[...]
    gt = _dot(a, wg_ref[...])
    up = _dot(a, wu_ref[...])
    act = (gt * _sigmoid(gt) * up).astype(BF16)
    acc_scr[...] += _dot(act, wo_ref[...])

    @pl.when(j == pl.num_programs(1) - 1)
    def _():
        m = mod_ref[0]
        o_ref[...] = h_ref[...] + m[:, 5 * d:6 * d] * (_rms(acc_scr[...]) * g3_ref[...])


def _ffn(hs, g2, g3, mod, w_in, w_out, layer, n_rows, tiles_per_batch, nb, name):
    d = hs.shape[1]
    f = w_out.shape[1]
    tm = ROW_TILE
    tf = 512
    nf = f // tf
    return pl.pallas_call(
        functools.partial(_ffn_kernel, d=d),
        grid=(n_rows // tm, nf),
        in_specs=[
            pl.BlockSpec((tm, d), lambda i, j: (i, 0)),
            pl.BlockSpec((1, d), lambda i, j: (0, 0)),
            pl.BlockSpec((1, d), lambda i, j: (0, 0)),
            _mod_spec(mod, tiles_per_batch, nb),
            pl.BlockSpec((None, d, tf), lambda i, j: (layer, 0, j)),
            pl.BlockSpec((None, d, tf), lambda i, j: (layer, 0, nf + j)),
            pl.BlockSpec((None, tf, d), lambda i, j: (layer, j, 0)),
        ],
        out_specs=pl.BlockSpec((tm, d), lambda i, j: (i, 0)),
        out_shape=jax.ShapeDtypeStruct((n_rows, d), F32),
        scratch_shapes=[pltpu.VMEM((tm, d), BF16), pltpu.VMEM((tm, d), F32)],
        compiler_params=_cparams(("parallel", "arbitrary")),
        name=name,
    )(hs, g2.reshape(1, d), g3.reshape(1, d), mod, w_in, w_in, w_out)


GLA_CHUNK = 128


def _gla_structure(cs, reverse):
    idx = np.arange(cs)
    ip = cs - 1 - idx if reverse else idx
    ii, jj = ip[:, None], ip[None, :]
    levels = []
    s = cs // 2
    while s >= 1:
        levels.append(((ii // (2 * s)) == (jj // (2 * s))) & ((ii & s) != 0) & ((jj & s) == 0))
        s //= 2
    levels.append(ii == jj)
    return np.stack(levels).astype(np.float32), (jj <= ii).astype(np.float32)


def _gla_chunk(q, k, v_b, lg, msk_ref, tri, st_ref, b_ref, *, cs, dk, reverse):
    hi = lg.astype(BF16)
    r1 = lg - hi.astype(F32)
    mid = r1.astype(BF16)
    lo = (r1 - mid.astype(F32)).astype(BF16)
    b = _dot(tri, hi) + _dot(tri, mid) + _dot(tri, lo)
    b_ref[...] = b
    yield
    last = 0 if reverse else cs - 1
    b_last = b_ref[pl.ds(last, 1), :]

    st = st_ref[...]
    q_b = q.astype(BF16)
    k_b = k.astype(BF16)
    o = _dot_nt(q_b * jnp.exp2(b).astype(BF16), st.astype(BF16))
    yield

    row = lax.broadcasted_iota(jnp.int32, (cs, 1), 0)
    ip = (cs - 1 - row) if reverse else row

    def orig(p):
        return cs - 1 - p if reverse else p

    attn = jnp.zeros((cs, cs), F32)
    lvl = 0
    s = cs // 2
    while s >= 1:
        if s == 1:
            ql = q_b * jnp.exp2(lg).astype(BF16)
            kl = k_b
        else:
            if 2 * s >= SUBLANES:
                nblk = cs // (2 * s)
                pieces = []
                for jb in range(nblk):
                    m = nblk - 1 - jb if reverse else jb
                    p = orig(m * 2 * s + s - 1)
                    pieces.append(jnp.broadcast_to(b_ref[pl.ds(p, 1), :], (2 * s, dk)))
                ref = pieces[0] if nblk == 1 else jnp.concatenate(pieces, axis=0)
            else:
                ngrp = cs // SUBLANES
                pa, pb = [], []
                for jg in range(ngrp):
                    gp = ngrp - 1 - jg if reverse else jg
                    pa.append(jnp.broadcast_to(b_ref[pl.ds(orig(gp * SUBLANES + 1), 1), :], (SUBLANES, dk)))
                    pb.append(jnp.broadcast_to(b_ref[pl.ds(orig(gp * SUBLANES + 5), 1), :], (SUBLANES, dk)))
                ref = jnp.where((ip & 4) == 0, jnp.concatenate(pa, axis=0), jnp.concatenate(pb, axis=0))
            fac = jnp.exp2(-jnp.abs(b - ref)).astype(BF16)
            ql = q_b * fac
            kl = k_b * fac
        attn = attn + msk_ref[lvl] * _dot_nt(ql, kl)
        yield
        lvl += 1
        s //= 2
    attn = attn + msk_ref[lvl] * _dot_nt(q_b, k_b)

    k_dec = k_b * jnp.exp2(b_last - b).astype(BF16)
    st_ref[...] = st * jnp.exp2(b_last) + _dot_tn(v_b, k_dec)
    yield
    return o + _dot(attn.astype(BF16), v_b)


def _interleave(gens):
    results = [None] * len(gens)
    active = list(range(len(gens)))
    while active:
        for idx in list(active):
            try:
                next(gens[idx])
            except StopIteration as stop:
                results[idx] = stop.value
                active.remove(idx)
    return results


def _gla_scan_kernel(qf_ref, kf_ref, vf_ref, gf_ref, qb_ref, kb_ref, vb_ref, gb_ref, wu_ref, bg_ref,
                     msk_ref, tri_ref, of_ref, ob_ref, st_scr, b_scr, *, cs, dk, dv):
    @pl.when(pl.program_id(1) == 0)
    def _():
        st_scr[...] = jnp.zeros_like(st_scr)

    dirs = ((qf_ref, kf_ref, vf_ref, gf_ref, of_ref), (qb_ref, kb_ref, vb_ref, gb_ref, ob_ref))
    chains, dests = [], []
    for dr, (q_ref, k_ref, v_ref, gd_ref, o_ref) in enumerate(dirs):
        z = _dot(gd_ref[...].astype(BF16), wu_ref[dr]) + bg_ref[dr]
        lg_all = (jnp.minimum(z, 0.0) - jnp.log1p(jnp.exp(-jnp.abs(z)))) * (LOG2E / GLA_TAU)
        for h in range(GLA_HEADS):
            ks = slice(h * dk, (h + 1) * dk)
            vs = slice(h * dv, (h + 1) * dv)
            chains.append(_gla_chunk(q_ref[:, ks] * (dk ** -0.5), k_ref[:, ks], v_ref[:, vs].astype(BF16),
                                     lg_all[:, ks], msk_ref.at[dr], tri_ref[dr], st_scr.at[dr, h], b_scr.at[dr, h],
                                     cs=cs, dk=dk, reverse=bool(dr)))
            dests.append((o_ref, vs))
    for (o_ref, vs), o in zip(dests, _interleave(chains)):
        o_ref[:, vs] = o


def _gla_scan(p, gd, wu_ext, bg, b_sz, t_lat, t_ctx):
    cs = GLA_CHUNK
    qkw = wu_ext.shape[-1]
    dk = qkw // GLA_HEADS
    vw = 2 * qkw
    dv = vw // GLA_HEADS
    n_tok = p.shape[0]
    ncc, ncl = t_ctx // cs, t_lat // cs
    ctx_blk0 = (b_sz * t_lat) // cs
    structs = [_gla_structure(cs, rev) for rev in (False, True)]
    masks = np.stack([m for m, _ in structs])
    tri = np.stack([t for _, t in structs])

    def rowblk(rev):
        def f(b, s):
            if rev:
                cc, lc = ncc - 1 - s, ncl - 1 - (s - ncc)
            else:
                cc, lc = s, s - ncc
            return jnp.where(s < ncc, ctx_blk0 + b * ncc + cc, b * ncl + lc)
        return f

    def dir_specs(rev):
        rb = rowblk(rev)
        return [pl.BlockSpec((cs, qkw), lambda b, s: (rb(b, s), 0)),
                pl.BlockSpec((cs, qkw), lambda b, s: (rb(b, s), 1)),
                pl.BlockSpec((cs, vw), lambda b, s: (rb(b, s), 1)),
                pl.BlockSpec((cs, LANES), lambda b, s: (rb(b, s), 0))]

    return pl.pallas_call(
        functools.partial(_gla_scan_kernel, cs=cs, dk=dk, dv=dv),
        grid=(b_sz, ncc + ncl),
        in_specs=dir_specs(False) + dir_specs(True) + [
            pl.BlockSpec((2, LANES, qkw), lambda b, s: (0, 0, 0)),
            pl.BlockSpec((2, 1, qkw), lambda b, s: (0, 0, 0)),
            pl.BlockSpec(masks.shape, lambda b, s: (0, 0, 0, 0)),
            pl.BlockSpec(tri.shape, lambda b, s: (0, 0, 0)),
        ],
        out_specs=[pl.BlockSpec((cs, vw), lambda b, s: (rowblk(False)(b, s), 0)),
                   pl.BlockSpec((cs, vw), lambda b, s: (rowblk(True)(b, s), 0))],
        out_shape=[jax.ShapeDtypeStruct((n_tok, vw), F32)] * 2,
        scratch_shapes=[pltpu.VMEM((2, GLA_HEADS, dv, dk), F32), pltpu.VMEM((2, GLA_HEADS, cs, dk), F32)],
        compiler_params=_cparams(("parallel", "arbitrary")),
        name="gla_scan",
    )(p, p, p, gd, p, p, p, gd, wu_ext, bg, jnp.asarray(masks), jnp.asarray(tri, dtype=BF16))


def _gla_combine_kernel(of_ref, ob_ref, r_ref, gh_ref, o_ref, *, dv):
    for h in range(GLA_HEADS):
        sl = slice(h * dv, (h + 1) * dv)
        y = _rms(of_ref[:, sl] + ob_ref[:, sl]) * gh_ref[...]
        r = r_ref[:, sl]
        o_ref[:, sl] = (y * (r * _sigmoid(r))).astype(o_ref.dtype)


def _gla_combine(o_f, o_b, r, g_head, n_rows):
    d = o_f.shape[1]
    dv = d // GLA_HEADS
    tm = ROW_TILE
    return pl.pallas_call(
        functools.partial(_gla_combine_kernel, dv=dv),
        grid=(n_rows // tm,),
        in_specs=[
            pl.BlockSpec((tm, d), lambda i: (i, 0)),
            pl.BlockSpec((tm, d), lambda i: (i, 0)),
            pl.BlockSpec((tm, d), lambda i: (i, 0)),
            pl.BlockSpec((1, dv), lambda i: (0, 0)),
        ],
        out_specs=pl.BlockSpec((tm, d), lambda i: (i, 0)),
        out_shape=jax.ShapeDtypeStruct((n_rows, d), BF16),
        compiler_params=_cparams(("parallel",)),
        name="gla_combine",
    )(o_f, o_b, r, g_head.reshape(1, dv))


def _rope_tables(t_lat):
    t = jnp.arange(t_lat)
    row = (t // GRID_W).astype(F32)
    col = (t % GRID_W).astype(F32)
    n_freq = ROPE_DIM // 4
    inv_freq = ROPE_THETA ** (-jnp.arange(n_freq, dtype=F32) / n_freq)
    ang = jnp.concatenate([row[:, None] * inv_freq, col[:, None] * inv_freq], axis=-1)
    cos, sin = jnp.cos(ang), jnp.sin(ang)
    reps = LANES // ROPE_DIM
    return jnp.tile(jnp.concatenate([cos, cos], axis=-1), (1, reps)), jnp.tile(jnp.concatenate([-sin, sin], axis=-1), (1, reps))


MLA_TQ = 512
MLA_TK = 512
MLA_HEADS_PER_STEP = 2


def _mla_attn_kernel(qn_ref, qr_ref, qnc_ref, qrc_ref, knc_ref, vc_ref, krc_ref, knl_ref, vl_ref, krl_ref,
                     o_ref, oc_ref, kcat, vext, *, tc, tl):
    t = pl.program_id(2)
    nh = MLA_HEADS_PER_STEP
    rd = ROPE_DIM
    vw = MLA_V + LANES

    def qcat(n_ref, r_ref, h):
        return jnp.concatenate([n_ref[:, h * MLA_NOPE:(h + 1) * MLA_NOPE], r_ref[:, h * rd:(h + 1) * rd]], axis=1)

    def finish(acc):
        return acc[:, 0:MLA_V] / acc[:, MLA_V:vw]

    @pl.when(t == 0)
    def _():
        for h in range(nh):
            ns = slice(h * MLA_NOPE, (h + 1) * MLA_NOPE)
            kcat[h, 0:tc, 0:MLA_NOPE] = knc_ref[:, ns]
            kcat[h, 0:tc, MLA_NOPE:MLA_NOPE + rd] = krc_ref[:, 0:rd]
            kcat[h, tc:tc + tl, 0:MLA_NOPE] = knl_ref[:, ns]
            kcat[h, tc:tc + tl, MLA_NOPE:MLA_NOPE + rd] = krl_ref[:, 0:rd]
            vext[0:tc, h * vw:h * vw + MLA_V] = vc_ref[:, h * MLA_V:(h + 1) * MLA_V]
            vext[tc:tc + tl, h * vw:h * vw + MLA_V] = vl_ref[:, h * MLA_V:(h + 1) * MLA_V]
            vext[:, h * vw + MLA_V:(h + 1) * vw] = jnp.ones((tc + tl, LANES), BF16)
        for h in range(nh):
            s = _dot_nt(qcat(qnc_ref, qrc_ref, h), kcat[h, 0:tc, :])
            p = jnp.exp2(s - jnp.max(s, axis=-1, keepdims=True))
            acc = _dot(p.astype(BF16), vext[0:tc, h * vw:(h + 1) * vw])
            oc_ref[:, h * MLA_V:(h + 1) * MLA_V] = finish(acc).astype(oc_ref.dtype)

    for h in range(nh):
        qc = qcat(qn_ref, qr_ref, h)
        s = _dot_nt(qc, kcat[h, 0:tc, :])
        m = jnp.max(s, axis=-1, keepdims=True)
        acc = _dot(jnp.exp2(s - m).astype(BF16), vext[0:tc, h * vw:(h + 1) * vw])
        for c0 in range(tc, tc + tl, MLA_TK):
            s = _dot_nt(qc, kcat[h, c0:c0 + MLA_TK, :])
            m_new = jnp.maximum(m, jnp.max(s, axis=-1, keepdims=True))
            acc = jnp.exp2(m - m_new) * acc + _dot(jnp.exp2(s - m_new).astype(BF16),
                                                   vext[c0:c0 + MLA_TK, h * vw:(h + 1) * vw])
            m = m_new
        o_ref[:, h * MLA_V:(h + 1) * MLA_V] = finish(acc).astype(o_ref.dtype)


def _mla_attention(qn, qr, kv, kr, b_sz, t_lat, t_ctx):
    hd = qn.shape[1]
    nh = MLA_HEADS_PER_STEP
    n_heads = hd // MLA_NOPE
    tq = MLA_TQ
    assert t_lat % tq == 0 and t_lat % MLA_TK == 0
    nlt = t_lat // tq
    ctx_blk0 = (b_sz * t_lat) // t_ctx
    vcol0 = n_heads // nh
    wn, wr = nh * MLA_NOPE, nh * ROPE_DIM
    return pl.pallas_call(
        functools.partial(_mla_attn_kernel, tc=t_ctx, tl=t_lat),
        grid=(b_sz, n_heads // nh, nlt),
        in_specs=[
            pl.BlockSpec((tq, wn), lambda b, h, t: (b * nlt + t, h)),
            pl.BlockSpec((tq, wr), lambda b, h, t: (b * nlt + t, h)),
            pl.BlockSpec((t_ctx, wn), lambda b, h, t: (ctx_blk0 + b, h)),
            pl.BlockSpec((t_ctx, wr), lambda b, h, t: (ctx_blk0 + b, h)),
            pl.BlockSpec((t_ctx, wn), lambda b, h, t: (ctx_blk0 + b, h)),
            pl.BlockSpec((t_ctx, wn), lambda b, h, t: (ctx_blk0 + b, vcol0 + h)),
            pl.BlockSpec((t_ctx, LANES), lambda b, h, t: (ctx_blk0 + b, 0)),
            pl.BlockSpec((t_lat, wn), lambda b, h, t: (b, h)),
            pl.BlockSpec((t_lat, wn), lambda b, h, t: (b, vcol0 + h)),
            pl.BlockSpec((t_lat, LANES), lambda b, h, t: (b, 0)),
        ],
        out_specs=[pl.BlockSpec((tq, nh * MLA_V), lambda b, h, t: (b * nlt + t, h)),
                   pl.BlockSpec((t_ctx, nh * MLA_V), lambda b, h, t: (b, h))],
        out_shape=[jax.ShapeDtypeStruct((b_sz * t_lat, n_heads * MLA_V), BF16),
                   jax.ShapeDtypeStruct((b_sz * t_ctx, n_heads * MLA_V), BF16)],
        scratch_shapes=[pltpu.VMEM((nh, t_ctx + t_lat, MLA_NOPE + ROPE_DIM), BF16),
                        pltpu.VMEM((t_ctx + t_lat, nh * (MLA_V + LANES)), BF16)],
        compiler_params=_cparams(("parallel", "parallel", "arbitrary")),
        name="mla_attention",
    )(qn, qr, qn, qr, kv, kv, kr, kv, kv, kr)


def _swa_attn_kernel(sink_ref, q_ref, kc_ref, vc_ref, kp_ref, kc0_ref, kn_ref, vp_ref, vc0_ref, vn_ref, o_ref,
                     *, nct, nb, n_kv):
    t = pl.program_id(1)
    n = t - nct
    blk = WINDOW
    hd = SWA_HEAD_DIM
    tc = kc_ref.shape[0]
    nk = tc + 3 * blk
    npair = SWA_GROUP // 2

    qi = lax.broadcasted_iota(jnp.int32, (blk, 3 * blk), 0)
    kj = lax.broadcasted_iota(jnp.int32, (blk, 3 * blk), 1)
    lo = jnp.where(n > 0, 0, blk)
    hi = jnp.where(t < nct, 0, jnp.where(n < nb - 1, 3 * blk, 2 * blk))
    win_ok = (jnp.abs(qi - kj + blk) <= WINDOW) & (kj >= lo) & (kj < hi)
    bias = jnp.concatenate([jnp.zeros((blk, tc), F32), jnp.where(win_ok, 0.0, NEG_BIG)], axis=1)
    bias = jnp.concatenate([bias] * npair, axis=0)
    lane = lax.broadcasted_iota(jnp.int32, (blk, 2 * hd), 1)
    zeros = jnp.zeros((nk, hd), BF16)
    ones = jnp.ones((nk, hd), BF16)

    for kvh in range(n_kv):
        ks = slice(kvh * hd, (kvh + 1) * hd)
        k_all = jnp.concatenate([kc_ref[:, ks], kp_ref[:, ks], kc0_ref[:, ks], kn_ref[:, ks]], axis=0)
        v_all = jnp.concatenate([vc_ref[:, ks], vp_ref[:, ks], vc0_ref[:, ks], vn_ref[:, ks]], axis=0)
        k_bd = jnp.concatenate([jnp.concatenate([k_all, zeros], axis=1),
                                jnp.concatenate([zeros, k_all], axis=1)], axis=0)
        h0 = kvh * SWA_GROUP
        q_st = jnp.concatenate([q_ref[:, (h0 + 2 * j) * hd:(h0 + 2 * j + 2) * hd] for j in range(npair)], axis=0)
        s = _dot_nt(q_st, k_bd)
        ps, ms = [], []
        for e in range(2):
            sink_col = jnp.concatenate([jnp.full((blk, 1), sink_ref[h0 + 2 * j + e] * LOG2E, F32)
                                        for j in range(npair)], axis=0)
            s_e = s[:, e * nk:(e + 1) * nk] + bias
            m = jnp.maximum(jnp.max(s_e, axis=-1, keepdims=True), sink_col)
            p = jnp.exp2(s_e - m).astype(BF16)
            ps += [p[:, 0:tc], p[:, tc:]]
            ms.append(sink_col - m)

        def v_ext(r0, r1, e):
            v, z, o = v_all[r0:r1], zeros[r0:r1], ones[r0:r1]
            return jnp.concatenate([z, v, z, o] if e else [v, z, o, z], axis=1)

        tail = tc + 2 * blk
        acc = _dot(jnp.concatenate([ps[1][:, 2 * blk:], ps[3][:, 2 * blk:]], axis=1),
                   jnp.concatenate([v_ext(tail, nk, 0), v_ext(tail, nk, 1)], axis=0))
        for e in range(2):
            acc = acc + _dot(ps[2 * e], v_ext(0, tc, e)) + _dot(ps[2 * e + 1][:, 0:2 * blk], v_ext(tc, tail, e))
        for j in range(npair):
            r = slice(j * blk, (j + 1) * blk)
            sink_term = jnp.exp2(jnp.where(lane < hd, ms[0][r], ms[1][r]))
            o = acc[r, 0:2 * hd] / (acc[r, 2 * hd:4 * hd] + sink_term)
            o_ref[:, (h0 + 2 * j) * hd:(h0 + 2 * j + 2) * hd] = o.astype(o_ref.dtype)


def _swa_attention(qkv, sinks, b_sz, t_lat, t_ctx):
    n_tok = qkv.shape[0]
    blk = WINDOW
    hd = SWA_HEAD_DIM
    n_q = sinks.shape[0]
    n_kv = n_q // SWA_GROUP
    kvw = n_kv * hd
    kcol = (n_q * hd) // kvw
    vcol = kcol + 1
    nct, nb = t_ctx // blk, t_lat // blk
    ctx_blk0 = (b_sz * t_lat) // blk
    ctx_row0 = (b_sz * t_lat) // t_ctx

    def qrow(b, t):
        return jnp.where(t < nct, ctx_blk0 + b * nct + t, b * nb + t - nct)

    def krow(b, t, off):
        return b * nb + jnp.clip(t - nct + off, 0, nb - 1)

    kspec = lambda off: pl.BlockSpec((blk, kvw), lambda b, t: (krow(b, t, off), kcol))
    vspec = lambda off: pl.BlockSpec((blk, kvw), lambda b, t: (krow(b, t, off), vcol))
    return pl.pallas_call(
        functools.partial(_swa_attn_kernel, nct=nct, nb=nb, n_kv=n_kv),
        grid=(b_sz, nct + nb),
        in_specs=[
            pl.BlockSpec(memory_space=pltpu.SMEM),
            pl.BlockSpec((blk, n_q * hd), lambda b, t: (qrow(b, t), 0)),
            pl.BlockSpec((t_ctx, kvw), lambda b, t: (ctx_row0 + b, kcol)),
            pl.BlockSpec((t_ctx, kvw), lambda b, t: (ctx_row0 + b, vcol)),
            kspec(-1), kspec(0), kspec(1), vspec(-1), vspec(0), vspec(1),
        ],
        out_specs=pl.BlockSpec((blk, n_q * hd), lambda b, t: (qrow(b, t), 0)),
        out_shape=jax.ShapeDtypeStruct((n_tok, n_q * hd), BF16),
        compiler_params=_cparams(("parallel", "arbitrary")),
        name="swa_attention",
    )(sinks, qkv, qkv, qkv, qkv, qkv, qkv, qkv, qkv, qkv)


def kernel(x, c, ctx, c_ctx, w_ada, b_ada, g_norm, w_ffn_in, w_ffn_out, gla_w_in, gla_w_gate_down, gla_w_gate_up, gla_b_gate, gla_g_head, gla_w_out, mla_w_in, mla_g_q, mla_w_uq, mla_g_kv, mla_w_ukv, mla_w_out, swa_w_in, swa_sinks, swa_w_out):
    b_sz, t_lat, d = x.shape
    t_ctx = ctx.shape[1]
    depth = w_ada.shape[0]
    n_lat, n_ctx = b_sz * t_lat, b_sz * t_ctx
    n_tok = n_lat + n_ctx
    tm = ROW_TILE
    assert t_lat % tm == 0 and n_ctx % tm == 0 and b_sz < MOD_ROWS
    tpb = t_lat // tm
    n_lat_tiles = n_lat // tm

    hs = jnp.concatenate([x.reshape(n_lat, d), ctx.reshape(n_ctx, d)], axis=0)
    cond = jnp.zeros((MOD_ROWS, d), F32).at[:b_sz].set(c).at[b_sz].set(c_ctx)
    mod_all = _modulation(cond, w_ada, b_ada)
    cosf, sinf = _rope_tables(t_lat)
    common = dict(tiles_per_batch=tpb, nb=b_sz)
    w_ffn_in_b, w_ffn_out_b = w_ffn_in.astype(BF16), w_ffn_out.astype(BF16)

    for i in range(depth):
        kind, j = i % 3, i // 3
        last = i == depth - 1
        n_out = n_lat if last else n_tok
        mod = mod_all[i].reshape(MOD_ROWS, 1, 6 * d)

        if kind == 0:
            rank = GLA_GATE_RANK
            qk_w = gla_w_gate_up.shape[-1]
            pad = LANES - 2 * rank
            w_gd = jnp.concatenate([gla_w_gate_down[j, 0], gla_w_gate_down[j, 1], jnp.zeros((d, pad), F32)],
                                   axis=1).astype(BF16)
            big = 2 * tm
            tm_in = big if t_lat % big == 0 and n_ctx % big == 0 else tm
            qkv, r, gd = _gla_in_proj(hs, g_norm[i, 0], mod, gla_w_in[j].astype(BF16), w_gd, 2 * qk_w + d, tm_in,
                                      t_lat // tm_in, b_sz)
            wu_ext = jnp.zeros((2, LANES, qk_w), F32)
            for dr in range(2):
                wu_ext = wu_ext.at[dr, dr * rank:(dr + 1) * rank].set(gla_w_gate_up[j, dr])
            o_f, o_b = _gla_scan(qkv, gd, wu_ext.astype(BF16), gla_b_gate[j].reshape(2, 1, qk_w), b_sz, t_lat, t_ctx)
            a = a2 = _gla_combine(o_f, o_b, r, gla_g_head[j], n_out)
            w_o = gla_w_out[j]
        elif kind == 1:
            n_heads = mla_w_out.shape[1] // MLA_V
            w_in = jnp.concatenate([mla_w_in[j], jnp.zeros((d, LANES - ROPE_DIM), F32)], axis=1).astype(BF16)
            w_uq = mla_w_uq[j].reshape(MLA_Q_RANK, n_heads, MLA_NOPE + ROPE_DIM)
            w_uq = jnp.concatenate([w_uq[:, :, :MLA_NOPE].reshape(MLA_Q_RANK, -1),
                                    w_uq[:, :, MLA_NOPE:].reshape(MLA_Q_RANK, -1)], axis=1).astype(BF16)
            w_ukv = mla_w_ukv[j].reshape(MLA_KV_RANK, n_heads, MLA_NOPE + MLA_V)
            w_ukv = jnp.concatenate([w_ukv[:, :, :MLA_NOPE].reshape(MLA_KV_RANK, -1),
                                     w_ukv[:, :, MLA_NOPE:].reshape(MLA_KV_RANK, -1)], axis=1).astype(BF16)
            q_scale = (MLA_NOPE + ROPE_DIM) ** -0.5 * LOG2E
            qn, qr, kv, kr = _mla_in_proj(hs, g_norm[i, 0], mod, w_in, mla_g_q[j], w_uq, mla_g_kv[j], w_ukv,
                                          n_heads * MLA_NOPE, q_scale, cosf, sinf, n_lat_tiles, **common)
            a, a2 = _mla_attention(qn, qr, kv, kr, b_sz, t_lat, t_ctx)
            w_o = mla_w_out[j]
        else:
            n_q = swa_sinks.shape[1]
            qkw = (n_q + n_q // SWA_GROUP) * SWA_HEAD_DIM
            qkv = _swa_in_proj(hs, g_norm[i, 0], mod, swa_w_in[j].astype(BF16), cosf, sinf, qkw // LANES,
                               (n_q * SWA_HEAD_DIM) // LANES, SWA_HEAD_DIM ** -0.5 * LOG2E, n_lat_tiles, **common)
            a = a2 = _swa_attention(qkv, swa_sinks[j], b_sz, t_lat, t_ctx)
            w_o = swa_w_out[j]

        hs = _proj_resid(a, a2, w_o.astype(BF16), hs, g_norm[i, 1], mod, 2, n_out, name="mixer_out_proj", **common)
        hs = _ffn(hs, g_norm[i, 2], g_norm[i, 3], mod, w_ffn_in_b, w_ffn_out_b, i, n_out, name="ffn", **common)

    return hs.reshape(b_sz, t_lat, d)
```

```python
import functools

import numpy as np
import jax
import jax.numpy as jnp
from jax import lax
from jax.experimental import pallas as pl
from jax.experimental.pallas import tpu as pltpu

F32 = jnp.float32
BF16 = jnp.bfloat16

EPS = 1e-6
ROPE_THETA = 10000.0
ROPE_DIM = 64
GRID_W = 64
WINDOW = 128
GLA_HEADS = 4
GLA_GATE_RANK = 16
GLA_TAU = 16.0
MLA_Q_RANK = 512
MLA_KV_RANK = 512
MLA_NOPE = 128
MLA_V = 128
SWA_HEAD_DIM = 64
SWA_GROUP = 8

LANES = 128
SUBLANES = 8
VMEM_LIMIT = 56 * 1024 * 1024
ROW_TILE = 512
MOD_ROWS = 16
NEG_BIG = -1e30
LOG2E = 1.4426950408889634


def _cparams(sem):
    return pltpu.CompilerParams(dimension_semantics=sem, vmem_limit_bytes=VMEM_LIMIT)


def _dot(a, b):
    return jnp.dot(a, b, preferred_element_type=F32)


def _dot_nt(a, b):
    return lax.dot_general(a, b, (((1,), (1,)), ((), ())), preferred_element_type=F32)


def _dot_tn(a, b):
    return lax.dot_general(a, b, (((0,), (0,)), ((), ())), preferred_element_type=F32)


def _sigmoid(x):
    return 1.0 / (1.0 + jnp.exp(-x))


def _rms(x):
    return x * lax.rsqrt(jnp.mean(x * x, axis=-1, keepdims=True) + EPS)


def _mod_kernel(c_ref, w_ref, b_ref, o_ref):
    c = c_ref[...]
    s = (c * _sigmoid(c)).astype(BF16)
    o_ref[0] = _dot(s, w_ref[0].astype(BF16)) + b_ref[0]


def _modulation(cond, w_ada, b_ada):
    depth, d, n = w_ada.shape
    tn = 1024
    return pl.pallas_call(
        _mod_kernel,
        grid=(depth, n // tn),
        in_specs=[
            pl.BlockSpec((MOD_ROWS, d), lambda l, j: (0, 0)),
            pl.BlockSpec((1, d, tn), lambda l, j: (l, 0, j)),
            pl.BlockSpec((1, 1, tn), lambda l, j: (l, 0, j)),
        ],
        out_specs=pl.BlockSpec((1, MOD_ROWS, tn), lambda l, j: (l, 0, j)),
        out_shape=jax.ShapeDtypeStruct((depth, MOD_ROWS, n), F32),
        compiler_params=_cparams(("parallel", "parallel")),
        name="modulation",
    )(cond, w_ada, b_ada.reshape(depth, 1, n))


NORM_ROWS = 16


def _adaln_rows(dst_ref, x_ref, g_ref, mod_ref, shift_idx, scale_idx, d):
    m = mod_ref[0]
    gs = g_ref[...] * (1.0 + m[:, scale_idx * d:(scale_idx + 1) * d])
    sh = m[:, shift_idx * d:(shift_idx + 1) * d]
    for r in range(0, x_ref.shape[0], NORM_ROWS):
        rows = pl.ds(r, NORM_ROWS)
        dst_ref[rows, :] = (_rms(x_ref[rows, :]) * gs + sh).astype(dst_ref.dtype)


def _resid_norm_rows(o_ref, h_ref, y_ref, g_ref, mod_ref, gate_idx, d):
    gg = mod_ref[0][:, gate_idx * d:(gate_idx + 1) * d] * g_ref[...]
    for r in range(0, h_ref.shape[0], NORM_ROWS):
        rows = pl.ds(r, NORM_ROWS)
        o_ref[rows, :] = h_ref[rows, :] + _rms(y_ref[rows, :]) * gg


def _rope_chunk(y, cosf, sinf, first, is_lat):
    partner = jnp.where(first, pltpu.roll(y, LANES - ROPE_DIM // 2, 1), pltpu.roll(y, ROPE_DIM // 2, 1))
    return jnp.where(is_lat, y * cosf + partner * sinf, y)


def _first_half_lanes(shape):
    lane = lax.broadcasted_iota(jnp.int32, shape, 1)
    return (lane % ROPE_DIM) < (ROPE_DIM // 2)


def _mod_spec(mod, tiles_per_batch, nb):
    return pl.BlockSpec((1, 1, mod.shape[-1]), lambda i, *_: (jnp.minimum(i // tiles_per_batch, nb), 0, 0))


def _gla_in_proj_kernel(x_ref, g_ref, mod_ref, w_ref, wgd_ref, qkv_ref, r_ref, gd_ref, a_scr, *, d, n_qkv_tiles):
    j = pl.program_id(1)

    @pl.when(j == 0)
    def _():
        _adaln_rows(a_scr, x_ref, g_ref, mod_ref, 0, 1, d)
        gd_ref[...] = _dot(a_scr[...], wgd_ref[...])

    o = _dot(a_scr[...], w_ref[...])

    @pl.when(j < n_qkv_tiles)
    def _():
        qkv_ref[...] = o.astype(qkv_ref.dtype)

    @pl.when(j >= n_qkv_tiles)
    def _():
        r_ref[...] = o


def _gla_in_proj(hs, g, mod, w, w_gd, n_qkv, tm, tiles_per_batch, nb):
    n_rows, d = hs.shape
    n = w.shape[1]
    tn = 1024
    nq = n_qkv // tn
    n_side = w_gd.shape[1]
    return pl.pallas_call(
        functools.partial(_gla_in_proj_kernel, d=d, n_qkv_tiles=nq),
        grid=(n_rows // tm, n // tn),
        in_specs=[
            pl.BlockSpec((tm, d), lambda i, j: (i, 0)),
            pl.BlockSpec((1, d), lambda i, j: (0, 0)),
            _mod_spec(mod, tiles_per_batch, nb),
            pl.BlockSpec((d, tn), lambda i, j: (0, j)),
            pl.BlockSpec((d, n_side), lambda i, j: (0, 0)),
        ],
        out_specs=[pl.BlockSpec((tm, tn), lambda i, j: (i, jnp.minimum(j, nq - 1))),
                   pl.BlockSpec((tm, tn), lambda i, j: (i, jnp.maximum(j - nq, 0))),
                   pl.BlockSpec((tm, n_side), lambda i, j: (i, 0))],
        out_shape=[jax.ShapeDtypeStruct((n_rows, n_qkv), BF16),
                   jax.ShapeDtypeStruct((n_rows, n - n_qkv), F32),
                   jax.ShapeDtypeStruct((n_rows, n_side), F32)],
        scratch_shapes=[pltpu.VMEM((tm, d), BF16)],
        compiler_params=_cparams(("parallel", "arbitrary")),
        name="gla_in_proj",
    )(hs, g.reshape(1, d), mod, w, w_gd)


def _swa_in_proj_kernel(x_ref, g_ref, mod_ref, cos_ref, sin_ref, w_ref, o_ref, a_scr, *, d, n_lat_tiles, n_roped,
                        n_scaled, out_scale):
    _adaln_rows(a_scr, x_ref, g_ref, mod_ref, 0, 1, d)
    y_all = _dot(a_scr[...], w_ref[...])
    is_lat = pl.program_id(0) < n_lat_tiles
    cosf, sinf = cos_ref[...], sin_ref[...]
    first = _first_half_lanes(cosf.shape)
    for c in range(y_all.shape[1] // LANES):
        y = y_all[:, c * LANES:(c + 1) * LANES]
        if c < n_roped:
            y = _rope_chunk(y, cosf, sinf, first, is_lat)
        if c < n_scaled:
            y = y * out_scale
        o_ref[:, c * LANES:(c + 1) * LANES] = y.astype(o_ref.dtype)


def _swa_in_proj(hs, g, mod, w, cosf, sinf, n_roped, n_scaled, out_scale, n_lat_tiles, tiles_per_batch, nb):
    n_rows, d = hs.shape
    n = w.shape[1]
    tm = ROW_TILE
    return pl.pallas_call(
        functools.partial(_swa_in_proj_kernel, d=d, n_lat_tiles=n_lat_tiles, n_roped=n_roped, n_scaled=n_scaled,
                          out_scale=out_scale),
        grid=(n_rows // tm,),
        in_specs=[
            pl.BlockSpec((tm, d), lambda i: (i, 0)),
            pl.BlockSpec((1, d), lambda i: (0, 0)),
            _mod_spec(mod, tiles_per_batch, nb),
            pl.BlockSpec((tm, LANES), lambda i: (i % tiles_per_batch, 0)),
            pl.BlockSpec((tm, LANES), lambda i: (i % tiles_per_batch, 0)),
            pl.BlockSpec((d, n), lambda i: (0, 0)),
        ],
        out_specs=pl.BlockSpec((tm, n), lambda i: (i, 0)),
        out_shape=jax.ShapeDtypeStruct((n_rows, n), BF16),
        scratch_shapes=[pltpu.VMEM((tm, d), BF16)],
        compiler_params=_cparams(("parallel",)),
        name="swa_in_proj",
    )(hs, g.reshape(1, d), mod, cosf, sinf, w)


def _mla_in_proj_kernel(x_ref, g_ref, mod_ref, cos_ref, sin_ref, win_ref, gq_ref, gkv_ref, wuq_ref, wukv_ref,
                        qn_ref, qr_ref, kv_ref, kr_ref, a_scr, *, d, n_lat_tiles, q_scale):
    _adaln_rows(a_scr, x_ref, g_ref, mod_ref, 0, 1, d)
    p1 = _dot(a_scr[...], win_ref[...])
    cq = (_rms(p1[:, 0:MLA_Q_RANK]) * gq_ref[...]).astype(BF16)
    ckv = (_rms(p1[:, MLA_Q_RANK:MLA_Q_RANK + MLA_KV_RANK]) * gkv_ref[...]).astype(BF16)
    q = _dot(cq, wuq_ref[...]) * q_scale
    n_nope = qn_ref.shape[1]
    qn_ref[...] = q[:, 0:n_nope].astype(qn_ref.dtype)
    kv_ref[...] = _dot(ckv, wukv_ref[...]).astype(kv_ref.dtype)
    is_lat = pl.program_id(0) < n_lat_tiles
    cosf, sinf = cos_ref[...], sin_ref[...]
    first = _first_half_lanes(cosf.shape)
    for c in range(qr_ref.shape[1] // LANES):
        y = q[:, n_nope + c * LANES:n_nope + (c + 1) * LANES]
        qr_ref[:, c * LANES:(c + 1) * LANES] = _rope_chunk(y, cosf, sinf, first, is_lat).astype(qr_ref.dtype)
    k_rope = p1[:, MLA_Q_RANK + MLA_KV_RANK:MLA_Q_RANK + MLA_KV_RANK + LANES]
    kr_ref[...] = _rope_chunk(k_rope, cosf, sinf, first, is_lat).astype(kr_ref.dtype)


def _mla_in_proj(hs, g, mod, w_in, g_q, w_uq, g_kv, w_ukv, n_nope, q_scale, cosf, sinf, n_lat_tiles, tiles_per_batch,
                 nb):
    n_rows, d = hs.shape
    tm = ROW_TILE
    n_q, n_kv = w_uq.shape[1], w_ukv.shape[1]
    const = lambda shape: pl.BlockSpec(shape, lambda i: (0, 0))
    row = lambda w: pl.BlockSpec((tm, w), lambda i: (i, 0))
    return pl.pallas_call(
        functools.partial(_mla_in_proj_kernel, d=d, n_lat_tiles=n_lat_tiles, q_scale=q_scale),
        grid=(n_rows // tm,),
        in_specs=[
            row(d), const((1, d)), _mod_spec(mod, tiles_per_batch, nb),
            pl.BlockSpec((tm, LANES), lambda i: (i % tiles_per_batch, 0)),
            pl.BlockSpec((tm, LANES), lambda i: (i % tiles_per_batch, 0)),
            const(w_in.shape), const((1, MLA_Q_RANK)), const((1, MLA_KV_RANK)), const(w_uq.shape), const(w_ukv.shape),
        ],
        out_specs=[row(n_nope), row(n_q - n_nope), row(n_kv), row(LANES)],
        out_shape=[jax.ShapeDtypeStruct((n_rows, n_nope), BF16), jax.ShapeDtypeStruct((n_rows, n_q - n_nope), BF16),
                   jax.ShapeDtypeStruct((n_rows, n_kv), BF16), jax.ShapeDtypeStruct((n_rows, LANES), BF16)],
        scratch_shapes=[pltpu.VMEM((tm, d), BF16)],
        compiler_params=_cparams(("parallel",)),
        name="mla_in_proj",
    )(hs, g.reshape(1, d), mod, cosf, sinf, w_in, g_q.reshape(1, -1), g_kv.reshape(1, -1), w_uq, w_ukv)


def _proj_resid_kernel(a_ref, a2_ref, w_ref, h_ref, g_ref, mod_ref, o_ref, y_scr, *, gate_idx, d, n1):
    def body(src_ref):
        y_scr[...] = _dot(src_ref[...].astype(BF16), w_ref[...])
        _resid_norm_rows(o_ref, h_ref, y_scr, g_ref, mod_ref, gate_idx, d)

    pl.when(pl.program_id(0) < n1)(lambda: body(a_ref))
    pl.when(pl.program_id(0) >= n1)(lambda: body(a2_ref))


def _proj_resid(a, a2, w, hs, g, mod, gate_idx, n_rows, tiles_per_batch, nb, name):
    k, d = w.shape
    tm = ROW_TILE
    n1 = a.shape[0] // tm
    return pl.pallas_call(
        functools.partial(_proj_resid_kernel, gate_idx=gate_idx, d=d, n1=n1),
        grid=(n_rows // tm,),
        in_specs=[
            pl.BlockSpec((tm, k), lambda i: (jnp.minimum(i, n1 - 1), 0)),
            pl.BlockSpec((tm, k), lambda i: (jnp.maximum(i - n1, 0), 0)),
            pl.BlockSpec((k, d), lambda i: (0, 0)),
            pl.BlockSpec((tm, d), lambda i: (i, 0)),
            pl.BlockSpec((1, d), lambda i: (0, 0)),
            _mod_spec(mod, tiles_per_batch, nb),
        ],
        out_specs=pl.BlockSpec((tm, d), lambda i: (i, 0)),
        out_shape=jax.ShapeDtypeStruct((n_rows, d), F32),
        scratch_shapes=[pltpu.VMEM((tm, d), F32)],
        compiler_params=_cparams(("parallel",)),
        name=name,
    )(a, a2, w, hs, g.reshape(1, d), mod)


def _ffn_kernel(h_ref, g2_ref, g3_ref, mod_ref, wg_ref, wu_ref, wo_ref, o_ref, a_scr, acc_scr, *, d):
    j = pl.program_id(1)

    @pl.when(j == 0)
    def _():
        _adaln_rows(a_scr, h_ref, g2_ref, mod_ref, 3, 4, d)
        acc_scr[...] = jnp.zeros_like(acc_scr)

    a = a_scr[...]
    gt = _dot(a, wg_ref[...])
    up = _dot(a, wu_ref[...])
    act = (gt * _sigmoid(gt) * up).astype(BF16)
    acc_scr[...] += _dot(act, wo_ref[...])

    @pl.when(j == pl.num_programs(1) - 1)
    def _():
        _resid_norm_rows(o_ref, h_ref, acc_scr, g3_ref, mod_ref, 5, d)


def _ffn(hs, g2, g3, mod, w_in, w_out, layer, n_rows, tiles_per_batch, nb, name):
    d = hs.shape[1]
    f = w_out.shape[1]
    tm = ROW_TILE
    tf = 512
    nf = f // tf
    return pl.pallas_call(
        functools.partial(_ffn_kernel, d=d),
        grid=(n_rows // tm, nf),
        in_specs=[
            pl.BlockSpec((tm, d), lambda i, j: (i, 0)),
            pl.BlockSpec((1, d), lambda i, j: (0, 0)),
            pl.BlockSpec((1, d), lambda i, j: (0, 0)),
            _mod_spec(mod, tiles_per_batch, nb),
            pl.BlockSpec((None, d, tf), lambda i, j: (layer, 0, j)),
            pl.BlockSpec((None, d, tf), lambda i, j: (layer, 0, nf + j)),
            pl.BlockSpec((None, tf, d), lambda i, j: (layer, j, 0)),
        ],
        out_specs=pl.BlockSpec((tm, d), lambda i, j: (i, 0)),
        out_shape=jax.ShapeDtypeStruct((n_rows, d), F32),
        scratch_shapes=[pltpu.VMEM((tm, d), BF16), pltpu.VMEM((tm, d), F32)],
        compiler_params=_cparams(("parallel", "arbitrary")),
        name=name,
    )(hs, g2.reshape(1, d), g3.reshape(1, d), mod, w_in, w_in, w_out)


GLA_CHUNK = 128


def _gla_structure(cs, reverse):
    idx = np.arange(cs)
    ip = cs - 1 - idx if reverse else idx
    ii, jj = ip[:, None], ip[None, :]
    levels = []
    s = cs // 2
    while s >= 1:
        levels.append(((ii // (2 * s)) == (jj // (2 * s))) & ((ii & s) != 0) & ((jj & s) == 0))
        s //= 2
    levels.append(ii == jj)
    return np.stack(levels).astype(np.float32), (jj <= ii).astype(np.float32)


def _gla_chunk(q, k, v_b, lg, msk_ref, tri, st_ref, b_ref, *, cs, dk, reverse):
    hi = lg.astype(BF16)
    r1 = lg - hi.astype(F32)
    mid = r1.astype(BF16)
    lo = (r1 - mid.astype(F32)).astype(BF16)
    b = _dot(tri, hi) + _dot(tri, mid) + _dot(tri, lo)
    b_ref[...] = b
    yield
    last = 0 if reverse else cs - 1
    b_last = b_ref[pl.ds(last, 1), :]

    st = st_ref[...]
    q_b = q.astype(BF16)
    k_b = k.astype(BF16)
    o = _dot_nt(q_b * jnp.exp2(b).astype(BF16), st.astype(BF16))
    yield

    row = lax.broadcasted_iota(jnp.int32, (cs, 1), 0)
    ip = (cs - 1 - row) if reverse else row

    def orig(p):
        return cs - 1 - p if reverse else p

    attn = jnp.zeros((cs, cs), F32)
    lvl = 0
    s = cs // 2
    while s >= 1:
        if s == 1:
            ql = q_b * jnp.exp2(lg).astype(BF16)
            kl = k_b
        else:
            if 2 * s >= SUBLANES:
                nblk = cs // (2 * s)
                pieces = []
                for jb in range(nblk):
                    m = nblk - 1 - jb if reverse else jb
                    p = orig(m * 2 * s + s - 1)
                    pieces.append(jnp.broadcast_to(b_ref[pl.ds(p, 1), :], (2 * s, dk)))
                ref = pieces[0] if nblk == 1 else jnp.concatenate(pieces, axis=0)
            else:
                ngrp = cs // SUBLANES
                pa, pb = [], []
                for jg in range(ngrp):
                    gp = ngrp - 1 - jg if reverse else jg
                    pa.append(jnp.broadcast_to(b_ref[pl.ds(orig(gp * SUBLANES + 1), 1), :], (SUBLANES, dk)))
                    pb.append(jnp.broadcast_to(b_ref[pl.ds(orig(gp * SUBLANES + 5), 1), :], (SUBLANES, dk)))
                ref = jnp.where((ip & 4) == 0, jnp.concatenate(pa, axis=0), jnp.concatenate(pb, axis=0))
            fac = jnp.exp2(-jnp.abs(b - ref)).astype(BF16)
            ql = q_b * fac
            kl = k_b * fac
        attn = attn + msk_ref[lvl] * _dot_nt(ql, kl)
        yield
        lvl += 1
        s //= 2
    attn = attn + msk_ref[lvl] * _dot_nt(q_b, k_b)

    k_dec = k_b * jnp.exp2(b_last - b).astype(BF16)
    st_ref[...] = st * jnp.exp2(b_last) + _dot_tn(v_b, k_dec)
    yield
    return o + _dot(attn.astype(BF16), v_b)


def _interleave(gens):
    results = [None] * len(gens)
    active = list(range(len(gens)))
    while active:
        for idx in list(active):
            try:
                next(gens[idx])
            except StopIteration as stop:
                results[idx] = stop.value
                active.remove(idx)
    return results


def _gla_scan_kernel(qf_ref, kf_ref, vf_ref, gf_ref, qb_ref, kb_ref, vb_ref, gb_ref, wu_ref, bg_ref,
                     msk_ref, tri_ref, of_ref, ob_ref, st_scr, b_scr, *, cs, dk, dv):
    @pl.when(pl.program_id(1) == 0)
    def _():
        st_scr[...] = jnp.zeros_like(st_scr)

    dirs = ((qf_ref, kf_ref, vf_ref, gf_ref, of_ref), (qb_ref, kb_ref, vb_ref, gb_ref, ob_ref))
    chains, dests = [], []
    for dr, (q_ref, k_ref, v_ref, gd_ref, o_ref) in enumerate(dirs):
        z = _dot(gd_ref[...].astype(BF16), wu_ref[dr]) + bg_ref[dr]
        lg_all = (jnp.minimum(z, 0.0) - jnp.log1p(jnp.exp(-jnp.abs(z)))) * (LOG2E / GLA_TAU)
        for h in range(GLA_HEADS):
            ks = slice(h * dk, (h + 1) * dk)
            vs = slice(h * dv, (h + 1) * dv)
            chains.append(_gla_chunk(q_ref[:, ks] * (dk ** -0.5), k_ref[:, ks], v_ref[:, vs].astype(BF16),
                                     lg_all[:, ks], msk_ref.at[dr], tri_ref[dr], st_scr.at[dr, h], b_scr.at[dr, h],
                                     cs=cs, dk=dk, reverse=bool(dr)))
            dests.append((o_ref, vs))
    for (o_ref, vs), o in zip(dests, _interleave(chains)):
        o_ref[:, vs] = o


def _gla_scan(p, gd, wu_ext, bg, b_sz, t_lat, t_ctx):
    cs = GLA_CHUNK
    qkw = wu_ext.shape[-1]
    dk = qkw // GLA_HEADS
    vw = 2 * qkw
    dv = vw // GLA_HEADS
    n_tok = p.shape[0]
    ncc, ncl = t_ctx // cs, t_lat // cs
    ctx_blk0 = (b_sz * t_lat) // cs
    structs = [_gla_structure(cs, rev) for rev in (False, True)]
    masks = np.stack([m for m, _ in structs])
    tri = np.stack([t for _, t in structs])

    def rowblk(rev):
        def f(b, s):
            if rev:
                cc, lc = ncc - 1 - s, ncl - 1 - (s - ncc)
            else:
                cc, lc = s, s - ncc
            return jnp.where(s < ncc, ctx_blk0 + b * ncc + cc, b * ncl + lc)
        return f

    def dir_specs(rev):
        rb = rowblk(rev)
        return [pl.BlockSpec((cs, qkw), lambda b, s: (rb(b, s), 0)),
                pl.BlockSpec((cs, qkw), lambda b, s: (rb(b, s), 1)),
                pl.BlockSpec((cs, vw), lambda b, s: (rb(b, s), 1)),
                pl.BlockSpec((cs, LANES), lambda b, s: (rb(b, s), 0))]

    return pl.pallas_call(
        functools.partial(_gla_scan_kernel, cs=cs, dk=dk, dv=dv),
        grid=(b_sz, ncc + ncl),
        in_specs=dir_specs(False) + dir_specs(True) + [
            pl.BlockSpec((2, LANES, qkw), lambda b, s: (0, 0, 0)),
            pl.BlockSpec((2, 1, qkw), lambda b, s: (0, 0, 0)),
            pl.BlockSpec(masks.shape, lambda b, s: (0, 0, 0, 0)),
            pl.BlockSpec(tri.shape, lambda b, s: (0, 0, 0)),
        ],
        out_specs=[pl.BlockSpec((cs, vw), lambda b, s: (rowblk(False)(b, s), 0)),
                   pl.BlockSpec((cs, vw), lambda b, s: (rowblk(True)(b, s), 0))],
        out_shape=[jax.ShapeDtypeStruct((n_tok, vw), F32)] * 2,
        scratch_shapes=[pltpu.VMEM((2, GLA_HEADS, dv, dk), F32), pltpu.VMEM((2, GLA_HEADS, cs, dk), F32)],
        compiler_params=_cparams(("parallel", "arbitrary")),
        name="gla_scan",
    )(p, p, p, gd, p, p, p, gd, wu_ext, bg, jnp.asarray(masks), jnp.asarray(tri, dtype=BF16))


def _gla_combine_kernel(of_ref, ob_ref, r_ref, gh_ref, o_ref, *, dv):
    g = gh_ref[...]
    for r0 in range(0, o_ref.shape[0], NORM_ROWS):
        rows = pl.ds(r0, NORM_ROWS)
        for h in range(GLA_HEADS):
            sl = slice(h * dv, (h + 1) * dv)
            y = _rms(of_ref[rows, sl] + ob_ref[rows, sl]) * g
            r = r_ref[rows, sl]
            o_ref[rows, sl] = (y * (r * _sigmoid(r))).astype(o_ref.dtype)


def _gla_combine(o_f, o_b, r, g_head, n_rows):
    d = o_f.shape[1]
    dv = d // GLA_HEADS
    tm = ROW_TILE
    return pl.pallas_call(
        functools.partial(_gla_combine_kernel, dv=dv),
        grid=(n_rows // tm,),
        in_specs=[
            pl.BlockSpec((tm, d), lambda i: (i, 0)),
            pl.BlockSpec((tm, d), lambda i: (i, 0)),
            pl.BlockSpec((tm, d), lambda i: (i, 0)),
            pl.BlockSpec((1, dv), lambda i: (0, 0)),
        ],
        out_specs=pl.BlockSpec((tm, d), lambda i: (i, 0)),
        out_shape=jax.ShapeDtypeStruct((n_rows, d), BF16),
        compiler_params=_cparams(("parallel",)),
        name="gla_combine",
    )(o_f, o_b, r, g_head.reshape(1, dv))


def _rope_tables(t_lat):
    t = jnp.arange(t_lat)
    row = (t // GRID_W).astype(F32)
    col = (t % GRID_W).astype(F32)
    n_freq = ROPE_DIM // 4
    inv_freq = ROPE_THETA ** (-jnp.arange(n_freq, dtype=F32) / n_freq)
    ang = jnp.concatenate([row[:, None] * inv_freq, col[:, None] * inv_freq], axis=-1)
    cos, sin = jnp.cos(ang), jnp.sin(ang)
    reps = LANES // ROPE_DIM
    return jnp.tile(jnp.concatenate([cos, cos], axis=-1), (1, reps)), jnp.tile(jnp.concatenate([-sin, sin], axis=-1), (1, reps))


MLA_TQ = 512
MLA_TK = 1024
MLA_HEADS_PER_STEP = 2


def _mla_attn_kernel(qn_ref, qr_ref, qnc_ref, qrc_ref, knc_ref, vc_ref, krc_ref, knl_ref, vl_ref, krl_ref,
                     o_ref, oc_ref, kcat, vext, *, tc, tl):
    t = pl.program_id(2)
    nh = MLA_HEADS_PER_STEP
    rd = ROPE_DIM
    vw = MLA_V + LANES

    def qcat(n_ref, r_ref, h):
        return jnp.concatenate([n_ref[:, h * MLA_NOPE:(h + 1) * MLA_NOPE], r_ref[:, h * rd:(h + 1) * rd]], axis=1)

    def finish(acc):
        return acc[:, 0:MLA_V] / acc[:, MLA_V:vw]

    @pl.when(t == 0)
    def _():
        for h in range(nh):
            ns = slice(h * MLA_NOPE, (h + 1) * MLA_NOPE)
            kcat[h, 0:tc, 0:MLA_NOPE] = knc_ref[:, ns]
            kcat[h, 0:tc, MLA_NOPE:MLA_NOPE + rd] = krc_ref[:, 0:rd]
            kcat[h, tc:tc + tl, 0:MLA_NOPE] = knl_ref[:, ns]
            kcat[h, tc:tc + tl, MLA_NOPE:MLA_NOPE + rd] = krl_ref[:, 0:rd]
            vext[0:tc, h * vw:h * vw + MLA_V] = vc_ref[:, h * MLA_V:(h + 1) * MLA_V]
            vext[tc:tc + tl, h * vw:h * vw + MLA_V] = vl_ref[:, h * MLA_V:(h + 1) * MLA_V]
            vext[:, h * vw + MLA_V:(h + 1) * vw] = jnp.ones((tc + tl, LANES), BF16)
        for h in range(nh):
            s = _dot_nt(qcat(qnc_ref, qrc_ref, h), kcat[h, 0:tc, :])
            p = jnp.exp2(s - jnp.max(s, axis=-1, keepdims=True))
            acc = _dot(p.astype(BF16), vext[0:tc, h * vw:(h + 1) * vw])
            oc_ref[:, h * MLA_V:(h + 1) * MLA_V] = finish(acc).astype(oc_ref.dtype)

    for h in range(nh):
        qc = qcat(qn_ref, qr_ref, h)
        s = _dot_nt(qc, kcat[h, 0:tc, :])
        m = jnp.max(s, axis=-1, keepdims=True)
        acc = _dot(jnp.exp2(s - m).astype(BF16), vext[0:tc, h * vw:(h + 1) * vw])
        for c0 in range(tc, tc + tl, MLA_TK):
            s = _dot_nt(qc, kcat[h, c0:c0 + MLA_TK, :])
            m_new = jnp.maximum(m, jnp.max(s, axis=-1, keepdims=True))
            acc = jnp.exp2(m - m_new) * acc + _dot(jnp.exp2(s - m_new).astype(BF16),
                                                   vext[c0:c0 + MLA_TK, h * vw:(h + 1) * vw])
            m = m_new
        o_ref[:, h * MLA_V:(h + 1) * MLA_V] = finish(acc).astype(o_ref.dtype)


def _mla_attention(qn, qr, kv, kr, b_sz, t_lat, t_ctx):
    hd = qn.shape[1]
    nh = MLA_HEADS_PER_STEP
    n_heads = hd // MLA_NOPE
    tq = MLA_TQ
    assert t_lat % tq == 0 and t_lat % MLA_TK == 0
    nlt = t_lat // tq
    ctx_blk0 = (b_sz * t_lat) // t_ctx
    vcol0 = n_heads // nh
    wn, wr = nh * MLA_NOPE, nh * ROPE_DIM
    return pl.pallas_call(
        functools.partial(_mla_attn_kernel, tc=t_ctx, tl=t_lat),
        grid=(b_sz, n_heads // nh, nlt),
        in_specs=[
            pl.BlockSpec((tq, wn), lambda b, h, t: (b * nlt + t, h)),
            pl.BlockSpec((tq, wr), lambda b, h, t: (b * nlt + t, h)),
            pl.BlockSpec((t_ctx, wn), lambda b, h, t: (ctx_blk0 + b, h)),
            pl.BlockSpec((t_ctx, wr), lambda b, h, t: (ctx_blk0 + b, h)),
            pl.BlockSpec((t_ctx, wn), lambda b, h, t: (ctx_blk0 + b, h)),
            pl.BlockSpec((t_ctx, wn), lambda b, h, t: (ctx_blk0 + b, vcol0 + h)),
            pl.BlockSpec((t_ctx, LANES), lambda b, h, t: (ctx_blk0 + b, 0)),
            pl.BlockSpec((t_lat, wn), lambda b, h, t: (b, h)),
            pl.BlockSpec((t_lat, wn), lambda b, h, t: (b, vcol0 + h)),
            pl.BlockSpec((t_lat, LANES), lambda b, h, t: (b, 0)),
        ],
        out_specs=[pl.BlockSpec((tq, nh * MLA_V), lambda b, h, t: (b * nlt + t, h)),
                   pl.BlockSpec((t_ctx, nh * MLA_V), lambda b, h, t: (b, h))],
        out_shape=[jax.ShapeDtypeStruct((b_sz * t_lat, n_heads * MLA_V), BF16),
                   jax.ShapeDtypeStruct((b_sz * t_ctx, n_heads * MLA_V), BF16)],
        scratch_shapes=[pltpu.VMEM((nh, t_ctx + t_lat, MLA_NOPE + ROPE_DIM), BF16),
                        pltpu.VMEM((t_ctx + t_lat, nh * (MLA_V + LANES)), BF16)],
        compiler_params=_cparams(("parallel", "parallel", "arbitrary")),
        name="mla_attention",
    )(qn, qr, qn, qr, kv, kv, kr, kv, kv, kr)


def _swa_attn_kernel(sink_ref, q_ref, kc_ref, vc_ref, kp_ref, kc0_ref, kn_ref, vp_ref, vc0_ref, vn_ref, o_ref,
                     *, nct, nb, n_kv):
    t = pl.program_id(1)
    n = t - nct
    blk = WINDOW
    hd = SWA_HEAD_DIM
    tc = kc_ref.shape[0]
    nk = tc + 3 * blk
    npair = SWA_GROUP // 2

    qi = lax.broadcasted_iota(jnp.int32, (blk, 3 * blk), 0)
    kj = lax.broadcasted_iota(jnp.int32, (blk, 3 * blk), 1)
    lo = jnp.where(n > 0, 0, blk)
    hi = jnp.where(t < nct, 0, jnp.where(n < nb - 1, 3 * blk, 2 * blk))
    win_ok = (jnp.abs(qi - kj + blk) <= WINDOW) & (kj >= lo) & (kj < hi)
    bias = jnp.concatenate([jnp.zeros((blk, tc), F32), jnp.where(win_ok, 0.0, NEG_BIG)], axis=1)
    bias = jnp.concatenate([bias] * npair, axis=0)
    lane = lax.broadcasted_iota(jnp.int32, (blk, 2 * hd), 1)
    zeros = jnp.zeros((nk, hd), BF16)
    ones = jnp.ones((nk, hd), BF16)

    for kvh in range(n_kv):
        ks = slice(kvh * hd, (kvh + 1) * hd)
        k_all = jnp.concatenate([kc_ref[:, ks], kp_ref[:, ks], kc0_ref[:, ks], kn_ref[:, ks]], axis=0)
        v_all = jnp.concatenate([vc_ref[:, ks], vp_ref[:, ks], vc0_ref[:, ks], vn_ref[:, ks]], axis=0)
        k_bd = jnp.concatenate([jnp.concatenate([k_all, zeros], axis=1),
                                jnp.concatenate([zeros, k_all], axis=1)], axis=0)
        h0 = kvh * SWA_GROUP
        q_st = jnp.concatenate([q_ref[:, (h0 + 2 * j) * hd:(h0 + 2 * j + 2) * hd] for j in range(npair)], axis=0)
        s = _dot_nt(q_st, k_bd)
        ps, ms = [], []
        for e in range(2):
            sink_col = jnp.concatenate([jnp.full((blk, 1), sink_ref[h0 + 2 * j + e] * LOG2E, F32)
                                        for j in range(npair)], axis=0)
            s_e = s[:, e * nk:(e + 1) * nk] + bias
            m = jnp.maximum(jnp.max(s_e, axis=-1, keepdims=True), sink_col)
            p = jnp.exp2(s_e - m).astype(BF16)
            ps += [p[:, 0:tc], p[:, tc:]]
            ms.append(sink_col - m)

        def v_ext(r0, r1, e):
            v, z, o = v_all[r0:r1], zeros[r0:r1], ones[r0:r1]
            return jnp.concatenate([z, v, z, o] if e else [v, z, o, z], axis=1)

        tail = tc + 2 * blk
        acc = _dot(jnp.concatenate([ps[1][:, 2 * blk:], ps[3][:, 2 * blk:]], axis=1),
                   jnp.concatenate([v_ext(tail, nk, 0), v_ext(tail, nk, 1)], axis=0))
        for e in range(2):
            acc = acc + _dot(ps[2 * e], v_ext(0, tc, e)) + _dot(ps[2 * e + 1][:, 0:2 * blk], v_ext(tc, tail, e))
        for j in range(npair):
            r = slice(j * blk, (j + 1) * blk)
            sink_term = jnp.exp2(jnp.where(lane < hd, ms[0][r], ms[1][r]))
            o = acc[r, 0:2 * hd] / (acc[r, 2 * hd:4 * hd] + sink_term)
            o_ref[:, (h0 + 2 * j) * hd:(h0 + 2 * j + 2) * hd] = o.astype(o_ref.dtype)


def _swa_attention(qkv, sinks, b_sz, t_lat, t_ctx):
    n_tok = qkv.shape[0]
    blk = WINDOW
    hd = SWA_HEAD_DIM
    n_q = sinks.shape[0]
    n_kv = n_q // SWA_GROUP
    kvw = n_kv * hd
    kcol = (n_q * hd) // kvw
    vcol = kcol + 1
    nct, nb = t_ctx // blk, t_lat // blk
    ctx_blk0 = (b_sz * t_lat) // blk
    ctx_row0 = (b_sz * t_lat) // t_ctx

    def qrow(b, t):
        return jnp.where(t < nct, ctx_blk0 + b * nct + t, b * nb + t - nct)

    def krow(b, t, off):
        return b * nb + jnp.clip(t - nct + off, 0, nb - 1)

    kspec = lambda off: pl.BlockSpec((blk, kvw), lambda b, t: (krow(b, t, off), kcol))
    vspec = lambda off: pl.BlockSpec((blk, kvw), lambda b, t: (krow(b, t, off), vcol))
    return pl.pallas_call(
        functools.partial(_swa_attn_kernel, nct=nct, nb=nb, n_kv=n_kv),
        grid=(b_sz, nct + nb),
        in_specs=[
            pl.BlockSpec(memory_space=pltpu.SMEM),
            pl.BlockSpec((blk, n_q * hd), lambda b, t: (qrow(b, t), 0)),
            pl.BlockSpec((t_ctx, kvw), lambda b, t: (ctx_row0 + b, kcol)),
            pl.BlockSpec((t_ctx, kvw), lambda b, t: (ctx_row0 + b, vcol)),
            kspec(-1), kspec(0), kspec(1), vspec(-1), vspec(0), vspec(1),
        ],
        out_specs=pl.BlockSpec((blk, n_q * hd), lambda b, t: (qrow(b, t), 0)),
        out_shape=jax.ShapeDtypeStruct((n_tok, n_q * hd), BF16),
        compiler_params=_cparams(("parallel", "arbitrary")),
        name="swa_attention",
    )(sinks, qkv, qkv, qkv, qkv, qkv, qkv, qkv, qkv, qkv)


def kernel(x, c, ctx, c_ctx, w_ada, b_ada, g_norm, w_ffn_in, w_ffn_out, gla_w_in, gla_w_gate_down, gla_w_gate_up, gla_b_gate, gla_g_head, gla_w_out, mla_w_in, mla_g_q, mla_w_uq, mla_g_kv, mla_w_ukv, mla_w_out, swa_w_in, swa_sinks, swa_w_out):
    b_sz, t_lat, d = x.shape
    t_ctx = ctx.shape[1]
    depth = w_ada.shape[0]
    n_lat, n_ctx = b_sz * t_lat, b_sz * t_ctx
    n_tok = n_lat + n_ctx
    tm = ROW_TILE
    assert t_lat % tm == 0 and n_ctx % tm == 0 and b_sz < MOD_ROWS
    tpb = t_lat // tm
    n_lat_tiles = n_lat // tm

    hs = jnp.concatenate([x.reshape(n_lat, d), ctx.reshape(n_ctx, d)], axis=0)
    cond = jnp.zeros((MOD_ROWS, d), F32).at[:b_sz].set(c).at[b_sz].set(c_ctx)
    mod_all = _modulation(cond, w_ada, b_ada)
    cosf, sinf = _rope_tables(t_lat)
    common = dict(tiles_per_batch=tpb, nb=b_sz)
    w_ffn_in_b, w_ffn_out_b = w_ffn_in.astype(BF16), w_ffn_out.astype(BF16)

    for i in range(depth):
        kind, j = i % 3, i // 3
        last = i == depth - 1
        n_out = n_lat if last else n_tok
        mod = mod_all[i].reshape(MOD_ROWS, 1, 6 * d)

        if kind == 0:
            rank = GLA_GATE_RANK
            qk_w = gla_w_gate_up.shape[-1]
            pad = LANES - 2 * rank
            w_gd = jnp.concatenate([gla_w_gate_down[j, 0], gla_w_gate_down[j, 1], jnp.zeros((d, pad), F32)],
                                   axis=1).astype(BF16)
            big = 2 * tm
            tm_in = big if t_lat % big == 0 and n_ctx % big == 0 else tm
            qkv, r, gd = _gla_in_proj(hs, g_norm[i, 0], mod, gla_w_in[j].astype(BF16), w_gd, 2 * qk_w + d, tm_in,
                                      t_lat // tm_in, b_sz)
            wu_ext = jnp.zeros((2, LANES, qk_w), F32)
            for dr in range(2):
                wu_ext = wu_ext.at[dr, dr * rank:(dr + 1) * rank].set(gla_w_gate_up[j, dr])
            o_f, o_b = _gla_scan(qkv, gd, wu_ext.astype(BF16), gla_b_gate[j].reshape(2, 1, qk_w), b_sz, t_lat, t_ctx)
            a = a2 = _gla_combine(o_f, o_b, r, gla_g_head[j], n_out)
            w_o = gla_w_out[j]
        elif kind == 1:
            n_heads = mla_w_out.shape[1] // MLA_V
            w_in = jnp.concatenate([mla_w_in[j], jnp.zeros((d, LANES - ROPE_DIM), F32)], axis=1).astype(BF16)
            w_uq = mla_w_uq[j].reshape(MLA_Q_RANK, n_heads, MLA_NOPE + ROPE_DIM)
            w_uq = jnp.concatenate([w_uq[:, :, :MLA_NOPE].reshape(MLA_Q_RANK, -1),
                                    w_uq[:, :, MLA_NOPE:].reshape(MLA_Q_RANK, -1)], axis=1).astype(BF16)
            w_ukv = mla_w_ukv[j].reshape(MLA_KV_RANK, n_heads, MLA_NOPE + MLA_V)
            w_ukv = jnp.concatenate([w_ukv[:, :, :MLA_NOPE].reshape(MLA_KV_RANK, -1),
                                     w_ukv[:, :, MLA_NOPE:].reshape(MLA_KV_RANK, -1)], axis=1).astype(BF16)
            q_scale = (MLA_NOPE + ROPE_DIM) ** -0.5 * LOG2E
            qn, qr, kv, kr = _mla_in_proj(hs, g_norm[i, 0], mod, w_in, mla_g_q[j], w_uq, mla_g_kv[j], w_ukv,
                                          n_heads * MLA_NOPE, q_scale, cosf, sinf, n_lat_tiles, **common)
            a, a2 = _mla_attention(qn, qr, kv, kr, b_sz, t_lat, t_ctx)
            w_o = mla_w_out[j]
        else:
            n_q = swa_sinks.shape[1]
            qkw = (n_q + n_q // SWA_GROUP) * SWA_HEAD_DIM
            qkv = _swa_in_proj(hs, g_norm[i, 0], mod, swa_w_in[j].astype(BF16), cosf, sinf, qkw // LANES,
                               (n_q * SWA_HEAD_DIM) // LANES, SWA_HEAD_DIM ** -0.5 * LOG2E, n_lat_tiles, **common)
            a = a2 = _swa_attention(qkv, swa_sinks[j], b_sz, t_lat, t_ctx)
            w_o = swa_w_out[j]

        hs = _proj_resid(a, a2, w_o.astype(BF16), hs, g_norm[i, 1], mod, 2, n_out, name="mixer_out_proj", **common)
        hs = _ffn(hs, g_norm[i, 2], g_norm[i, 3], mod, w_ffn_in_b, w_ffn_out_b, i, n_out, name="ffn", **common)

    return hs.reshape(b_sz, t_lat, d)
```

```python
import functools

import numpy as np
import jax
import jax.numpy as jnp
from jax import lax
from jax.experimental import pallas as pl
from jax.experimental.pallas import tpu as pltpu

F32 = jnp.float32
BF16 = jnp.bfloat16

EPS = 1e-6
ROPE_THETA = 10000.0
ROPE_DIM = 64
GRID_W = 64
WINDOW = 128
GLA_HEADS = 4
GLA_GATE_RANK = 16
GLA_TAU = 16.0
MLA_Q_RANK = 512
MLA_KV_RANK = 512
MLA_NOPE = 128
MLA_V = 128
SWA_HEAD_DIM = 64
SWA_GROUP = 8

LANES = 128
SUBLANES = 8
VMEM_LIMIT = 56 * 1024 * 1024
ROW_TILE = 512
MOD_ROWS = 16
NEG_BIG = -1e30
LOG2E = 1.4426950408889634


def _cparams(sem):
    return pltpu.CompilerParams(dimension_semantics=sem, vmem_limit_bytes=VMEM_LIMIT)


def _dot(a, b):
    return jnp.dot(a, b, preferred_element_type=F32)


def _dot_nt(a, b):
    return lax.dot_general(a, b, (((1,), (1,)), ((), ())), preferred_element_type=F32)


def _dot_tn(a, b):
    return lax.dot_general(a, b, (((0,), (0,)), ((), ())), preferred_element_type=F32)


def _sigmoid(x):
    return 1.0 / (1.0 + jnp.exp(-x))


def _rms(x):
    return x * lax.rsqrt(jnp.mean(x * x, axis=-1, keepdims=True) + EPS)


def _mod_kernel(c_ref, w_ref, b_ref, o_ref):
    c = c_ref[...]
    s = (c * _sigmoid(c)).astype(BF16)
    o_ref[0] = _dot(s, w_ref[0].astype(BF16)) + b_ref[0]


def _modulation(cond, w_ada, b_ada):
    depth, d, n = w_ada.shape
    tn = 1024
    return pl.pallas_call(
        _mod_kernel,
        grid=(depth, n // tn),
        in_specs=[
            pl.BlockSpec((MOD_ROWS, d), lambda l, j: (0, 0)),
            pl.BlockSpec((1, d, tn), lambda l, j: (l, 0, j)),
            pl.BlockSpec((1, 1, tn), lambda l, j: (l, 0, j)),
        ],
        out_specs=pl.BlockSpec((1, MOD_ROWS, tn), lambda l, j: (l, 0, j)),
        out_shape=jax.ShapeDtypeStruct((depth, MOD_ROWS, n), F32),
        compiler_params=_cparams(("parallel", "parallel")),
        name="modulation",
    )(cond, w_ada, b_ada.reshape(depth, 1, n))


NORM_ROWS = 16


def _adaln_rows(dst_ref, x_ref, g_ref, mod_ref, shift_idx, scale_idx, d):
    m = mod_ref[0]
    gs = g_ref[...] * (1.0 + m[:, scale_idx * d:(scale_idx + 1) * d])
    sh = m[:, shift_idx * d:(shift_idx + 1) * d]
    for r in range(0, x_ref.shape[0], NORM_ROWS):
        rows = pl.ds(r, NORM_ROWS)
        dst_ref[rows, :] = (_rms(x_ref[rows, :]) * gs + sh).astype(dst_ref.dtype)


def _resid_norm_rows(o_ref, h_ref, y_ref, g_ref, mod_ref, gate_idx, d):
    gg = mod_ref[0][:, gate_idx * d:(gate_idx + 1) * d] * g_ref[...]
    for r in range(0, h_ref.shape[0], NORM_ROWS):
        rows = pl.ds(r, NORM_ROWS)
        o_ref[rows, :] = h_ref[rows, :] + _rms(y_ref[rows, :]) * gg


def _rope_chunk(y, cosf, sinf, first, is_lat):
    partner = jnp.where(first, pltpu.roll(y, LANES - ROPE_DIM // 2, 1), pltpu.roll(y, ROPE_DIM // 2, 1))
    return jnp.where(is_lat, y * cosf + partner * sinf, y)


def _first_half_lanes(shape):
    lane = lax.broadcasted_iota(jnp.int32, shape, 1)
    return (lane % ROPE_DIM) < (ROPE_DIM // 2)


def _mod_spec(mod, tiles_per_batch, nb):
    return pl.BlockSpec((1, 1, mod.shape[-1]), lambda i, *_: (jnp.minimum(i // tiles_per_batch, nb), 0, 0))


def _gla_in_proj_kernel(x_ref, g_ref, mod_ref, w_ref, wgd_ref, qkv_ref, r_ref, gd_ref, a_scr, *, d, n_qkv_tiles):
    j = pl.program_id(1)

    @pl.when(j == 0)
    def _():
        _adaln_rows(a_scr, x_ref, g_ref, mod_ref, 0, 1, d)
        gd_ref[...] = _dot(a_scr[...], wgd_ref[...])

    @pl.when(j < n_qkv_tiles)
    def _():
        qkv_ref[...] = _dot(a_scr[...], w_ref[...]).astype(qkv_ref.dtype)

    @pl.when(j >= n_qkv_tiles)
    def _():
        r_ref[...] = _dot(a_scr[...], w_ref[...])


def _gla_in_proj(hs, g, mod, w, w_gd, n_qkv, tm, tiles_per_batch, nb):
    n_rows, d = hs.shape
    n = w.shape[1]
    tn = 1024
    nq = n_qkv // tn
    n_side = w_gd.shape[1]
    return pl.pallas_call(
        functools.partial(_gla_in_proj_kernel, d=d, n_qkv_tiles=nq),
        grid=(n_rows // tm, n // tn),
        in_specs=[
            pl.BlockSpec((tm, d), lambda i, j: (i, 0)),
            pl.BlockSpec((1, d), lambda i, j: (0, 0)),
            _mod_spec(mod, tiles_per_batch, nb),
            pl.BlockSpec((d, tn), lambda i, j: (0, j)),
            pl.BlockSpec((d, n_side), lambda i, j: (0, 0)),
        ],
        out_specs=[pl.BlockSpec((tm, tn), lambda i, j: (i, jnp.minimum(j, nq - 1))),
                   pl.BlockSpec((tm, tn), lambda i, j: (i, jnp.maximum(j - nq, 0))),
                   pl.BlockSpec((tm, n_side), lambda i, j: (i, 0))],
        out_shape=[jax.ShapeDtypeStruct((n_rows, n_qkv), BF16),
                   jax.ShapeDtypeStruct((n_rows, n - n_qkv), F32),
                   jax.ShapeDtypeStruct((n_rows, n_side), F32)],
        scratch_shapes=[pltpu.VMEM((tm, d), BF16)],
        compiler_params=_cparams(("parallel", "arbitrary")),
        name="gla_in_proj",
    )(hs, g.reshape(1, d), mod, w, w_gd)


def _swa_in_proj_kernel(x_ref, g_ref, mod_ref, cos_ref, sin_ref, w_ref, o_ref, a_scr, *, d, n_lat_tiles, n_roped,
                        n_scaled, out_scale):
    _adaln_rows(a_scr, x_ref, g_ref, mod_ref, 0, 1, d)
    y_all = _dot(a_scr[...], w_ref[...])
    is_lat = pl.program_id(0) < n_lat_tiles
    cosf, sinf = cos_ref[...], sin_ref[...]
    first = _first_half_lanes(cosf.shape)
    for c in range(y_all.shape[1] // LANES):
        y = y_all[:, c * LANES:(c + 1) * LANES]
        if c < n_roped:
            y = _rope_chunk(y, cosf, sinf, first, is_lat)
        if c < n_scaled:
            y = y * out_scale
        o_ref[:, c * LANES:(c + 1) * LANES] = y.astype(o_ref.dtype)


def _swa_in_proj(hs, g, mod, w, cosf, sinf, n_roped, n_scaled, out_scale, n_lat_tiles, tiles_per_batch, nb):
    n_rows, d = hs.shape
    n = w.shape[1]
    tm = ROW_TILE
    return pl.pallas_call(
        functools.partial(_swa_in_proj_kernel, d=d, n_lat_tiles=n_lat_tiles, n_roped=n_roped, n_scaled=n_scaled,
                          out_scale=out_scale),
        grid=(n_rows // tm,),
        in_specs=[
            pl.BlockSpec((tm, d), lambda i: (i, 0)),
            pl.BlockSpec((1, d), lambda i: (0, 0)),
            _mod_spec(mod, tiles_per_batch, nb),
            pl.BlockSpec((tm, LANES), lambda i: (i % tiles_per_batch, 0)),
            pl.BlockSpec((tm, LANES), lambda i: (i % tiles_per_batch, 0)),
            pl.BlockSpec((d, n), lambda i: (0, 0)),
        ],
        out_specs=pl.BlockSpec((tm, n), lambda i: (i, 0)),
        out_shape=jax.ShapeDtypeStruct((n_rows, n), BF16),
        scratch_shapes=[pltpu.VMEM((tm, d), BF16)],
        compiler_params=_cparams(("parallel",)),
        name="swa_in_proj",
    )(hs, g.reshape(1, d), mod, cosf, sinf, w)


def _mla_in_proj_kernel(x_ref, g_ref, mod_ref, cos_ref, sin_ref, win_ref, gq_ref, gkv_ref, wuq_ref, wukv_ref,
                        qn_ref, qr_ref, kv_ref, kr_ref, a_scr, *, d, n_lat_tiles, q_scale):
    _adaln_rows(a_scr, x_ref, g_ref, mod_ref, 0, 1, d)
    p1 = _dot(a_scr[...], win_ref[...])
    cq = (_rms(p1[:, 0:MLA_Q_RANK]) * gq_ref[...]).astype(BF16)
    ckv = (_rms(p1[:, MLA_Q_RANK:MLA_Q_RANK + MLA_KV_RANK]) * gkv_ref[...]).astype(BF16)
    q = _dot(cq, wuq_ref[...]) * q_scale
    n_nope = qn_ref.shape[1]
    qn_ref[...] = q[:, 0:n_nope].astype(qn_ref.dtype)
    kv_ref[...] = _dot(ckv, wukv_ref[...]).astype(kv_ref.dtype)
    is_lat = pl.program_id(0) < n_lat_tiles
    cosf, sinf = cos_ref[...], sin_ref[...]
    first = _first_half_lanes(cosf.shape)
    for c in range(qr_ref.shape[1] // LANES):
        y = q[:, n_nope + c * LANES:n_nope + (c + 1) * LANES]
        qr_ref[:, c * LANES:(c + 1) * LANES] = _rope_chunk(y, cosf, sinf, first, is_lat).astype(qr_ref.dtype)
    k_rope = p1[:, MLA_Q_RANK + MLA_KV_RANK:MLA_Q_RANK + MLA_KV_RANK + LANES]
    kr_ref[...] = _rope_chunk(k_rope, cosf, sinf, first, is_lat).astype(kr_ref.dtype)


def _mla_in_proj(hs, g, mod, w_in, g_q, w_uq, g_kv, w_ukv, n_nope, q_scale, cosf, sinf, n_lat_tiles, tiles_per_batch,
                 nb):
    n_rows, d = hs.shape
    tm = ROW_TILE
    n_q, n_kv = w_uq.shape[1], w_ukv.shape[1]
    const = lambda shape: pl.BlockSpec(shape, lambda i: (0, 0))
    row = lambda w: pl.BlockSpec((tm, w), lambda i: (i, 0))
    return pl.pallas_call(
        functools.partial(_mla_in_proj_kernel, d=d, n_lat_tiles=n_lat_tiles, q_scale=q_scale),
        grid=(n_rows // tm,),
        in_specs=[
            row(d), const((1, d)), _mod_spec(mod, tiles_per_batch, nb),
            pl.BlockSpec((tm, LANES), lambda i: (i % tiles_per_batch, 0)),
            pl.BlockSpec((tm, LANES), lambda i: (i % tiles_per_batch, 0)),
            const(w_in.shape), const((1, MLA_Q_RANK)), const((1, MLA_KV_RANK)), const(w_uq.shape), const(w_ukv.shape),
        ],
        out_specs=[row(n_nope), row(n_q - n_nope), row(n_kv), row(LANES)],
        out_shape=[jax.ShapeDtypeStruct((n_rows, n_nope), BF16), jax.ShapeDtypeStruct((n_rows, n_q - n_nope), BF16),
                   jax.ShapeDtypeStruct((n_rows, n_kv), BF16), jax.ShapeDtypeStruct((n_rows, LANES), BF16)],
        scratch_shapes=[pltpu.VMEM((tm, d), BF16)],
        compiler_params=_cparams(("parallel",)),
        name="mla_in_proj",
    )(hs, g.reshape(1, d), mod, cosf, sinf, w_in, g_q.reshape(1, -1), g_kv.reshape(1, -1), w_uq, w_ukv)


def _proj_resid_kernel(a_ref, a2_ref, w_ref, h_ref, g_ref, mod_ref, o_ref, y_scr, *, gate_idx, d, n1):
    def body(src_ref):
        y_scr[...] = _dot(src_ref[...].astype(BF16), w_ref[...])
        _resid_norm_rows(o_ref, h_ref, y_scr, g_ref, mod_ref, gate_idx, d)

    pl.when(pl.program_id(0) < n1)(lambda: body(a_ref))
    pl.when(pl.program_id(0) >= n1)(lambda: body(a2_ref))


def _proj_resid(a, a2, w, hs, g, mod, gate_idx, n_rows, tiles_per_batch, nb, name):
    k, d = w.shape
    tm = ROW_TILE
    n1 = a.shape[0] // tm
    return pl.pallas_call(
        functools.partial(_proj_resid_kernel, gate_idx=gate_idx, d=d, n1=n1),
        grid=(n_rows // tm,),
        in_specs=[
            pl.BlockSpec((tm, k), lambda i: (jnp.minimum(i, n1 - 1), 0)),
            pl.BlockSpec((tm, k), lambda i: (jnp.maximum(i - n1, 0), 0)),
            pl.BlockSpec((k, d), lambda i: (0, 0)),
            pl.BlockSpec((tm, d), lambda i: (i, 0)),
            pl.BlockSpec((1, d), lambda i: (0, 0)),
            _mod_spec(mod, tiles_per_batch, nb),
        ],
        out_specs=pl.BlockSpec((tm, d), lambda i: (i, 0)),
        out_shape=jax.ShapeDtypeStruct((n_rows, d), F32),
        scratch_shapes=[pltpu.VMEM((tm, d), F32)],
        compiler_params=_cparams(("parallel",)),
        name=name,
    )(a, a2, w, hs, g.reshape(1, d), mod)


def _ffn_kernel(h_ref, g2_ref, g3_ref, mod_ref, wg_ref, wu_ref, wo_ref, o_ref, a_scr, *, d):
    j = pl.program_id(1)

    @pl.when(j == 0)
    def _():
        _adaln_rows(a_scr, h_ref, g2_ref, mod_ref, 3, 4, d)
        o_ref[...] = jnp.zeros_like(o_ref)

    a = a_scr[...]
    gt = _dot(a, wg_ref[...])
    up = _dot(a, wu_ref[...])
    act = (gt * _sigmoid(gt) * up).astype(BF16)
    o_ref[...] += _dot(act, wo_ref[...])

    @pl.when(j == pl.num_programs(1) - 1)
    def _():
        _resid_norm_rows(o_ref, h_ref, o_ref, g3_ref, mod_ref, 5, d)


def _ffn(hs, g2, g3, mod, w_in, w_out, layer, n_rows, tm, tiles_per_batch, nb, name):
    d = hs.shape[1]
    f = w_out.shape[1]
    tf = 512
    nf = f // tf
    return pl.pallas_call(
        functools.partial(_ffn_kernel, d=d),
        grid=(n_rows // tm, nf),
        in_specs=[
            pl.BlockSpec((tm, d), lambda i, j: (i, 0)),
            pl.BlockSpec((1, d), lambda i, j: (0, 0)),
            pl.BlockSpec((1, d), lambda i, j: (0, 0)),
            _mod_spec(mod, tiles_per_batch, nb),
            pl.BlockSpec((None, d, tf), lambda i, j: (layer, 0, j)),
            pl.BlockSpec((None, d, tf), lambda i, j: (layer, 0, nf + j)),
            pl.BlockSpec((None, tf, d), lambda i, j: (layer, j, 0)),
        ],
        out_specs=pl.BlockSpec((tm, d), lambda i, j: (i, 0)),
        out_shape=jax.ShapeDtypeStruct((n_rows, d), F32),
        scratch_shapes=[pltpu.VMEM((tm, d), BF16)],
        compiler_params=_cparams(("parallel", "arbitrary")),
        name=name,
    )(hs, g2.reshape(1, d), g3.reshape(1, d), mod, w_in, w_in, w_out)


GLA_CHUNK = 128


def _gla_structure(cs, reverse):
    idx = np.arange(cs)
    ip = cs - 1 - idx if reverse else idx
    ii, jj = ip[:, None], ip[None, :]
    levels = []
    s = cs // 2
    while s >= 1:
        levels.append(((ii // (2 * s)) == (jj // (2 * s))) & ((ii & s) != 0) & ((jj & s) == 0))
        s //= 2
    levels.append(ii == jj)
    return np.stack(levels).astype(np.float32), (jj <= ii).astype(np.float32)


def _gla_chunk(q, k, v_b, lg, msk_ref, tri, st_ref, b_ref, *, cs, dk, reverse):
    hi = lg.astype(BF16)
    r1 = lg - hi.astype(F32)
    mid = r1.astype(BF16)
    lo = (r1 - mid.astype(F32)).astype(BF16)
    b = _dot(tri, hi) + _dot(tri, mid) + _dot(tri, lo)
    b_ref[...] = b
    yield
    last = 0 if reverse else cs - 1
    b_last = b_ref[pl.ds(last, 1), :]

    st = st_ref[...]
    q_b = q.astype(BF16)
    k_b = k.astype(BF16)
    o = _dot_nt(q_b * jnp.exp2(b).astype(BF16), st.astype(BF16))
    yield

    row = lax.broadcasted_iota(jnp.int32, (cs, 1), 0)
    ip = (cs - 1 - row) if reverse else row

    def orig(p):
        return cs - 1 - p if reverse else p

    attn = jnp.zeros((cs, cs), F32)
    lvl = 0
    s = cs // 2
    while s >= 1:
        if s == 1:
            ql = q_b * jnp.exp2(lg).astype(BF16)
            kl = k_b
        else:
            if 2 * s >= SUBLANES:
                nblk = cs // (2 * s)
                pieces = []
                for jb in range(nblk):
                    m = nblk - 1 - jb if reverse else jb
                    p = orig(m * 2 * s + s - 1)
                    pieces.append(jnp.broadcast_to(b_ref[pl.ds(p, 1), :], (2 * s, dk)))
                ref = pieces[0] if nblk == 1 else jnp.concatenate(pieces, axis=0)
            else:
                ngrp = cs // SUBLANES
                pa, pb = [], []
                for jg in range(ngrp):
                    gp = ngrp - 1 - jg if reverse else jg
                    pa.append(jnp.broadcast_to(b_ref[pl.ds(orig(gp * SUBLANES + 1), 1), :], (SUBLANES, dk)))
                    pb.append(jnp.broadcast_to(b_ref[pl.ds(orig(gp * SUBLANES + 5), 1), :], (SUBLANES, dk)))
                ref = jnp.where((ip & 4) == 0, jnp.concatenate(pa, axis=0), jnp.concatenate(pb, axis=0))
            fac = jnp.exp2(-jnp.abs(b - ref)).astype(BF16)
            ql = q_b * fac
            kl = k_b * fac
        attn = attn + msk_ref[lvl] * _dot_nt(ql, kl)
        yield
        lvl += 1
        s //= 2
    attn = attn + msk_ref[lvl] * _dot_nt(q_b, k_b)

    k_dec = k_b * jnp.exp2(b_last - b).astype(BF16)
    st_ref[...] = st * jnp.exp2(b_last) + _dot_tn(v_b, k_dec)
    yield
    return o + _dot(attn.astype(BF16), v_b)


def _interleave(gens):
    results = [None] * len(gens)
    active = list(range(len(gens)))
    while active:
        for idx in list(active):
            try:
                next(gens[idx])
            except StopIteration as stop:
                results[idx] = stop.value
                active.remove(idx)
    return results


def _gla_scan_kernel(qf_ref, kf_ref, vf_ref, gf_ref, qb_ref, kb_ref, vb_ref, gb_ref, wu_ref, bg_ref,
                     msk_ref, tri_ref, of_ref, ob_ref, st_scr, b_scr, *, cs, dk, dv):
    @pl.when(pl.program_id(1) == 0)
    def _():
        st_scr[...] = jnp.zeros_like(st_scr)

    dirs = ((qf_ref, kf_ref, vf_ref, gf_ref, of_ref), (qb_ref, kb_ref, vb_ref, gb_ref, ob_ref))
    chains, dests = [], []
    for dr, (q_ref, k_ref, v_ref, gd_ref, o_ref) in enumerate(dirs):
        z = _dot(gd_ref[...].astype(BF16), wu_ref[dr]) + bg_ref[dr]
        lg_all = (jnp.minimum(z, 0.0) - jnp.log1p(jnp.exp(-jnp.abs(z)))) * (LOG2E / GLA_TAU)
        for h in range(GLA_HEADS):
            ks = slice(h * dk, (h + 1) * dk)
            vs = slice(h * dv, (h + 1) * dv)
            chains.append(_gla_chunk(q_ref[:, ks] * (dk ** -0.5), k_ref[:, ks], v_ref[:, vs].astype(BF16),
                                     lg_all[:, ks], msk_ref.at[dr], tri_ref[dr], st_scr.at[dr, h], b_scr.at[dr, h],
                                     cs=cs, dk=dk, reverse=bool(dr)))
            dests.append((o_ref, vs))
    for (o_ref, vs), o in zip(dests, _interleave(chains)):
        o_ref[:, vs] = o


def _gla_scan(p, gd, wu_ext, bg, b_sz, t_lat, t_ctx):
    cs = GLA_CHUNK
    qkw = wu_ext.shape[-1]
    dk = qkw // GLA_HEADS
    vw = 2 * qkw
    dv = vw // GLA_HEADS
    n_tok = p.shape[0]
    ncc, ncl = t_ctx // cs, t_lat // cs
    ctx_blk0 = (b_sz * t_lat) // cs
    structs = [_gla_structure(cs, rev) for rev in (False, True)]
    masks = np.stack([m for m, _ in structs])
    tri = np.stack([t for _, t in structs])

    def rowblk(rev):
        def f(b, s):
            if rev:
                cc, lc = ncc - 1 - s, ncl - 1 - (s - ncc)
            else:
                cc, lc = s, s - ncc
            return jnp.where(s < ncc, ctx_blk0 + b * ncc + cc, b * ncl + lc)
        return f

    def dir_specs(rev):
        rb = rowblk(rev)
        return [pl.BlockSpec((cs, qkw), lambda b, s: (rb(b, s), 0)),
                pl.BlockSpec((cs, qkw), lambda b, s: (rb(b, s), 1)),
                pl.BlockSpec((cs, vw), lambda b, s: (rb(b, s), 1)),
                pl.BlockSpec((cs, LANES), lambda b, s: (rb(b, s), 0))]

    return pl.pallas_call(
        functools.partial(_gla_scan_kernel, cs=cs, dk=dk, dv=dv),
        grid=(b_sz, ncc + ncl),
        in_specs=dir_specs(False) + dir_specs(True) + [
            pl.BlockSpec((2, LANES, qkw), lambda b, s: (0, 0, 0)),
            pl.BlockSpec((2, 1, qkw), lambda b, s: (0, 0, 0)),
            pl.BlockSpec(masks.shape, lambda b, s: (0, 0, 0, 0)),
            pl.BlockSpec(tri.shape, lambda b, s: (0, 0, 0)),
        ],
        out_specs=[pl.BlockSpec((cs, vw), lambda b, s: (rowblk(False)(b, s), 0)),
                   pl.BlockSpec((cs, vw), lambda b, s: (rowblk(True)(b, s), 0))],
        out_shape=[jax.ShapeDtypeStruct((n_tok, vw), F32)] * 2,
        scratch_shapes=[pltpu.VMEM((2, GLA_HEADS, dv, dk), F32), pltpu.VMEM((2, GLA_HEADS, cs, dk), F32)],
        compiler_params=_cparams(("parallel", "arbitrary")),
        name="gla_scan",
    )(p, p, p, gd, p, p, p, gd, wu_ext, bg, jnp.asarray(masks), jnp.asarray(tri, dtype=BF16))


def _gla_combine_kernel(of_ref, ob_ref, r_ref, gh_ref, o_ref, *, dv):
    g = gh_ref[...]
    for r0 in range(0, o_ref.shape[0], NORM_ROWS):
        rows = pl.ds(r0, NORM_ROWS)
        for h in range(GLA_HEADS):
            sl = slice(h * dv, (h + 1) * dv)
            y = _rms(of_ref[rows, sl] + ob_ref[rows, sl]) * g
            r = r_ref[rows, sl]
            o_ref[rows, sl] = (y * (r * _sigmoid(r))).astype(o_ref.dtype)


def _gla_combine(o_f, o_b, r, g_head, n_rows):
    d = o_f.shape[1]
    dv = d // GLA_HEADS
    tm = ROW_TILE
    return pl.pallas_call(
        functools.partial(_gla_combine_kernel, dv=dv),
        grid=(n_rows // tm,),
        in_specs=[
            pl.BlockSpec((tm, d), lambda i: (i, 0)),
            pl.BlockSpec((tm, d), lambda i: (i, 0)),
            pl.BlockSpec((tm, d), lambda i: (i, 0)),
            pl.BlockSpec((1, dv), lambda i: (0, 0)),
        ],
        out_specs=pl.BlockSpec((tm, d), lambda i: (i, 0)),
        out_shape=jax.ShapeDtypeStruct((n_rows, d), BF16),
        compiler_params=_cparams(("parallel",)),
        name="gla_combine",
    )(o_f, o_b, r, g_head.reshape(1, dv))


def _rope_tables(t_lat):
    t = jnp.arange(t_lat)
    row = (t // GRID_W).astype(F32)
    col = (t % GRID_W).astype(F32)
    n_freq = ROPE_DIM // 4
    inv_freq = ROPE_THETA ** (-jnp.arange(n_freq, dtype=F32) / n_freq)
    ang = jnp.concatenate([row[:, None] * inv_freq, col[:, None] * inv_freq], axis=-1)
    cos, sin = jnp.cos(ang), jnp.sin(ang)
    reps = LANES // ROPE_DIM
    return jnp.tile(jnp.concatenate([cos, cos], axis=-1), (1, reps)), jnp.tile(jnp.concatenate([-sin, sin], axis=-1), (1, reps))


MLA_TQ = 512
MLA_TK = 1024
MLA_HEADS_PER_STEP = 2


def _mla_attn_kernel(qn_ref, qr_ref, qnc_ref, qrc_ref, knc_ref, vc_ref, krc_ref, knl_ref, vl_ref, krl_ref,
                     o_ref, oc_ref, kcat, vext, *, tc, tl):
    t = pl.program_id(2)
    nh = MLA_HEADS_PER_STEP
    rd = ROPE_DIM
    vw = MLA_V + LANES

    def qcat(n_ref, r_ref, h):
        return jnp.concatenate([n_ref[:, h * MLA_NOPE:(h + 1) * MLA_NOPE], r_ref[:, h * rd:(h + 1) * rd]], axis=1)

    def finish(acc):
        return acc[:, 0:MLA_V] / acc[:, MLA_V:vw]

    @pl.when(t == 0)
    def _():
        for h in range(nh):
            ns = slice(h * MLA_NOPE, (h + 1) * MLA_NOPE)
            kcat[h, 0:tc, 0:MLA_NOPE] = knc_ref[:, ns]
            kcat[h, 0:tc, MLA_NOPE:MLA_NOPE + rd] = krc_ref[:, 0:rd]
            kcat[h, tc:tc + tl, 0:MLA_NOPE] = knl_ref[:, ns]
            kcat[h, tc:tc + tl, MLA_NOPE:MLA_NOPE + rd] = krl_ref[:, 0:rd]
            vext[0:tc, h * vw:h * vw + MLA_V] = vc_ref[:, h * MLA_V:(h + 1) * MLA_V]
            vext[tc:tc + tl, h * vw:h * vw + MLA_V] = vl_ref[:, h * MLA_V:(h + 1) * MLA_V]
            vext[:, h * vw + MLA_V:(h + 1) * vw] = jnp.ones((tc + tl, LANES), BF16)
        for h in range(nh):
            s = _dot_nt(qcat(qnc_ref, qrc_ref, h), kcat[h, 0:tc, :])
            p = jnp.exp2(s - jnp.max(s, axis=-1, keepdims=True))
            acc = _dot(p.astype(BF16), vext[0:tc, h * vw:(h + 1) * vw])
            oc_ref[:, h * MLA_V:(h + 1) * MLA_V] = finish(acc).astype(oc_ref.dtype)

    for h in range(nh):
        qc = qcat(qn_ref, qr_ref, h)
        s = _dot_nt(qc, kcat[h, 0:tc, :])
        m = jnp.max(s, axis=-1, keepdims=True)
        acc = _dot(jnp.exp2(s - m).astype(BF16), vext[0:tc, h * vw:(h + 1) * vw])
        for c0 in range(tc, tc + tl, MLA_TK):
            s = _dot_nt(qc, kcat[h, c0:c0 + MLA_TK, :])
            m_new = jnp.maximum(m, jnp.max(s, axis=-1, keepdims=True))
            acc = jnp.exp2(m - m_new) * acc + _dot(jnp.exp2(s - m_new).astype(BF16),
                                                   vext[c0:c0 + MLA_TK, h * vw:(h + 1) * vw])
            m = m_new
        o_ref[:, h * MLA_V:(h + 1) * MLA_V] = finish(acc).astype(o_ref.dtype)


def _mla_attention(qn, qr, kv, kr, b_sz, t_lat, t_ctx):
    hd = qn.shape[1]
    nh = MLA_HEADS_PER_STEP
    n_heads = hd // MLA_NOPE
    tq = MLA_TQ
    assert t_lat % tq == 0 and t_lat % MLA_TK == 0
    nlt = t_lat // tq
    ctx_blk0 = (b_sz * t_lat) // t_ctx
    vcol0 = n_heads // nh
    wn, wr = nh * MLA_NOPE, nh * ROPE_DIM
    return pl.pallas_call(
        functools.partial(_mla_attn_kernel, tc=t_ctx, tl=t_lat),
        grid=(b_sz, n_heads // nh, nlt),
        in_specs=[
            pl.BlockSpec((tq, wn), lambda b, h, t: (b * nlt + t, h)),
            pl.BlockSpec((tq, wr), lambda b, h, t: (b * nlt + t, h)),
            pl.BlockSpec((t_ctx, wn), lambda b, h, t: (ctx_blk0 + b, h)),
            pl.BlockSpec((t_ctx, wr), lambda b, h, t: (ctx_blk0 + b, h)),
            pl.BlockSpec((t_ctx, wn), lambda b, h, t: (ctx_blk0 + b, h)),
            pl.BlockSpec((t_ctx, wn), lambda b, h, t: (ctx_blk0 + b, vcol0 + h)),
            pl.BlockSpec((t_ctx, LANES), lambda b, h, t: (ctx_blk0 + b, 0)),
            pl.BlockSpec((t_lat, wn), lambda b, h, t: (b, h)),
            pl.BlockSpec((t_lat, wn), lambda b, h, t: (b, vcol0 + h)),
            pl.BlockSpec((t_lat, LANES), lambda b, h, t: (b, 0)),
        ],
        out_specs=[pl.BlockSpec((tq, nh * MLA_V), lambda b, h, t: (b * nlt + t, h)),
                   pl.BlockSpec((t_ctx, nh * MLA_V), lambda b, h, t: (b, h))],
        out_shape=[jax.ShapeDtypeStruct((b_sz * t_lat, n_heads * MLA_V), BF16),
                   jax.ShapeDtypeStruct((b_sz * t_ctx, n_heads * MLA_V), BF16)],
        scratch_shapes=[pltpu.VMEM((nh, t_ctx + t_lat, MLA_NOPE + ROPE_DIM), BF16),
                        pltpu.VMEM((t_ctx + t_lat, nh * (MLA_V + LANES)), BF16)],
        compiler_params=_cparams(("parallel", "parallel", "arbitrary")),
        name="mla_attention",
    )(qn, qr, qn, qr, kv, kv, kr, kv, kv, kr)


def _swa_attn_kernel(sink_ref, q_ref, kc_ref, vc_ref, kp_ref, kc0_ref, kn_ref, vp_ref, vc0_ref, vn_ref, o_ref,
                     *, nct, nb, n_kv):
    t = pl.program_id(1)
    n = t - nct
    blk = WINDOW
    hd = SWA_HEAD_DIM
    tc = kc_ref.shape[0]
    nk = tc + 3 * blk
    npair = SWA_GROUP // 2

    qi = lax.broadcasted_iota(jnp.int32, (blk, 3 * blk), 0)
    kj = lax.broadcasted_iota(jnp.int32, (blk, 3 * blk), 1)
    lo = jnp.where(n > 0, 0, blk)
    hi = jnp.where(t < nct, 0, jnp.where(n < nb - 1, 3 * blk, 2 * blk))
    win_ok = (jnp.abs(qi - kj + blk) <= WINDOW) & (kj >= lo) & (kj < hi)
    bias = jnp.concatenate([jnp.zeros((blk, tc), F32), jnp.where(win_ok, 0.0, NEG_BIG)], axis=1)
    bias = jnp.concatenate([bias] * npair, axis=0)
    lane = lax.broadcasted_iota(jnp.int32, (blk, 2 * hd), 1)
    zeros = jnp.zeros((nk, hd), BF16)
    ones = jnp.ones((nk, hd), BF16)

    def kv_chain(kvh):
        ks = slice(kvh * hd, (kvh + 1) * hd)
        k_all = jnp.concatenate([kc_ref[:, ks], kp_ref[:, ks], kc0_ref[:, ks], kn_ref[:, ks]], axis=0)
        v_all = jnp.concatenate([vc_ref[:, ks], vp_ref[:, ks], vc0_ref[:, ks], vn_ref[:, ks]], axis=0)
        k_bd = jnp.concatenate([jnp.concatenate([k_all, zeros], axis=1),
                                jnp.concatenate([zeros, k_all], axis=1)], axis=0)
        h0 = kvh * SWA_GROUP
        q_st = jnp.concatenate([q_ref[:, (h0 + 2 * j) * hd:(h0 + 2 * j + 2) * hd] for j in range(npair)], axis=0)
        s = _dot_nt(q_st, k_bd)
        ps, ms = [], []
        for e in range(2):
            sink_col = jnp.concatenate([jnp.full((blk, 1), sink_ref[h0 + 2 * j + e] * LOG2E, F32)
                                        for j in range(npair)], axis=0)
            s_e = s[:, e * nk:(e + 1) * nk] + bias
            m = jnp.maximum(jnp.max(s_e, axis=-1, keepdims=True), sink_col)
            p = jnp.exp2(s_e - m).astype(BF16)
            ps += [p[:, 0:tc], p[:, tc:]]
            ms.append(sink_col - m)

        def v_ext(r0, r1, e):
            v, z, o = v_all[r0:r1], zeros[r0:r1], ones[r0:r1]
            return jnp.concatenate([z, v, z, o] if e else [v, z, o, z], axis=1)

        tail = tc + 2 * blk
        acc = _dot(jnp.concatenate([ps[1][:, 2 * blk:], ps[3][:, 2 * blk:]], axis=1),
                   jnp.concatenate([v_ext(tail, nk, 0), v_ext(tail, nk, 1)], axis=0))
        for e in range(2):
            acc = acc + _dot(ps[2 * e], v_ext(0, tc, e)) + _dot(ps[2 * e + 1][:, 0:2 * blk], v_ext(tc, tail, e))
        for j in range(npair):
            r = slice(j * blk, (j + 1) * blk)
            sink_term = jnp.exp2(jnp.where(lane < hd, ms[0][r], ms[1][r]))
            o = acc[r, 0:2 * hd] / (acc[r, 2 * hd:4 * hd] + sink_term)
            o_ref[:, (h0 + 2 * j) * hd:(h0 + 2 * j + 2) * hd] = o.astype(o_ref.dtype)

    for kvh in range(n_kv):
        kv_chain(kvh)


def _swa_attention(qkv, sinks, b_sz, t_lat, t_ctx):
    n_tok = qkv.shape[0]
    blk = WINDOW
    hd = SWA_HEAD_DIM
    n_q = sinks.shape[0]
    n_kv = n_q // SWA_GROUP
    kvw = n_kv * hd
    kcol = (n_q * hd) // kvw
    vcol = kcol + 1
    nct, nb = t_ctx // blk, t_lat // blk
    ctx_blk0 = (b_sz * t_lat) // blk
    ctx_row0 = (b_sz * t_lat) // t_ctx

    def qrow(b, t):
        return jnp.where(t < nct, ctx_blk0 + b * nct + t, b * nb + t - nct)

    def krow(b, t, off):
        return b * nb + jnp.clip(t - nct + off, 0, nb - 1)

    kspec = lambda off: pl.BlockSpec((blk, kvw), lambda b, t: (krow(b, t, off), kcol))
    vspec = lambda off: pl.BlockSpec((blk, kvw), lambda b, t: (krow(b, t, off), vcol))
    return pl.pallas_call(
        functools.partial(_swa_attn_kernel, nct=nct, nb=nb, n_kv=n_kv),
        grid=(b_sz, nct + nb),
        in_specs=[
            pl.BlockSpec(memory_space=pltpu.SMEM),
            pl.BlockSpec((blk, n_q * hd), lambda b, t: (qrow(b, t), 0)),
            pl.BlockSpec((t_ctx, kvw), lambda b, t: (ctx_row0 + b, kcol)),
            pl.BlockSpec((t_ctx, kvw), lambda b, t: (ctx_row0 + b, vcol)),
            kspec(-1), kspec(0), kspec(1), vspec(-1), vspec(0), vspec(1),
        ],
        out_specs=pl.BlockSpec((blk, n_q * hd), lambda b, t: (qrow(b, t), 0)),
        out_shape=jax.ShapeDtypeStruct((n_tok, n_q * hd), BF16),
        compiler_params=_cparams(("parallel", "arbitrary")),
        name="swa_attention",
    )(sinks, qkv, qkv, qkv, qkv, qkv, qkv, qkv, qkv, qkv)


def kernel(x, c, ctx, c_ctx, w_ada, b_ada, g_norm, w_ffn_in, w_ffn_out, gla_w_in, gla_w_gate_down, gla_w_gate_up, gla_b_gate, gla_g_head, gla_w_out, mla_w_in, mla_g_q, mla_w_uq, mla_g_kv, mla_w_ukv, mla_w_out, swa_w_in, swa_sinks, swa_w_out):
    b_sz, t_lat, d = x.shape
    t_ctx = ctx.shape[1]
    depth = w_ada.shape[0]
    n_lat, n_ctx = b_sz * t_lat, b_sz * t_ctx
    n_tok = n_lat + n_ctx
    tm = ROW_TILE
    assert t_lat % tm == 0 and n_ctx % tm == 0 and b_sz < MOD_ROWS
    tpb = t_lat // tm
    n_lat_tiles = n_lat // tm

    hs = jnp.concatenate([x.reshape(n_lat, d), ctx.reshape(n_ctx, d)], axis=0)
    cond = jnp.zeros((MOD_ROWS, d), F32).at[:b_sz].set(c).at[b_sz].set(c_ctx)
    mod_all = _modulation(cond, w_ada, b_ada)
    cosf, sinf = _rope_tables(t_lat)
    common = dict(tiles_per_batch=tpb, nb=b_sz)
    tm_big = 2 * tm if t_lat % (2 * tm) == 0 and n_ctx % (2 * tm) == 0 else tm
    w_ffn_in_b, w_ffn_out_b = w_ffn_in.astype(BF16), w_ffn_out.astype(BF16)

    for i in range(depth):
        kind, j = i % 3, i // 3
        last = i == depth - 1
        n_out = n_lat if last else n_tok
        mod = mod_all[i].reshape(MOD_ROWS, 1, 6 * d)

        if kind == 0:
            rank = GLA_GATE_RANK
            qk_w = gla_w_gate_up.shape[-1]
            pad = LANES - 2 * rank
            w_gd = jnp.concatenate([gla_w_gate_down[j, 0], gla_w_gate_down[j, 1], jnp.zeros((d, pad), F32)],
                                   axis=1).astype(BF16)
            qkv, r, gd = _gla_in_proj(hs, g_norm[i, 0], mod, gla_w_in[j].astype(BF16), w_gd, 2 * qk_w + d, tm_big,
                                      t_lat // tm_big, b_sz)
            wu_ext = jnp.zeros((2, LANES, qk_w), F32)
            for dr in range(2):
                wu_ext = wu_ext.at[dr, dr * rank:(dr + 1) * rank].set(gla_w_gate_up[j, dr])
            o_f, o_b = _gla_scan(qkv, gd, wu_ext.astype(BF16), gla_b_gate[j].reshape(2, 1, qk_w), b_sz, t_lat, t_ctx)
            a = a2 = _gla_combine(o_f, o_b, r, gla_g_head[j], n_out)
            w_o = gla_w_out[j]
        elif kind == 1:
            n_heads = mla_w_out.shape[1] // MLA_V
            w_in = jnp.concatenate([mla_w_in[j], jnp.zeros((d, LANES - ROPE_DIM), F32)], axis=1).astype(BF16)
            w_uq = mla_w_uq[j].reshape(MLA_Q_RANK, n_heads, MLA_NOPE + ROPE_DIM)
            w_uq = jnp.concatenate([w_uq[:, :, :MLA_NOPE].reshape(MLA_Q_RANK, -1),
                                    w_uq[:, :, MLA_NOPE:].reshape(MLA_Q_RANK, -1)], axis=1).astype(BF16)
            w_ukv = mla_w_ukv[j].reshape(MLA_KV_RANK, n_heads, MLA_NOPE + MLA_V)
            w_ukv = jnp.concatenate([w_ukv[:, :, :MLA_NOPE].reshape(MLA_KV_RANK, -1),
                                     w_ukv[:, :, MLA_NOPE:].reshape(MLA_KV_RANK, -1)], axis=1).astype(BF16)
            q_scale = (MLA_NOPE + ROPE_DIM) ** -0.5 * LOG2E
            qn, qr, kv, kr = _mla_in_proj(hs, g_norm[i, 0], mod, w_in, mla_g_q[j], w_uq, mla_g_kv[j], w_ukv,
                                          n_heads * MLA_NOPE, q_scale, cosf, sinf, n_lat_tiles, **common)
            a, a2 = _mla_attention(qn, qr, kv, kr, b_sz, t_lat, t_ctx)
            w_o = mla_w_out[j]
        else:
            n_q = swa_sinks.shape[1]
            qkw = (n_q + n_q // SWA_GROUP) * SWA_HEAD_DIM
            qkv = _swa_in_proj(hs, g_norm[i, 0], mod, swa_w_in[j].astype(BF16), cosf, sinf, qkw // LANES,
                               (n_q * SWA_HEAD_DIM) // LANES, SWA_HEAD_DIM ** -0.5 * LOG2E, n_lat_tiles, **common)
            a = a2 = _swa_attention(qkv, swa_sinks[j], b_sz, t_lat, t_ctx)
            w_o = swa_w_out[j]

        hs = _proj_resid(a, a2, w_o.astype(BF16), hs, g_norm[i, 1], mod, 2, n_out, name="mixer_out_proj", **common)
        hs = _ffn(hs, g_norm[i, 2], g_norm[i, 3], mod, w_ffn_in_b, w_ffn_out_b, i, n_out, tm_big, t_lat // tm_big, b_sz,
                  name="ffn")

    return hs.reshape(b_sz, t_lat, d)
```

```python
import functools

import numpy as np
import jax
import jax.numpy as jnp
from jax import lax
from jax.experimental import pallas as pl
from jax.experimental.pallas import tpu as pltpu

F32 = jnp.float32
BF16 = jnp.bfloat16

EPS = 1e-6
ROPE_THETA = 10000.0
ROPE_DIM = 64
GRID_W = 64
WINDOW = 128
GLA_HEADS = 4
GLA_GATE_RANK = 16
GLA_TAU = 16.0
MLA_Q_RANK = 512
MLA_KV_RANK = 512
MLA_NOPE = 128
MLA_V = 128
SWA_HEAD_DIM = 64
SWA_GROUP = 8

LANES = 128
SUBLANES = 8
VMEM_LIMIT = 56 * 1024 * 1024
ROW_TILE = 512
MOD_ROWS = 16
NEG_BIG = -1e30
LOG2E = 1.4426950408889634


def _cparams(sem):
    return pltpu.CompilerParams(dimension_semantics=sem, vmem_limit_bytes=VMEM_LIMIT)


def _dot(a, b):
    return jnp.dot(a, b, preferred_element_type=F32)


def _dot_nt(a, b):
    return lax.dot_general(a, b, (((1,), (1,)), ((), ())), preferred_element_type=F32)


def _dot_tn(a, b):
    return lax.dot_general(a, b, (((0,), (0,)), ((), ())), preferred_element_type=F32)


def _sigmoid(x):
    return 1.0 / (1.0 + jnp.exp(-x))


def _rms(x):
    return x * lax.rsqrt(jnp.mean(x * x, axis=-1, keepdims=True) + EPS)


def _mod_kernel(c_ref, w_ref, b_ref, o_ref):
    c = c_ref[...]
    s = (c * _sigmoid(c)).astype(BF16)
    o_ref[0] = _dot(s, w_ref[0].astype(BF16)) + b_ref[0]


def _modulation(cond, w_ada, b_ada):
    depth, d, n = w_ada.shape
    tn = 1024
    return pl.pallas_call(
        _mod_kernel,
        grid=(depth, n // tn),
        in_specs=[
            pl.BlockSpec((MOD_ROWS, d), lambda l, j: (0, 0)),
            pl.BlockSpec((1, d, tn), lambda l, j: (l, 0, j)),
            pl.BlockSpec((1, 1, tn), lambda l, j: (l, 0, j)),
        ],
        out_specs=pl.BlockSpec((1, MOD_ROWS, tn), lambda l, j: (l, 0, j)),
        out_shape=jax.ShapeDtypeStruct((depth, MOD_ROWS, n), F32),
        compiler_params=_cparams(("parallel", "parallel")),
        name="modulation",
    )(cond, w_ada, b_ada.reshape(depth, 1, n))


NORM_ROWS = 16


def _adaln_rows(dst_ref, x_ref, g_ref, mod_ref, shift_idx, scale_idx, d):
    m = mod_ref[0]
    gs = g_ref[...] * (1.0 + m[:, scale_idx * d:(scale_idx + 1) * d])
    sh = m[:, shift_idx * d:(shift_idx + 1) * d]
    for r in range(0, x_ref.shape[0], NORM_ROWS):
        rows = pl.ds(r, NORM_ROWS)
        dst_ref[rows, :] = (_rms(x_ref[rows, :]) * gs + sh).astype(dst_ref.dtype)


def _resid_norm_rows(o_ref, h_ref, y_ref, g_ref, mod_ref, gate_idx, d):
    gg = mod_ref[0][:, gate_idx * d:(gate_idx + 1) * d] * g_ref[...]
    for r in range(0, h_ref.shape[0], NORM_ROWS):
        rows = pl.ds(r, NORM_ROWS)
        o_ref[rows, :] = h_ref[rows, :] + _rms(y_ref[rows, :]) * gg


def _rope_chunk(y, cosf, sinf, first, is_lat):
    partner = jnp.where(first, pltpu.roll(y, LANES - ROPE_DIM // 2, 1), pltpu.roll(y, ROPE_DIM // 2, 1))
    return jnp.where(is_lat, y * cosf + partner * sinf, y)


def _first_half_lanes(shape):
    lane = lax.broadcasted_iota(jnp.int32, shape, 1)
    return (lane % ROPE_DIM) < (ROPE_DIM // 2)


def _mod_spec(mod, tiles_per_batch, nb):
    return pl.BlockSpec((1, 1, mod.shape[-1]), lambda i, *_: (jnp.minimum(i // tiles_per_batch, nb), 0, 0))


def _gla_in_proj_kernel(x_ref, g_ref, mod_ref, w_ref, wgd_ref, qkv_ref, r_ref, gd_ref, a_scr, *, d, n_qkv_tiles):
    j = pl.program_id(1)

    @pl.when(j == 0)
    def _():
        _adaln_rows(a_scr, x_ref, g_ref, mod_ref, 0, 1, d)
        gd_ref[...] = _dot(a_scr[...], wgd_ref[...])

    @pl.when(j < n_qkv_tiles)
    def _():
        qkv_ref[...] = _dot(a_scr[...], w_ref[...]).astype(qkv_ref.dtype)

    @pl.when(j >= n_qkv_tiles)
    def _():
        r_ref[...] = _dot(a_scr[...], w_ref[...])


def _gla_in_proj(hs, g, mod, w, w_gd, n_qkv, tm, tiles_per_batch, nb):
    n_rows, d = hs.shape
    n = w.shape[1]
    tn = 1024
    nq = n_qkv // tn
    n_side = w_gd.shape[1]
    return pl.pallas_call(
        functools.partial(_gla_in_proj_kernel, d=d, n_qkv_tiles=nq),
        grid=(n_rows // tm, n // tn),
        in_specs=[
            pl.BlockSpec((tm, d), lambda i, j: (i, 0)),
            pl.BlockSpec((1, d), lambda i, j: (0, 0)),
            _mod_spec(mod, tiles_per_batch, nb),
            pl.BlockSpec((d, tn), lambda i, j: (0, j)),
            pl.BlockSpec((d, n_side), lambda i, j: (0, 0)),
        ],
        out_specs=[pl.BlockSpec((tm, tn), lambda i, j: (i, jnp.minimum(j, nq - 1))),
                   pl.BlockSpec((tm, tn), lambda i, j: (i, jnp.maximum(j - nq, 0))),
                   pl.BlockSpec((tm, n_side), lambda i, j: (i, 0))],
        out_shape=[jax.ShapeDtypeStruct((n_rows, n_qkv), BF16),
                   jax.ShapeDtypeStruct((n_rows, n - n_qkv), F32),
                   jax.ShapeDtypeStruct((n_rows, n_side), F32)],
        scratch_shapes=[pltpu.VMEM((tm, d), BF16)],
        compiler_params=_cparams(("parallel", "arbitrary")),
        name="gla_in_proj",
    )(hs, g.reshape(1, d), mod, w, w_gd)


def _swa_in_proj_kernel(x_ref, g_ref, mod_ref, cos_ref, sin_ref, w_ref, o_ref, a_scr, *, d, n_lat_tiles, n_roped,
                        n_scaled, out_scale):
    _adaln_rows(a_scr, x_ref, g_ref, mod_ref, 0, 1, d)
    y_all = _dot(a_scr[...], w_ref[...])
    is_lat = pl.program_id(0) < n_lat_tiles
    cosf, sinf = cos_ref[...], sin_ref[...]
    first = _first_half_lanes(cosf.shape)
    for c in range(y_all.shape[1] // LANES):
        y = y_all[:, c * LANES:(c + 1) * LANES]
        if c < n_roped:
            y = _rope_chunk(y, cosf, sinf, first, is_lat)
        if c < n_scaled:
            y = y * out_scale
        o_ref[:, c * LANES:(c + 1) * LANES] = y.astype(o_ref.dtype)


def _swa_in_proj(hs, g, mod, w, cosf, sinf, n_roped, n_scaled, out_scale, n_lat_tiles, tiles_per_batch, nb):
    n_rows, d = hs.shape
    n = w.shape[1]
    tm = ROW_TILE
    return pl.pallas_call(
        functools.partial(_swa_in_proj_kernel, d=d, n_lat_tiles=n_lat_tiles, n_roped=n_roped, n_scaled=n_scaled,
                          out_scale=out_scale),
        grid=(n_rows // tm,),
        in_specs=[
            pl.BlockSpec((tm, d), lambda i: (i, 0)),
            pl.BlockSpec((1, d), lambda i: (0, 0)),
            _mod_spec(mod, tiles_per_batch, nb),
            pl.BlockSpec((tm, LANES), lambda i: (i % tiles_per_batch, 0)),
            pl.BlockSpec((tm, LANES), lambda i: (i % tiles_per_batch, 0)),
            pl.BlockSpec((d, n), lambda i: (0, 0)),
        ],
        out_specs=pl.BlockSpec((tm, n), lambda i: (i, 0)),
        out_shape=jax.ShapeDtypeStruct((n_rows, n), BF16),
        scratch_shapes=[pltpu.VMEM((tm, d), BF16)],
        compiler_params=_cparams(("parallel",)),
        name="swa_in_proj",
    )(hs, g.reshape(1, d), mod, cosf, sinf, w)


def _mla_in_proj_kernel(x_ref, g_ref, mod_ref, cos_ref, sin_ref, win_ref, gq_ref, gkv_ref, wuq_ref, wukv_ref,
                        qn_ref, qr_ref, kv_ref, kr_ref, a_scr, *, d, n_lat_tiles, q_scale):
    _adaln_rows(a_scr, x_ref, g_ref, mod_ref, 0, 1, d)
    p1 = _dot(a_scr[...], win_ref[...])
    cq = (_rms(p1[:, 0:MLA_Q_RANK]) * gq_ref[...]).astype(BF16)
    ckv = (_rms(p1[:, MLA_Q_RANK:MLA_Q_RANK + MLA_KV_RANK]) * gkv_ref[...]).astype(BF16)
    q = _dot(cq, wuq_ref[...]) * q_scale
    n_nope = qn_ref.shape[1]
    qn_ref[...] = q[:, 0:n_nope].astype(qn_ref.dtype)
    kv_ref[...] = _dot(ckv, wukv_ref[...]).astype(kv_ref.dtype)
    is_lat = pl.program_id(0) < n_lat_tiles
    cosf, sinf = cos_ref[...], sin_ref[...]
    first = _first_half_lanes(cosf.shape)
    for c in range(qr_ref.shape[1] // LANES):
        y = q[:, n_nope + c * LANES:n_nope + (c + 1) * LANES]
        qr_ref[:, c * LANES:(c + 1) * LANES] = _rope_chunk(y, cosf, sinf, first, is_lat).astype(qr_ref.dtype)
    k_rope = p1[:, MLA_Q_RANK + MLA_KV_RANK:MLA_Q_RANK + MLA_KV_RANK + LANES]
    kr_ref[...] = _rope_chunk(k_rope, cosf, sinf, first, is_lat).astype(kr_ref.dtype)


def _mla_in_proj(hs, g, mod, w_in, g_q, w_uq, g_kv, w_ukv, n_nope, q_scale, cosf, sinf, n_lat_tiles, tiles_per_batch,
                 nb):
    n_rows, d = hs.shape
    tm = ROW_TILE
    n_q, n_kv = w_uq.shape[1], w_ukv.shape[1]
    const = lambda shape: pl.BlockSpec(shape, lambda i: (0, 0))
    row = lambda w: pl.BlockSpec((tm, w), lambda i: (i, 0))
    return pl.pallas_call(
        functools.partial(_mla_in_proj_kernel, d=d, n_lat_tiles=n_lat_tiles, q_scale=q_scale),
        grid=(n_rows // tm,),
        in_specs=[
            row(d), const((1, d)), _mod_spec(mod, tiles_per_batch, nb),
            pl.BlockSpec((tm, LANES), lambda i: (i % tiles_per_batch, 0)),
            pl.BlockSpec((tm, LANES), lambda i: (i % tiles_per_batch, 0)),
            const(w_in.shape), const((1, MLA_Q_RANK)), const((1, MLA_KV_RANK)), const(w_uq.shape), const(w_ukv.shape),
        ],
        out_specs=[row(n_nope), row(n_q - n_nope), row(n_kv), row(LANES)],
        out_shape=[jax.ShapeDtypeStruct((n_rows, n_nope), BF16), jax.ShapeDtypeStruct((n_rows, n_q - n_nope), BF16),
                   jax.ShapeDtypeStruct((n_rows, n_kv), BF16), jax.ShapeDtypeStruct((n_rows, LANES), BF16)],
        scratch_shapes=[pltpu.VMEM((tm, d), BF16)],
        compiler_params=_cparams(("parallel",)),
        name="mla_in_proj",
    )(hs, g.reshape(1, d), mod, cosf, sinf, w_in, g_q.reshape(1, -1), g_kv.reshape(1, -1), w_uq, w_ukv)


def _proj_resid_kernel(a_ref, a2_ref, w_ref, h_ref, g_ref, mod_ref, o_ref, y_scr, *, gate_idx, d, n1):
    def body(src_ref):
        y_scr[...] = _dot(src_ref[...].astype(BF16), w_ref[...])
        _resid_norm_rows(o_ref, h_ref, y_scr, g_ref, mod_ref, gate_idx, d)

    pl.when(pl.program_id(0) < n1)(lambda: body(a_ref))
    pl.when(pl.program_id(0) >= n1)(lambda: body(a2_ref))


def _proj_resid(a, a2, w, hs, g, mod, gate_idx, n_rows, tiles_per_batch, nb, name):
    k, d = w.shape
    tm = ROW_TILE
    n1 = a.shape[0] // tm
    return pl.pallas_call(
        functools.partial(_proj_resid_kernel, gate_idx=gate_idx, d=d, n1=n1),
        grid=(n_rows // tm,),
        in_specs=[
            pl.BlockSpec((tm, k), lambda i: (jnp.minimum(i, n1 - 1), 0)),
            pl.BlockSpec((tm, k), lambda i: (jnp.maximum(i - n1, 0), 0)),
            pl.BlockSpec((k, d), lambda i: (0, 0)),
            pl.BlockSpec((tm, d), lambda i: (i, 0)),
            pl.BlockSpec((1, d), lambda i: (0, 0)),
            _mod_spec(mod, tiles_per_batch, nb),
        ],
        out_specs=pl.BlockSpec((tm, d), lambda i: (i, 0)),
        out_shape=jax.ShapeDtypeStruct((n_rows, d), F32),
        scratch_shapes=[pltpu.VMEM((tm, d), F32)],
        compiler_params=_cparams(("parallel",)),
        name=name,
    )(a, a2, w, hs, g.reshape(1, d), mod)


def _ffn_kernel(h_ref, g2_ref, g3_ref, mod_ref, wg_ref, wu_ref, wo_ref, o_ref, a_scr, *, d):
    j = pl.program_id(1)

    @pl.when(j == 0)
    def _():
        _adaln_rows(a_scr, h_ref, g2_ref, mod_ref, 3, 4, d)
        o_ref[...] = jnp.zeros_like(o_ref)

    a = a_scr[...]
    gt = _dot(a, wg_ref[...])
    up = _dot(a, wu_ref[...])
    act = (gt * _sigmoid(gt) * up).astype(BF16)
    o_ref[...] += _dot(act, wo_ref[...])

    @pl.when(j == pl.num_programs(1) - 1)
    def _():
        _resid_norm_rows(o_ref, h_ref, o_ref, g3_ref, mod_ref, 5, d)


def _ffn(hs, g2, g3, mod, w_in, w_out, layer, n_rows, tm, tiles_per_batch, nb, name):
    d = hs.shape[1]
    f = w_out.shape[1]
    tf = 512
    nf = f // tf
    return pl.pallas_call(
        functools.partial(_ffn_kernel, d=d),
        grid=(n_rows // tm, nf),
        in_specs=[
            pl.BlockSpec((tm, d), lambda i, j: (i, 0)),
            pl.BlockSpec((1, d), lambda i, j: (0, 0)),
            pl.BlockSpec((1, d), lambda i, j: (0, 0)),
            _mod_spec(mod, tiles_per_batch, nb),
            pl.BlockSpec((None, d, tf), lambda i, j: (layer, 0, j)),
            pl.BlockSpec((None, d, tf), lambda i, j: (layer, 0, nf + j)),
            pl.BlockSpec((None, tf, d), lambda i, j: (layer, j, 0)),
        ],
        out_specs=pl.BlockSpec((tm, d), lambda i, j: (i, 0)),
        out_shape=jax.ShapeDtypeStruct((n_rows, d), F32),
        scratch_shapes=[pltpu.VMEM((tm, d), BF16)],
        compiler_params=_cparams(("parallel", "arbitrary")),
        name=name,
    )(hs, g2.reshape(1, d), g3.reshape(1, d), mod, w_in, w_in, w_out)


GLA_CHUNK = 128


def _gla_structure(cs, reverse):
    idx = np.arange(cs)
    ip = cs - 1 - idx if reverse else idx
    ii, jj = ip[:, None], ip[None, :]
    masks, dist = [], []
    s = cs // 2
    while s >= 1:
        masks.append(((ii // (2 * s)) == (jj // (2 * s))) & ((ii & s) != 0) & ((jj & s) == 0))
        pp = (ii // (2 * s)) * (2 * s) + s - 1
        dist.append((jj > np.minimum(ii, pp)) & (jj <= np.maximum(ii, pp)))
        s //= 2
    masks.append(ii == jj)
    return (np.stack(masks).astype(np.float32), (jj <= ii).astype(np.float32),
            np.concatenate(dist, axis=0).astype(np.float32))


def _gla_chunk(q_b, k_b, v_b, lg, msk_ref, tri, dist_ref, st_ref, *, cs, reverse):
    hi = lg.astype(BF16)
    lo = (lg - hi.astype(F32)).astype(BF16)
    b = _dot(tri, hi) + _dot(tri, lo)
    yield
    last = 0 if reverse else cs - 1
    b_last = b[last:last + 1, :]

    st = st_ref[...]
    o = _dot_nt(q_b * jnp.exp2(b).astype(BF16), st.astype(BF16))
    yield

    attn = jnp.zeros((cs, cs), F32)
    nlev = dist_ref.shape[0] // cs
    for lvl in range(nlev):
        nd = _dot(dist_ref[lvl * cs:(lvl + 1) * cs, :], hi)
        fac = jnp.exp2(nd).astype(BF16)
        attn = attn + msk_ref[lvl] * _dot_nt(q_b * fac, k_b * fac)
        yield
    attn = attn + msk_ref[nlev] * _dot_nt(q_b, k_b)

    k_dec = k_b * jnp.exp2(b_last - b).astype(BF16)
    st_ref[...] = st * jnp.exp2(b_last) + _dot_tn(v_b, k_dec)
    yield
    return o + _dot(attn.astype(BF16), v_b)


def _interleave(gens):
    results = [None] * len(gens)
    active = list(range(len(gens)))
    while active:
        for idx in list(active):
            try:
                next(gens[idx])
            except StopIteration as stop:
                results[idx] = stop.value
                active.remove(idx)
    return results


def _gla_scan_kernel(qf_ref, kf_ref, vf_ref, gf_ref, qb_ref, kb_ref, vb_ref, gb_ref, wu_ref, bg_ref,
                     msk_ref, tri_ref, dist_ref, of_ref, ob_ref, st_scr, *, cs, dk, dv):
    @pl.when(pl.program_id(1) == 0)
    def _():
        st_scr[...] = jnp.zeros_like(st_scr)

    dirs = ((qf_ref, kf_ref, vf_ref, gf_ref, of_ref), (qb_ref, kb_ref, vb_ref, gb_ref, ob_ref))
    chains, dests = [], []
    for dr, (q_ref, k_ref, v_ref, gd_ref, o_ref) in enumerate(dirs):
        z = _dot(gd_ref[...].astype(BF16), wu_ref[dr]) + bg_ref[dr]
        lg_all = (jnp.minimum(z, 0.0) - jnp.log1p(jnp.exp(-jnp.abs(z)))) * (LOG2E / GLA_TAU)
        for h in range(GLA_HEADS):
            ks = slice(h * dk, (h + 1) * dk)
            vs = slice(h * dv, (h + 1) * dv)
            chains.append(_gla_chunk(q_ref[:, ks] * (dk ** -0.5), k_ref[:, ks], v_ref[:, vs].astype(BF16),
                                     lg_all[:, ks], msk_ref.at[dr], tri_ref[dr], dist_ref.at[dr], st_scr.at[dr, h],
                                     cs=cs, reverse=bool(dr)))
            dests.append((o_ref, vs))
    for (o_ref, vs), o in zip(dests, _interleave(chains)):
        o_ref[:, vs] = o


def _gla_scan(p, gd, wu_ext, bg, b_sz, t_lat, t_ctx):
    cs = GLA_CHUNK
    qkw = wu_ext.shape[-1]
    dk = qkw // GLA_HEADS
    vw = 2 * qkw
    dv = vw // GLA_HEADS
    n_tok = p.shape[0]
    ncc, ncl = t_ctx // cs, t_lat // cs
    ctx_blk0 = (b_sz * t_lat) // cs
    structs = [_gla_structure(cs, rev) for rev in (False, True)]
    masks, tri, dist = (np.stack(parts) for parts in zip(*structs))

    def rowblk(rev):
        def f(b, s):
            if rev:
                cc, lc = ncc - 1 - s, ncl - 1 - (s - ncc)
            else:
                cc, lc = s, s - ncc
            return jnp.where(s < ncc, ctx_blk0 + b * ncc + cc, b * ncl + lc)
        return f

    def dir_specs(rev):
        rb = rowblk(rev)
        return [pl.BlockSpec((cs, qkw), lambda b, s: (rb(b, s), 0)),
                pl.BlockSpec((cs, qkw), lambda b, s: (rb(b, s), 1)),
                pl.BlockSpec((cs, vw), lambda b, s: (rb(b, s), 1)),
                pl.BlockSpec((cs, LANES), lambda b, s: (rb(b, s), 0))]

    return pl.pallas_call(
        functools.partial(_gla_scan_kernel, cs=cs, dk=dk, dv=dv),
        grid=(b_sz, ncc + ncl),
        in_specs=dir_specs(False) + dir_specs(True) + [
            pl.BlockSpec((2, LANES, qkw), lambda b, s: (0, 0, 0)),
            pl.BlockSpec((2, 1, qkw), lambda b, s: (0, 0, 0)),
            pl.BlockSpec(masks.shape, lambda b, s: (0, 0, 0, 0)),
            pl.BlockSpec(tri.shape, lambda b, s: (0, 0, 0)),
            pl.BlockSpec(dist.shape, lambda b, s: (0, 0, 0)),
        ],
        out_specs=[pl.BlockSpec((cs, vw), lambda b, s: (rowblk(False)(b, s), 0)),
                   pl.BlockSpec((cs, vw), lambda b, s: (rowblk(True)(b, s), 0))],
        out_shape=[jax.ShapeDtypeStruct((n_tok, vw), F32)] * 2,
        scratch_shapes=[pltpu.VMEM((2, GLA_HEADS, dv, dk), F32)],
        compiler_params=_cparams(("parallel", "arbitrary")),
        name="gla_scan",
    )(p, p, p, gd, p, p, p, gd, wu_ext, bg, jnp.asarray(masks), jnp.asarray(tri, dtype=BF16),
      jnp.asarray(dist, dtype=BF16))


def _gla_combine_kernel(of_ref, ob_ref, r_ref, gh_ref, o_ref, *, dv):
    g = gh_ref[...]
    for r0 in range(0, o_ref.shape[0], NORM_ROWS):
        rows = pl.ds(r0, NORM_ROWS)
        for h in range(GLA_HEADS):
            sl = slice(h * dv, (h + 1) * dv)
            y = _rms(of_ref[rows, sl] + ob_ref[rows, sl]) * g
            r = r_ref[rows, sl]
            o_ref[rows, sl] = (y * (r * _sigmoid(r))).astype(o_ref.dtype)


def _gla_combine(o_f, o_b, r, g_head, n_rows):
    d = o_f.shape[1]
    dv = d // GLA_HEADS
    tm = ROW_TILE
    return pl.pallas_call(
        functools.partial(_gla_combine_kernel, dv=dv),
        grid=(n_rows // tm,),
        in_specs=[
            pl.BlockSpec((tm, d), lambda i: (i, 0)),
            pl.BlockSpec((tm, d), lambda i: (i, 0)),
            pl.BlockSpec((tm, d), lambda i: (i, 0)),
            pl.BlockSpec((1, dv), lambda i: (0, 0)),
        ],
        out_specs=pl.BlockSpec((tm, d), lambda i: (i, 0)),
        out_shape=jax.ShapeDtypeStruct((n_rows, d), BF16),
        compiler_params=_cparams(("parallel",)),
        name="gla_combine",
    )(o_f, o_b, r, g_head.reshape(1, dv))


def _rope_tables(t_lat):
    t = jnp.arange(t_lat)
    row = (t // GRID_W).astype(F32)
    col = (t % GRID_W).astype(F32)
    n_freq = ROPE_DIM // 4
    inv_freq = ROPE_THETA ** (-jnp.arange(n_freq, dtype=F32) / n_freq)
    ang = jnp.concatenate([row[:, None] * inv_freq, col[:, None] * inv_freq], axis=-1)
    cos, sin = jnp.cos(ang), jnp.sin(ang)
    reps = LANES // ROPE_DIM
    return jnp.tile(jnp.concatenate([cos, cos], axis=-1), (1, reps)), jnp.tile(jnp.concatenate([-sin, sin], axis=-1), (1, reps))


MLA_TQ = 512
MLA_TK = 1024
MLA_HEADS_PER_STEP = 2


def _mla_attn_kernel(qn_ref, qr_ref, qnc_ref, qrc_ref, knc_ref, vc_ref, krc_ref, knl_ref, vl_ref, krl_ref,
                     o_ref, oc_ref, kcat, vext, *, tc, tl):
    t = pl.program_id(2)
    nh = MLA_HEADS_PER_STEP
    rd = ROPE_DIM
    vw = MLA_V + LANES

    def qcat(n_ref, r_ref, h):
        return jnp.concatenate([n_ref[:, h * MLA_NOPE:(h + 1) * MLA_NOPE], r_ref[:, h * rd:(h + 1) * rd]], axis=1)

    def finish(acc):
        return acc[:, 0:MLA_V] / acc[:, MLA_V:vw]

    @pl.when(t == 0)
    def _():
        for h in range(nh):
            ns = slice(h * MLA_NOPE, (h + 1) * MLA_NOPE)
            kcat[h, 0:tc, 0:MLA_NOPE] = knc_ref[:, ns]
            kcat[h, 0:tc, MLA_NOPE:MLA_NOPE + rd] = krc_ref[:, 0:rd]
            kcat[h, tc:tc + tl, 0:MLA_NOPE] = knl_ref[:, ns]
            kcat[h, tc:tc + tl, MLA_NOPE:MLA_NOPE + rd] = krl_ref[:, 0:rd]
            vext[0:tc, h * vw:h * vw + MLA_V] = vc_ref[:, h * MLA_V:(h + 1) * MLA_V]
            vext[tc:tc + tl, h * vw:h * vw + MLA_V] = vl_ref[:, h * MLA_V:(h + 1) * MLA_V]
            vext[:, h * vw + MLA_V:(h + 1) * vw] = jnp.ones((tc + tl, LANES), BF16)
        for h in range(nh):
            s = _dot_nt(qcat(qnc_ref, qrc_ref, h), kcat[h, 0:tc, :])
            p = jnp.exp2(s - jnp.max(s, axis=-1, keepdims=True))
            acc = _dot(p.astype(BF16), vext[0:tc, h * vw:(h + 1) * vw])
            oc_ref[:, h * MLA_V:(h + 1) * MLA_V] = finish(acc).astype(oc_ref.dtype)

    for h in range(nh):
        qc = qcat(qn_ref, qr_ref, h)
        s = _dot_nt(qc, kcat[h, 0:tc, :])
        m = jnp.max(s, axis=-1, keepdims=True)
        acc = _dot(jnp.exp2(s - m).astype(BF16), vext[0:tc, h * vw:(h + 1) * vw])
        for c0 in range(tc, tc + tl, MLA_TK):
            s = _dot_nt(qc, kcat[h, c0:c0 + MLA_TK, :])
            m_new = jnp.maximum(m, jnp.max(s, axis=-1, keepdims=True))
            acc = jnp.exp2(m - m_new) * acc + _dot(jnp.exp2(s - m_new).astype(BF16),
                                                   vext[c0:c0 + MLA_TK, h * vw:(h + 1) * vw])
            m = m_new
        o_ref[:, h * MLA_V:(h + 1) * MLA_V] = finish(acc).astype(o_ref.dtype)


def _mla_attention(qn, qr, kv, kr, b_sz, t_lat, t_ctx):
    hd = qn.shape[1]
    nh = MLA_HEADS_PER_STEP
    n_heads = hd // MLA_NOPE
    tq = MLA_TQ
    assert t_lat % tq == 0 and t_lat % MLA_TK == 0
    nlt = t_lat // tq
    ctx_blk0 = (b_sz * t_lat) // t_ctx
    vcol0 = n_heads // nh
    wn, wr = nh * MLA_NOPE, nh * ROPE_DIM
    return pl.pallas_call(
        functools.partial(_mla_attn_kernel, tc=t_ctx, tl=t_lat),
        grid=(b_sz, n_heads // nh, nlt),
        in_specs=[
            pl.BlockSpec((tq, wn), lambda b, h, t: (b * nlt + t, h)),
            pl.BlockSpec((tq, wr), lambda b, h, t: (b * nlt + t, h)),
            pl.BlockSpec((t_ctx, wn), lambda b, h, t: (ctx_blk0 + b, h)),
            pl.BlockSpec((t_ctx, wr), lambda b, h, t: (ctx_blk0 + b, h)),
            pl.BlockSpec((t_ctx, wn), lambda b, h, t: (ctx_blk0 + b, h)),
            pl.BlockSpec((t_ctx, wn), lambda b, h, t: (ctx_blk0 + b, vcol0 + h)),
            pl.BlockSpec((t_ctx, LANES), lambda b, h, t: (ctx_blk0 + b, 0)),
            pl.BlockSpec((t_lat, wn), lambda b, h, t: (b, h)),
            pl.BlockSpec((t_lat, wn), lambda b, h, t: (b, vcol0 + h)),
            pl.BlockSpec((t_lat, LANES), lambda b, h, t: (b, 0)),
        ],
        out_specs=[pl.BlockSpec((tq, nh * MLA_V), lambda b, h, t: (b * nlt + t, h)),
                   pl.BlockSpec((t_ctx, nh * MLA_V), lambda b, h, t: (b, h))],
        out_shape=[jax.ShapeDtypeStruct((b_sz * t_lat, n_heads * MLA_V), BF16),
                   jax.ShapeDtypeStruct((b_sz * t_ctx, n_heads * MLA_V), BF16)],
        scratch_shapes=[pltpu.VMEM((nh, t_ctx + t_lat, MLA_NOPE + ROPE_DIM), BF16),
                        pltpu.VMEM((t_ctx + t_lat, nh * (MLA_V + LANES)), BF16)],
        compiler_params=_cparams(("parallel", "parallel", "arbitrary")),
        name="mla_attention",
    )(qn, qr, qn, qr, kv, kv, kr, kv, kv, kr)


def _swa_attn_kernel(sink_ref, q_ref, kc_ref, vc_ref, kp_ref, kc0_ref, kn_ref, vp_ref, vc0_ref, vn_ref, o_ref,
                     *, nct, nb, n_kv):
    t = pl.program_id(1)
    n = t - nct
    blk = WINDOW
    hd = SWA_HEAD_DIM
    tc = kc_ref.shape[0]
    nk = tc + 3 * blk
    npair = SWA_GROUP // 2

    qi = lax.broadcasted_iota(jnp.int32, (blk, 3 * blk), 0)
    kj = lax.broadcasted_iota(jnp.int32, (blk, 3 * blk), 1)
    lo = jnp.where(n > 0, 0, blk)
    hi = jnp.where(t < nct, 0, jnp.where(n < nb - 1, 3 * blk, 2 * blk))
    win_ok = (jnp.abs(qi - kj + blk) <= WINDOW) & (kj >= lo) & (kj < hi)
    bias = jnp.concatenate([jnp.zeros((blk, tc), F32), jnp.where(win_ok, 0.0, NEG_BIG)], axis=1)
    bias = jnp.concatenate([bias] * npair, axis=0)
    lane = lax.broadcasted_iota(jnp.int32, (blk, 2 * hd), 1)
    zeros = jnp.zeros((nk, hd), BF16)
    ones = jnp.ones((nk, hd), BF16)

    def kv_chain(kvh):
        ks = slice(kvh * hd, (kvh + 1) * hd)
        k_all = jnp.concatenate([kc_ref[:, ks], kp_ref[:, ks], kc0_ref[:, ks], kn_ref[:, ks]], axis=0)
        v_all = jnp.concatenate([vc_ref[:, ks], vp_ref[:, ks], vc0_ref[:, ks], vn_ref[:, ks]], axis=0)
        k_bd = jnp.concatenate([jnp.concatenate([k_all, zeros], axis=1),
                                jnp.concatenate([zeros, k_all], axis=1)], axis=0)
        h0 = kvh * SWA_GROUP
        q_st = jnp.concatenate([q_ref[:, (h0 + 2 * j) * hd:(h0 + 2 * j + 2) * hd] for j in range(npair)], axis=0)
        s = _dot_nt(q_st, k_bd)
        yield
        ps, ms = [], []
        for e in range(2):
            sink_col = jnp.concatenate([jnp.full((blk, 1), sink_ref[h0 + 2 * j + e] * LOG2E, F32)
                                        for j in range(npair)], axis=0)
            s_e = s[:, e * nk:(e + 1) * nk] + bias
            m = jnp.maximum(jnp.max(s_e, axis=-1, keepdims=True), sink_col)
            p = jnp.exp2(s_e - m).astype(BF16)
            ps += [p[:, 0:tc], p[:, tc:]]
            ms.append(sink_col - m)

        def v_ext(r0, r1, e):
            v, z, o = v_all[r0:r1], zeros[r0:r1], ones[r0:r1]
            return jnp.concatenate([z, v, z, o] if e else [v, z, o, z], axis=1)

        tail = tc + 2 * blk
        acc = _dot(jnp.concatenate([ps[1][:, 2 * blk:], ps[3][:, 2 * blk:]], axis=1),
                   jnp.concatenate([v_ext(tail, nk, 0), v_ext(tail, nk, 1)], axis=0))
        for e in range(2):
            acc = acc + _dot(ps[2 * e], v_ext(0, tc, e)) + _dot(ps[2 * e + 1][:, 0:2 * blk], v_ext(tc, tail, e))
        for j in range(npair):
            r = slice(j * blk, (j + 1) * blk)
            sink_term = jnp.exp2(jnp.where(lane < hd, ms[0][r], ms[1][r]))
            o = acc[r, 0:2 * hd] / (acc[r, 2 * hd:4 * hd] + sink_term)
            o_ref[:, (h0 + 2 * j) * hd:(h0 + 2 * j + 2) * hd] = o.astype(o_ref.dtype)

    chains = [kv_chain(kvh) for kvh in range(n_kv)]
    next(chains[0])
    for kvh in range(n_kv):
        if kvh + 1 < n_kv:
            next(chains[kvh + 1])
        for _ in chains[kvh]:
            pass


def _swa_attention(qkv, sinks, b_sz, t_lat, t_ctx):
    n_tok = qkv.shape[0]
    blk = WINDOW
    hd = SWA_HEAD_DIM
    n_q = sinks.shape[0]
    n_kv = n_q // SWA_GROUP
    kvw = n_kv * hd
    kcol = (n_q * hd) // kvw
    vcol = kcol + 1
    nct, nb = t_ctx // blk, t_lat // blk
    ctx_blk0 = (b_sz * t_lat) // blk
    ctx_row0 = (b_sz * t_lat) // t_ctx

    def qrow(b, t):
        return jnp.where(t < nct, ctx_blk0 + b * nct + t, b * nb + t - nct)

    def krow(b, t, off):
        return b * nb + jnp.clip(t - nct + off, 0, nb - 1)

    kspec = lambda off: pl.BlockSpec((blk, kvw), lambda b, t: (krow(b, t, off), kcol))
    vspec = lambda off: pl.BlockSpec((blk, kvw), lambda b, t: (krow(b, t, off), vcol))
    return pl.pallas_call(
        functools.partial(_swa_attn_kernel, nct=nct, nb=nb, n_kv=n_kv),
        grid=(b_sz, nct + nb),
        in_specs=[
            pl.BlockSpec(memory_space=pltpu.SMEM),
            pl.BlockSpec((blk, n_q * hd), lambda b, t: (qrow(b, t), 0)),
            pl.BlockSpec((t_ctx, kvw), lambda b, t: (ctx_row0 + b, kcol)),
            pl.BlockSpec((t_ctx, kvw), lambda b, t: (ctx_row0 + b, vcol)),
            kspec(-1), kspec(0), kspec(1), vspec(-1), vspec(0), vspec(1),
        ],
        out_specs=pl.BlockSpec((blk, n_q * hd), lambda b, t: (qrow(b, t), 0)),
        out_shape=jax.ShapeDtypeStruct((n_tok, n_q * hd), BF16),
        compiler_params=_cparams(("parallel", "arbitrary")),
        name="swa_attention",
    )(sinks, qkv, qkv, qkv, qkv, qkv, qkv, qkv, qkv, qkv)


def kernel(x, c, ctx, c_ctx, w_ada, b_ada, g_norm, w_ffn_in, w_ffn_out, gla_w_in, gla_w_gate_down, gla_w_gate_up, gla_b_gate, gla_g_head, gla_w_out, mla_w_in, mla_g_q, mla_w_uq, mla_g_kv, mla_w_ukv, mla_w_out, swa_w_in, swa_sinks, swa_w_out):
    b_sz, t_lat, d = x.shape
    t_ctx = ctx.shape[1]
    depth = w_ada.shape[0]
    n_lat, n_ctx = b_sz * t_lat, b_sz * t_ctx
    n_tok = n_lat + n_ctx
    tm = ROW_TILE
    assert t_lat % tm == 0 and n_ctx % tm == 0 and b_sz < MOD_ROWS
    tpb = t_lat // tm
    n_lat_tiles = n_lat // tm

    hs = jnp.concatenate([x.reshape(n_lat, d), ctx.reshape(n_ctx, d)], axis=0)
    cond = jnp.zeros((MOD_ROWS, d), F32).at[:b_sz].set(c).at[b_sz].set(c_ctx)
    mod_all = _modulation(cond, w_ada, b_ada)
    cosf, sinf = _rope_tables(t_lat)
    common = dict(tiles_per_batch=tpb, nb=b_sz)
    tm_big = 2 * tm if t_lat % (2 * tm) == 0 and n_ctx % (2 * tm) == 0 else tm
    w_ffn_in_b, w_ffn_out_b = w_ffn_in.astype(BF16), w_ffn_out.astype(BF16)

    for i in range(depth):
        kind, j = i % 3, i // 3
        last = i == depth - 1
        n_out = n_lat if last else n_tok
        mod = mod_all[i].reshape(MOD_ROWS, 1, 6 * d)

        if kind == 0:
            rank = GLA_GATE_RANK
            qk_w = gla_w_gate_up.shape[-1]
            pad = LANES - 2 * rank
            w_gd = jnp.concatenate([gla_w_gate_down[j, 0], gla_w_gate_down[j, 1], jnp.zeros((d, pad), F32)],
                                   axis=1).astype(BF16)
            qkv, r, gd = _gla_in_proj(hs, g_norm[i, 0], mod, gla_w_in[j].astype(BF16), w_gd, 2 * qk_w + d, tm_big,
                                      t_lat // tm_big, b_sz)
            wu_ext = jnp.zeros((2, LANES, qk_w), F32)
            for dr in range(2):
                wu_ext = wu_ext.at[dr, dr * rank:(dr + 1) * rank].set(gla_w_gate_up[j, dr])
            o_f, o_b = _gla_scan(qkv, gd, wu_ext.astype(BF16), gla_b_gate[j].reshape(2, 1, qk_w), b_sz, t_lat, t_ctx)
            a = a2 = _gla_combine(o_f, o_b, r, gla_g_head[j], n_out)
            w_o = gla_w_out[j]
        elif kind == 1:
            n_heads = mla_w_out.shape[1] // MLA_V
            w_in = jnp.concatenate([mla_w_in[j], jnp.zeros((d, LANES - ROPE_DIM), F32)], axis=1).astype(BF16)
            w_uq = mla_w_uq[j].reshape(MLA_Q_RANK, n_heads, MLA_NOPE + ROPE_DIM)
            w_uq = jnp.concatenate([w_uq[:, :, :MLA_NOPE].reshape(MLA_Q_RANK, -1),
                                    w_uq[:, :, MLA_NOPE:].reshape(MLA_Q_RANK, -1)], axis=1).astype(BF16)
            w_ukv = mla_w_ukv[j].reshape(MLA_KV_RANK, n_heads, MLA_NOPE + MLA_V)
            w_ukv = jnp.concatenate([w_ukv[:, :, :MLA_NOPE].reshape(MLA_KV_RANK, -1),
                                     w_ukv[:, :, MLA_NOPE:].reshape(MLA_KV_RANK, -1)], axis=1).astype(BF16)
            q_scale = (MLA_NOPE + ROPE_DIM) ** -0.5 * LOG2E
            qn, qr, kv, kr = _mla_in_proj(hs, g_norm[i, 0], mod, w_in, mla_g_q[j], w_uq, mla_g_kv[j], w_ukv,
                                          n_heads * MLA_NOPE, q_scale, cosf, sinf, n_lat_tiles, **common)
            a, a2 = _mla_attention(qn, qr, kv, kr, b_sz, t_lat, t_ctx)
            w_o = mla_w_out[j]
        else:
            n_q = swa_sinks.shape[1]
            qkw = (n_q + n_q // SWA_GROUP) * SWA_HEAD_DIM
            qkv = _swa_in_proj(hs, g_norm[i, 0], mod, swa_w_in[j].astype(BF16), cosf, sinf, qkw // LANES,
                               (n_q * SWA_HEAD_DIM) // LANES, SWA_HEAD_DIM ** -0.5 * LOG2E, n_lat_tiles, **common)
            a = a2 = _swa_attention(qkv, swa_sinks[j], b_sz, t_lat, t_ctx)
            w_o = swa_w_out[j]

        hs = _proj_resid(a, a2, w_o.astype(BF16), hs, g_norm[i, 1], mod, 2, n_out, name="mixer_out_proj", **common)
        hs = _ffn(hs, g_norm[i, 2], g_norm[i, 3], mod, w_ffn_in_b, w_ffn_out_b, i, n_out, tm_big, t_lat // tm_big, b_sz,
                  name="ffn")

    return hs.reshape(b_sz, t_lat, d)
```

```python
import functools

import numpy as np
import jax
import jax.numpy as jnp
from jax import lax
from jax.experimental import pallas as pl
from jax.experimental.pallas import tpu as pltpu

F32 = jnp.float32
BF16 = jnp.bfloat16

EPS = 1e-6
ROPE_THETA = 10000.0
ROPE_DIM = 64
GRID_W = 64
WINDOW = 128
GLA_HEADS = 4
GLA_GATE_RANK = 16
GLA_TAU = 16.0
MLA_Q_RANK = 512
MLA_KV_RANK = 512
MLA_NOPE = 128
MLA_V = 128
SWA_HEAD_DIM = 64
SWA_GROUP = 8

LANES = 128
SUBLANES = 8
VMEM_LIMIT = 56 * 1024 * 1024
ROW_TILE = 512
MOD_ROWS = 16
NEG_BIG = -1e30
LOG2E = 1.4426950408889634


def _cparams(sem):
    return pltpu.CompilerParams(dimension_semantics=sem, vmem_limit_bytes=VMEM_LIMIT)


def _dot(a, b):
    return jnp.dot(a, b, preferred_element_type=F32)


def _dot_nt(a, b):
    return lax.dot_general(a, b, (((1,), (1,)), ((), ())), preferred_element_type=F32)


def _dot_tn(a, b):
    return lax.dot_general(a, b, (((0,), (0,)), ((), ())), preferred_element_type=F32)


def _sigmoid(x):
    return 1.0 / (1.0 + jnp.exp(-x))


def _rms(x):
    return x * lax.rsqrt(jnp.mean(x * x, axis=-1, keepdims=True) + EPS)


def _mod_kernel(c_ref, w_ref, b_ref, o_ref):
    c = c_ref[...]
    s = (c * _sigmoid(c)).astype(BF16)
    o_ref[0] = _dot(s, w_ref[0].astype(BF16)) + b_ref[0]


def _modulation(cond, w_ada, b_ada):
    depth, d, n = w_ada.shape
    tn = 1024
    return pl.pallas_call(
        _mod_kernel,
        grid=(depth, n // tn),
        in_specs=[
            pl.BlockSpec((MOD_ROWS, d), lambda l, j: (0, 0)),
            pl.BlockSpec((1, d, tn), lambda l, j: (l, 0, j)),
            pl.BlockSpec((1, 1, tn), lambda l, j: (l, 0, j)),
        ],
        out_specs=pl.BlockSpec((1, MOD_ROWS, tn), lambda l, j: (l, 0, j)),
        out_shape=jax.ShapeDtypeStruct((depth, MOD_ROWS, n), F32),
        compiler_params=_cparams(("parallel", "parallel")),
        name="modulation",
    )(cond, w_ada, b_ada.reshape(depth, 1, n))


NORM_ROWS = 16


def _adaln_rows(dst_ref, x_ref, g_ref, mod_ref, shift_idx, scale_idx, d):
    m = mod_ref[0]
    gs = g_ref[...] * (1.0 + m[:, scale_idx * d:(scale_idx + 1) * d])
    sh = m[:, shift_idx * d:(shift_idx + 1) * d]
    for r in range(0, x_ref.shape[0], NORM_ROWS):
        rows = pl.ds(r, NORM_ROWS)
        dst_ref[rows, :] = (_rms(x_ref[rows, :]) * gs + sh).astype(dst_ref.dtype)


def _resid_norm_rows(o_ref, h_ref, y_ref, g_ref, mod_ref, gate_idx, d):
    gg = mod_ref[0][:, gate_idx * d:(gate_idx + 1) * d] * g_ref[...]
    for r in range(0, h_ref.shape[0], NORM_ROWS):
        rows = pl.ds(r, NORM_ROWS)
        o_ref[rows, :] = h_ref[rows, :] + _rms(y_ref[rows, :]) * gg


def _rope_chunk(y, cosf, sinf, first, is_lat):
    partner = jnp.where(first, pltpu.roll(y, LANES - ROPE_DIM // 2, 1), pltpu.roll(y, ROPE_DIM // 2, 1))
    return jnp.where(is_lat, y * cosf + partner * sinf, y)


def _first_half_lanes(shape):
    lane = lax.broadcasted_iota(jnp.int32, shape, 1)
    return (lane % ROPE_DIM) < (ROPE_DIM // 2)


def _mod_spec(mod, tiles_per_batch, nb):
    return pl.BlockSpec((1, 1, mod.shape[-1]), lambda i, *_: (jnp.minimum(i // tiles_per_batch, nb), 0, 0))


def _gla_in_proj_kernel(x_ref, g_ref, mod_ref, w_ref, wgd_ref, o_ref, gd_ref, a_scr, *, d):
    @pl.when(pl.program_id(1) == 0)
    def _():
        _adaln_rows(a_scr, x_ref, g_ref, mod_ref, 0, 1, d)
        gd_ref[...] = _dot(a_scr[...], wgd_ref[...])

    o_ref[...] = _dot(a_scr[...], w_ref[...]).astype(o_ref.dtype)


def _gla_in_proj(hs, g, mod, w, w_gd, tm, tiles_per_batch, nb):
    n_rows, d = hs.shape
    n = w.shape[1]
    tn = 1024
    n_side = w_gd.shape[1]
    return pl.pallas_call(
        functools.partial(_gla_in_proj_kernel, d=d),
        grid=(n_rows // tm, n // tn),
        in_specs=[
            pl.BlockSpec((tm, d), lambda i, j: (i, 0)),
            pl.BlockSpec((1, d), lambda i, j: (0, 0)),
            _mod_spec(mod, tiles_per_batch, nb),
            pl.BlockSpec((d, tn), lambda i, j: (0, j)),
            pl.BlockSpec((d, n_side), lambda i, j: (0, 0)),
        ],
        out_specs=[pl.BlockSpec((tm, tn), lambda i, j: (i, j)),
                   pl.BlockSpec((tm, n_side), lambda i, j: (i, 0))],
        out_shape=[jax.ShapeDtypeStruct((n_rows, n), BF16),
                   jax.ShapeDtypeStruct((n_rows, n_side), F32)],
        scratch_shapes=[pltpu.VMEM((tm, d), BF16)],
        compiler_params=_cparams(("parallel", "arbitrary")),
        name="gla_in_proj",
    )(hs, g.reshape(1, d), mod, w, w_gd)


def _swa_in_proj_kernel(x_ref, g_ref, mod_ref, cos_ref, sin_ref, w_ref, o_ref, a_scr, *, d, n_lat_tiles, n_roped,
                        n_scaled, out_scale):
    _adaln_rows(a_scr, x_ref, g_ref, mod_ref, 0, 1, d)
    y_all = _dot(a_scr[...], w_ref[...])
    is_lat = pl.program_id(0) < n_lat_tiles
    cosf, sinf = cos_ref[...], sin_ref[...]
    first = _first_half_lanes(cosf.shape)
    for c in range(y_all.shape[1] // LANES):
        y = y_all[:, c * LANES:(c + 1) * LANES]
        if c < n_roped:
            y = _rope_chunk(y, cosf, sinf, first, is_lat)
        if c < n_scaled:
            y = y * out_scale
        o_ref[:, c * LANES:(c + 1) * LANES] = y.astype(o_ref.dtype)


def _swa_in_proj(hs, g, mod, w, cosf, sinf, n_roped, n_scaled, out_scale, n_lat_tiles, tiles_per_batch, nb):
    n_rows, d = hs.shape
    n = w.shape[1]
    tm = ROW_TILE
    return pl.pallas_call(
        functools.partial(_swa_in_proj_kernel, d=d, n_lat_tiles=n_lat_tiles, n_roped=n_roped, n_scaled=n_scaled,
                          out_scale=out_scale),
        grid=(n_rows // tm,),
        in_specs=[
            pl.BlockSpec((tm, d), lambda i: (i, 0)),
            pl.BlockSpec((1, d), lambda i: (0, 0)),
            _mod_spec(mod, tiles_per_batch, nb),
            pl.BlockSpec((tm, LANES), lambda i: (i % tiles_per_batch, 0)),
            pl.BlockSpec((tm, LANES), lambda i: (i % tiles_per_batch, 0)),
            pl.BlockSpec((d, n), lambda i: (0, 0)),
        ],
        out_specs=pl.BlockSpec((tm, n), lambda i: (i, 0)),
        out_shape=jax.ShapeDtypeStruct((n_rows, n), BF16),
        scratch_shapes=[pltpu.VMEM((tm, d), BF16)],
        compiler_params=_cparams(("parallel",)),
        name="swa_in_proj",
    )(hs, g.reshape(1, d), mod, cosf, sinf, w)


def _mla_in_proj_kernel(x_ref, g_ref, mod_ref, cos_ref, sin_ref, win_ref, gq_ref, gkv_ref, wuq_ref, wukv_ref,
                        qn_ref, qr_ref, kv_ref, kr_ref, a_scr, *, d, n_lat_tiles, q_scale):
    _adaln_rows(a_scr, x_ref, g_ref, mod_ref, 0, 1, d)
    p1 = _dot(a_scr[...], win_ref[...])
    cq = (_rms(p1[:, 0:MLA_Q_RANK]) * gq_ref[...]).astype(BF16)
    ckv = (_rms(p1[:, MLA_Q_RANK:MLA_Q_RANK + MLA_KV_RANK]) * gkv_ref[...]).astype(BF16)
    q = _dot(cq, wuq_ref[...]) * q_scale
    n_nope = qn_ref.shape[1]
    qn_ref[...] = q[:, 0:n_nope].astype(qn_ref.dtype)
    kv_ref[...] = _dot(ckv, wukv_ref[...]).astype(kv_ref.dtype)
    is_lat = pl.program_id(0) < n_lat_tiles
    cosf, sinf = cos_ref[...], sin_ref[...]
    first = _first_half_lanes(cosf.shape)
    for c in range(qr_ref.shape[1] // LANES):
        y = q[:, n_nope + c * LANES:n_nope + (c + 1) * LANES]
        qr_ref[:, c * LANES:(c + 1) * LANES] = _rope_chunk(y, cosf, sinf, first, is_lat).astype(qr_ref.dtype)
    k_rope = p1[:, MLA_Q_RANK + MLA_KV_RANK:MLA_Q_RANK + MLA_KV_RANK + LANES]
    kr_ref[...] = _rope_chunk(k_rope, cosf, sinf, first, is_lat).astype(kr_ref.dtype)


def _mla_in_proj(hs, g, mod, w_in, g_q, w_uq, g_kv, w_ukv, n_nope, q_scale, cosf, sinf, n_lat_tiles, tiles_per_batch,
                 nb):
    n_rows, d = hs.shape
    tm = ROW_TILE
    n_q, n_kv = w_uq.shape[1], w_ukv.shape[1]
    const = lambda shape: pl.BlockSpec(shape, lambda i: (0, 0))
    row = lambda w: pl.BlockSpec((tm, w), lambda i: (i, 0))
    return pl.pallas_call(
        functools.partial(_mla_in_proj_kernel, d=d, n_lat_tiles=n_lat_tiles, q_scale=q_scale),
        grid=(n_rows // tm,),
        in_specs=[
            row(d), const((1, d)), _mod_spec(mod, tiles_per_batch, nb),
            pl.BlockSpec((tm, LANES), lambda i: (i % tiles_per_batch, 0)),
            pl.BlockSpec((tm, LANES), lambda i: (i % tiles_per_batch, 0)),
            const(w_in.shape), const((1, MLA_Q_RANK)), const((1, MLA_KV_RANK)), const(w_uq.shape), const(w_ukv.shape),
        ],
        out_specs=[row(n_nope), row(n_q - n_nope), row(n_kv), row(LANES)],
        out_shape=[jax.ShapeDtypeStruct((n_rows, n_nope), BF16), jax.ShapeDtypeStruct((n_rows, n_q - n_nope), BF16),
                   jax.ShapeDtypeStruct((n_rows, n_kv), BF16), jax.ShapeDtypeStruct((n_rows, LANES), BF16)],
        scratch_shapes=[pltpu.VMEM((tm, d), BF16)],
        compiler_params=_cparams(("parallel",)),
        name="mla_in_proj",
    )(hs, g.reshape(1, d), mod, cosf, sinf, w_in, g_q.reshape(1, -1), g_kv.reshape(1, -1), w_uq, w_ukv)


def _proj_resid_kernel(a_ref, a2_ref, w_ref, h_ref, g_ref, mod_ref, o_ref, y_scr, *, gate_idx, d, n1):
    def body(src_ref):
        y_scr[...] = _dot(src_ref[...].astype(BF16), w_ref[...])
        _resid_norm_rows(o_ref, h_ref, y_scr, g_ref, mod_ref, gate_idx, d)

    pl.when(pl.program_id(0) < n1)(lambda: body(a_ref))
    pl.when(pl.program_id(0) >= n1)(lambda: body(a2_ref))


def _proj_resid(a, a2, w, hs, g, mod, gate_idx, n_rows, tiles_per_batch, nb, name):
    k, d = w.shape
    tm = ROW_TILE
    n1 = a.shape[0] // tm
    return pl.pallas_call(
        functools.partial(_proj_resid_kernel, gate_idx=gate_idx, d=d, n1=n1),
        grid=(n_rows // tm,),
        in_specs=[
            pl.BlockSpec((tm, k), lambda i: (jnp.minimum(i, n1 - 1), 0)),
            pl.BlockSpec((tm, k), lambda i: (jnp.maximum(i - n1, 0), 0)),
            pl.BlockSpec((k, d), lambda i: (0, 0)),
            pl.BlockSpec((tm, d), lambda i: (i, 0)),
            pl.BlockSpec((1, d), lambda i: (0, 0)),
            _mod_spec(mod, tiles_per_batch, nb),
        ],
        out_specs=pl.BlockSpec((tm, d), lambda i: (i, 0)),
        out_shape=jax.ShapeDtypeStruct((n_rows, d), F32),
        scratch_shapes=[pltpu.VMEM((tm, d), F32)],
        compiler_params=_cparams(("parallel",)),
        name=name,
    )(a, a2, w, hs, g.reshape(1, d), mod)


def _ffn_kernel(h_ref, g2_ref, g3_ref, mod_ref, wg_ref, wu_ref, wo_ref, o_ref, a_scr, *, d):
    j = pl.program_id(1)

    @pl.when(j == 0)
    def _():
        _adaln_rows(a_scr, h_ref, g2_ref, mod_ref, 3, 4, d)
        o_ref[...] = jnp.zeros_like(o_ref)

    a = a_scr[...]
    gt = _dot(a, wg_ref[...])
    up = _dot(a, wu_ref[...])
    act = (gt * _sigmoid(gt) * up).astype(BF16)
    o_ref[...] += _dot(act, wo_ref[...])

    @pl.when(j == pl.num_programs(1) - 1)
    def _():
        _resid_norm_rows(o_ref, h_ref, o_ref, g3_ref, mod_ref, 5, d)


def _ffn(hs, g2, g3, mod, w_in, w_out, layer, n_rows, tm, tiles_per_batch, nb, name):
    d = hs.shape[1]
    f = w_out.shape[1]
    tf = 512
    nf = f // tf
    return pl.pallas_call(
        functools.partial(_ffn_kernel, d=d),
        grid=(n_rows // tm, nf),
        in_specs=[
            pl.BlockSpec((tm, d), lambda i, j: (i, 0)),
            pl.BlockSpec((1, d), lambda i, j: (0, 0)),
            pl.BlockSpec((1, d), lambda i, j: (0, 0)),
            _mod_spec(mod, tiles_per_batch, nb),
            pl.BlockSpec((None, d, tf), lambda i, j: (layer, 0, j)),
            pl.BlockSpec((None, d, tf), lambda i, j: (layer, 0, nf + j)),
            pl.BlockSpec((None, tf, d), lambda i, j: (layer, j, 0)),
        ],
        out_specs=pl.BlockSpec((tm, d), lambda i, j: (i, 0)),
        out_shape=jax.ShapeDtypeStruct((n_rows, d), F32),
        scratch_shapes=[pltpu.VMEM((tm, d), BF16)],
        compiler_params=_cparams(("parallel", "arbitrary")),
        name=name,
    )(hs, g2.reshape(1, d), g3.reshape(1, d), mod, w_in, w_in, w_out)


GLA_CHUNK = 128


def _gla_structure(cs, reverse):
    idx = np.arange(cs)
    ip = cs - 1 - idx if reverse else idx
    ii, jj = ip[:, None], ip[None, :]
    masks, dist = [], []
    s = cs // 2
    while s >= 1:
        masks.append(((ii // (2 * s)) == (jj // (2 * s))) & ((ii & s) != 0) & ((jj & s) == 0))
        pp = (ii // (2 * s)) * (2 * s) + s - 1
        dist.append((jj > np.minimum(ii, pp)) & (jj <= np.maximum(ii, pp)))
        s //= 2
    masks.append(ii == jj)
    return (np.stack(masks).astype(np.float32), (jj <= ii).astype(np.float32),
            np.concatenate(dist, axis=0).astype(np.float32))


def _gla_chunk(q_b, k_b, v_b, lg, msk_ref, tri, dist_ref, st_ref, *, cs, reverse):
    hi = lg.astype(BF16)
    lo = (lg - hi.astype(F32)).astype(BF16)
    b = _dot(tri, hi) + _dot(tri, lo)
    yield
    last = 0 if reverse else cs - 1
    b_last = b[last:last + 1, :]

    st = st_ref[...]
    o = _dot_nt(q_b * jnp.exp2(b).astype(BF16), st.astype(BF16))
    yield

    attn = jnp.zeros((cs, cs), F32)
    nlev = dist_ref.shape[0] // cs
    for lvl in range(nlev):
        nd = _dot(dist_ref[lvl * cs:(lvl + 1) * cs, :], hi)
        fac = jnp.exp2(nd).astype(BF16)
        attn = attn + msk_ref[lvl] * _dot_nt(q_b * fac, k_b * fac)
        yield
    attn = attn + msk_ref[nlev] * _dot_nt(q_b, k_b)

    k_dec = k_b * jnp.exp2(b_last - b).astype(BF16)
    st_ref[...] = st * jnp.exp2(b_last) + _dot_tn(v_b, k_dec)
    yield
    return o + _dot(attn.astype(BF16), v_b)


def _interleave(gens):
    results = [None] * len(gens)
    active = list(range(len(gens)))
    while active:
        for idx in list(active):
            try:
                next(gens[idx])
            except StopIteration as stop:
                results[idx] = stop.value
                active.remove(idx)
    return results


def _gla_scan_kernel(qf_ref, kf_ref, vf_ref, gf_ref, qb_ref, kb_ref, vb_ref, gb_ref, wu_ref, bg_ref,
                     msk_ref, tri_ref, dist_ref, of_ref, ob_ref, st_scr, *, cs, dk, dv):
    @pl.when(pl.program_id(1) == 0)
    def _():
        st_scr[...] = jnp.zeros_like(st_scr)

    dirs = ((qf_ref, kf_ref, vf_ref, gf_ref, of_ref), (qb_ref, kb_ref, vb_ref, gb_ref, ob_ref))
    chains, dests = [], []
    for dr, (q_ref, k_ref, v_ref, gd_ref, o_ref) in enumerate(dirs):
        z = _dot(gd_ref[...].astype(BF16), wu_ref[dr]) + bg_ref[dr]
        lg_all = (jnp.minimum(z, 0.0) - jnp.log1p(jnp.exp(-jnp.abs(z)))) * (LOG2E / GLA_TAU)
        for h in range(GLA_HEADS):
            ks = slice(h * dk, (h + 1) * dk)
            vs = slice(h * dv, (h + 1) * dv)
            chains.append(_gla_chunk(q_ref[:, ks] * (dk ** -0.5), k_ref[:, ks], v_ref[:, vs].astype(BF16),
                                     lg_all[:, ks], msk_ref.at[dr], tri_ref[dr], dist_ref.at[dr], st_scr.at[dr, h],
                                     cs=cs, reverse=bool(dr)))
            dests.append((o_ref, vs))
    for (o_ref, vs), o in zip(dests, _interleave(chains)):
        o_ref[:, vs] = o.astype(o_ref.dtype)


def _gla_scan(p, gd, wu_ext, bg, b_sz, t_lat, t_ctx):
    cs = GLA_CHUNK
    qkw = wu_ext.shape[-1]
    dk = qkw // GLA_HEADS
    vw = 2 * qkw
    dv = vw // GLA_HEADS
    n_tok = p.shape[0]
    ncc, ncl = t_ctx // cs, t_lat // cs
    ctx_blk0 = (b_sz * t_lat) // cs
    structs = [_gla_structure(cs, rev) for rev in (False, True)]
    masks, tri, dist = (np.stack(parts) for parts in zip(*structs))

    def rowblk(rev):
        def f(b, s):
            if rev:
                cc, lc = ncc - 1 - s, ncl - 1 - (s - ncc)
            else:
                cc, lc = s, s - ncc
            return jnp.where(s < ncc, ctx_blk0 + b * ncc + cc, b * ncl + lc)
        return f

    def dir_specs(rev):
        rb = rowblk(rev)
        return [pl.BlockSpec((cs, qkw), lambda b, s: (rb(b, s), 0)),
                pl.BlockSpec((cs, qkw), lambda b, s: (rb(b, s), 1)),
                pl.BlockSpec((cs, vw), lambda b, s: (rb(b, s), 1)),
                pl.BlockSpec((cs, LANES), lambda b, s: (rb(b, s), 0))]

    return pl.pallas_call(
        functools.partial(_gla_scan_kernel, cs=cs, dk=dk, dv=dv),
        grid=(b_sz, ncc + ncl),
        in_specs=dir_specs(False) + dir_specs(True) + [
            pl.BlockSpec((2, LANES, qkw), lambda b, s: (0, 0, 0)),
            pl.BlockSpec((2, 1, qkw), lambda b, s: (0, 0, 0)),
            pl.BlockSpec(masks.shape, lambda b, s: (0, 0, 0, 0)),
            pl.BlockSpec(tri.shape, lambda b, s: (0, 0, 0)),
            pl.BlockSpec(dist.shape, lambda b, s: (0, 0, 0)),
        ],
        out_specs=[pl.BlockSpec((cs, vw), lambda b, s: (rowblk(False)(b, s), 0)),
                   pl.BlockSpec((cs, vw), lambda b, s: (rowblk(True)(b, s), 0))],
        out_shape=[jax.ShapeDtypeStruct((n_tok, vw), BF16)] * 2,
        scratch_shapes=[pltpu.VMEM((2, GLA_HEADS, dv, dk), F32)],
        compiler_params=_cparams(("parallel", "arbitrary")),
        name="gla_scan",
    )(p, p, p, gd, p, p, p, gd, wu_ext, bg, jnp.asarray(masks), jnp.asarray(tri, dtype=BF16),
      jnp.asarray(dist, dtype=BF16))


def _gla_combine_kernel(of_ref, ob_ref, r_ref, gh_ref, o_ref, *, dv):
    g = gh_ref[...]
    for r0 in range(0, o_ref.shape[0], NORM_ROWS):
        rows = pl.ds(r0, NORM_ROWS)
        for h in range(GLA_HEADS):
            sl = slice(h * dv, (h + 1) * dv)
            y = _rms(of_ref[rows, sl].astype(F32) + ob_ref[rows, sl].astype(F32)) * g
            r = r_ref[rows, sl].astype(F32)
            o_ref[rows, sl] = (y * (r * _sigmoid(r))).astype(o_ref.dtype)


def _gla_combine(o_f, o_b, p, g_head, n_rows):
    d = o_f.shape[1]
    dv = d // GLA_HEADS
    tm = ROW_TILE
    rcol = p.shape[1] // d - 1
    return pl.pallas_call(
        functools.partial(_gla_combine_kernel, dv=dv),
        grid=(n_rows // tm,),
        in_specs=[
            pl.BlockSpec((tm, d), lambda i: (i, 0)),
            pl.BlockSpec((tm, d), lambda i: (i, 0)),
            pl.BlockSpec((tm, d), lambda i: (i, rcol)),
            pl.BlockSpec((1, dv), lambda i: (0, 0)),
        ],
        out_specs=pl.BlockSpec((tm, d), lambda i: (i, 0)),
        out_shape=jax.ShapeDtypeStruct((n_rows, d), BF16),
        compiler_params=_cparams(("parallel",)),
        name="gla_combine",
    )(o_f, o_b, p, g_head.reshape(1, dv))


def _rope_tables(t_lat):
    t = jnp.arange(t_lat)
    row = (t // GRID_W).astype(F32)
    col = (t % GRID_W).astype(F32)
    n_freq = ROPE_DIM // 4
    inv_freq = ROPE_THETA ** (-jnp.arange(n_freq, dtype=F32) / n_freq)
    ang = jnp.concatenate([row[:, None] * inv_freq, col[:, None] * inv_freq], axis=-1)
    cos, sin = jnp.cos(ang), jnp.sin(ang)
    reps = LANES // ROPE_DIM
    return jnp.tile(jnp.concatenate([cos, cos], axis=-1), (1, reps)), jnp.tile(jnp.concatenate([-sin, sin], axis=-1), (1, reps))


MLA_TQ = 512
MLA_TK = 1024
MLA_HEADS_PER_STEP = 2


def _mla_attn_kernel(qn_ref, qr_ref, qnc_ref, qrc_ref, knc_ref, vc_ref, krc_ref, knl_ref, vl_ref, krl_ref,
                     o_ref, oc_ref, kcat, vext, *, tc, tl):
    t = pl.program_id(2)
    nh = MLA_HEADS_PER_STEP
    rd = ROPE_DIM
    vw = MLA_V + LANES

    def qcat(n_ref, r_ref, h):
        return jnp.concatenate([n_ref[:, h * MLA_NOPE:(h + 1) * MLA_NOPE], r_ref[:, h * rd:(h + 1) * rd]], axis=1)

    def finish(acc):
        return acc[:, 0:MLA_V] / acc[:, MLA_V:vw]

    @pl.when(t == 0)
    def _():
        for h in range(nh):
            ns = slice(h * MLA_NOPE, (h + 1) * MLA_NOPE)
            kcat[h, 0:tc, 0:MLA_NOPE] = knc_ref[:, ns]
            kcat[h, 0:tc, MLA_NOPE:MLA_NOPE + rd] = krc_ref[:, 0:rd]
            kcat[h, tc:tc + tl, 0:MLA_NOPE] = knl_ref[:, ns]
            kcat[h, tc:tc + tl, MLA_NOPE:MLA_NOPE + rd] = krl_ref[:, 0:rd]
            vext[0:tc, h * vw:h * vw + MLA_V] = vc_ref[:, h * MLA_V:(h + 1) * MLA_V]
            vext[tc:tc + tl, h * vw:h * vw + MLA_V] = vl_ref[:, h * MLA_V:(h + 1) * MLA_V]
            vext[:, h * vw + MLA_V:(h + 1) * vw] = jnp.ones((tc + tl, LANES), BF16)
        for h in range(nh):
            s = _dot_nt(qcat(qnc_ref, qrc_ref, h), kcat[h, 0:tc, :])
            p = jnp.exp2(s - jnp.max(s, axis=-1, keepdims=True))
            acc = _dot(p.astype(BF16), vext[0:tc, h * vw:(h + 1) * vw])
            oc_ref[:, h * MLA_V:(h + 1) * MLA_V] = finish(acc).astype(oc_ref.dtype)

    for h in range(nh):
        qc = qcat(qn_ref, qr_ref, h)
        s = _dot_nt(qc, kcat[h, 0:tc, :])
        m = jnp.max(s, axis=-1, keepdims=True)
        acc = _dot(jnp.exp2(s - m).astype(BF16), vext[0:tc, h * vw:(h + 1) * vw])
        for c0 in range(tc, tc + tl, MLA_TK):
            s = _dot_nt(qc, kcat[h, c0:c0 + MLA_TK, :])
            m_new = jnp.maximum(m, jnp.max(s, axis=-1, keepdims=True))
            acc = jnp.exp2(m - m_new) * acc + _dot(jnp.exp2(s - m_new).astype(BF16),
                                                   vext[c0:c0 + MLA_TK, h * vw:(h + 1) * vw])
            m = m_new
        o_ref[:, h * MLA_V:(h + 1) * MLA_V] = finish(acc).astype(o_ref.dtype)


def _mla_attention(qn, qr, kv, kr, b_sz, t_lat, t_ctx):
    hd = qn.shape[1]
    nh = MLA_HEADS_PER_STEP
    n_heads = hd // MLA_NOPE
    tq = MLA_TQ
    assert t_lat % tq == 0 and t_lat % MLA_TK == 0
    nlt = t_lat // tq
    ctx_blk0 = (b_sz * t_lat) // t_ctx
    vcol0 = n_heads // nh
    wn, wr = nh * MLA_NOPE, nh * ROPE_DIM
    return pl.pallas_call(
        functools.partial(_mla_attn_kernel, tc=t_ctx, tl=t_lat),
        grid=(b_sz, n_heads // nh, nlt),
        in_specs=[
            pl.BlockSpec((tq, wn), lambda b, h, t: (b * nlt + t, h)),
            pl.BlockSpec((tq, wr), lambda b, h, t: (b * nlt + t, h)),
            pl.BlockSpec((t_ctx, wn), lambda b, h, t: (ctx_blk0 + b, h)),
            pl.BlockSpec((t_ctx, wr), lambda b, h, t: (ctx_blk0 + b, h)),
            pl.BlockSpec((t_ctx, wn), lambda b, h, t: (ctx_blk0 + b, h)),
            pl.BlockSpec((t_ctx, wn), lambda b, h, t: (ctx_blk0 + b, vcol0 + h)),
            pl.BlockSpec((t_ctx, LANES), lambda b, h, t: (ctx_blk0 + b, 0)),
            pl.BlockSpec((t_lat, wn), lambda b, h, t: (b, h)),
            pl.BlockSpec((t_lat, wn), lambda b, h, t: (b, vcol0 + h)),
            pl.BlockSpec((t_lat, LANES), lambda b, h, t: (b, 0)),
        ],
        out_specs=[pl.BlockSpec((tq, nh * MLA_V), lambda b, h, t: (b * nlt + t, h)),
                   pl.BlockSpec((t_ctx, nh * MLA_V), lambda b, h, t: (b, h))],
        out_shape=[jax.ShapeDtypeStruct((b_sz * t_lat, n_heads * MLA_V), BF16),
                   jax.ShapeDtypeStruct((b_sz * t_ctx, n_heads * MLA_V), BF16)],
        scratch_shapes=[pltpu.VMEM((nh, t_ctx + t_lat, MLA_NOPE + ROPE_DIM), BF16),
                        pltpu.VMEM((t_ctx + t_lat, nh * (MLA_V + LANES)), BF16)],
        compiler_params=_cparams(("parallel", "parallel", "arbitrary")),
        name="mla_attention",
    )(qn, qr, qn, qr, kv, kv, kr, kv, kv, kr)


def _swa_attn_kernel(sink_ref, q_ref, kc_ref, vc_ref, kp_ref, kc0_ref, kn_ref, vp_ref, vc0_ref, vn_ref, o_ref,
                     *, nct, nb, n_kv):
    t = pl.program_id(1)
    n = t - nct
    blk = WINDOW
    hd = SWA_HEAD_DIM
    tc = kc_ref.shape[0]
    nk = tc + 3 * blk
    npair = SWA_GROUP // 2

    qi = lax.broadcasted_iota(jnp.int32, (blk, 3 * blk), 0)
    kj = lax.broadcasted_iota(jnp.int32, (blk, 3 * blk), 1)
    lo = jnp.where(n > 0, 0, blk)
    hi = jnp.where(t < nct, 0, jnp.where(n < nb - 1, 3 * blk, 2 * blk))
    win_ok = (jnp.abs(qi - kj + blk) <= WINDOW) & (kj >= lo) & (kj < hi)
    bias = jnp.concatenate([jnp.zeros((blk, tc), F32), jnp.where(win_ok, 0.0, NEG_BIG)], axis=1)
    bias = jnp.concatenate([bias] * npair, axis=0)
    lane = lax.broadcasted_iota(jnp.int32, (blk, 2 * hd), 1)
    zeros = jnp.zeros((nk, hd), BF16)
    ones = jnp.ones((nk, hd), BF16)

    def kv_chain(kvh):
        ks = slice(kvh * hd, (kvh + 1) * hd)
        k_all = jnp.concatenate([kc_ref[:, ks], kp_ref[:, ks], kc0_ref[:, ks], kn_ref[:, ks]], axis=0)
        v_all = jnp.concatenate([vc_ref[:, ks], vp_ref[:, ks], vc0_ref[:, ks], vn_ref[:, ks]], axis=0)
        k_bd = jnp.concatenate([jnp.concatenate([k_all, zeros], axis=1),
                                jnp.concatenate([zeros, k_all], axis=1)], axis=0)
        h0 = kvh * SWA_GROUP
        q_st = jnp.concatenate([q_ref[:, (h0 + 2 * j) * hd:(h0 + 2 * j + 2) * hd] for j in range(npair)], axis=0)
        s = _dot_nt(q_st, k_bd)
        yield
        ps, ms = [], []
        for e in range(2):
            sink_col = jnp.concatenate([jnp.full((blk, 1), sink_ref[h0 + 2 * j + e] * LOG2E, F32)
                                        for j in range(npair)], axis=0)
            s_e = s[:, e * nk:(e + 1) * nk] + bias
            m = jnp.maximum(jnp.max(s_e, axis=-1, keepdims=True), sink_col)
            p = jnp.exp2(s_e - m).astype(BF16)
            ps += [p[:, 0:tc], p[:, tc:]]
            ms.append(sink_col - m)

        def v_ext(r0, r1, e):
            v, z, o = v_all[r0:r1], zeros[r0:r1], ones[r0:r1]
            return jnp.concatenate([z, v, z, o] if e else [v, z, o, z], axis=1)

        tail = tc + 2 * blk
        acc = _dot(jnp.concatenate([ps[1][:, 2 * blk:], ps[3][:, 2 * blk:]], axis=1),
                   jnp.concatenate([v_ext(tail, nk, 0), v_ext(tail, nk, 1)], axis=0))
        for e in range(2):
            acc = acc + _dot(ps[2 * e], v_ext(0, tc, e)) + _dot(ps[2 * e + 1][:, 0:2 * blk], v_ext(tc, tail, e))
        for j in range(npair):
            r = slice(j * blk, (j + 1) * blk)
            sink_term = jnp.exp2(jnp.where(lane < hd, ms[0][r], ms[1][r]))
            o = acc[r, 0:2 * hd] / (acc[r, 2 * hd:4 * hd] + sink_term)
            o_ref[:, (h0 + 2 * j) * hd:(h0 + 2 * j + 2) * hd] = o.astype(o_ref.dtype)

    chains = [kv_chain(kvh) for kvh in range(n_kv)]
    next(chains[0])
    for kvh in range(n_kv):
        if kvh + 1 < n_kv:
            next(chains[kvh + 1])
        for _ in chains[kvh]:
            pass


def _swa_attention(qkv, sinks, b_sz, t_lat, t_ctx):
    n_tok = qkv.shape[0]
    blk = WINDOW
    hd = SWA_HEAD_DIM
    n_q = sinks.shape[0]
    n_kv = n_q // SWA_GROUP
    kvw = n_kv * hd
    kcol = (n_q * hd) // kvw
    vcol = kcol + 1
    nct, nb = t_ctx // blk, t_lat // blk
    ctx_blk0 = (b_sz * t_lat) // blk
    ctx_row0 = (b_sz * t_lat) // t_ctx

    def qrow(b, t):
        return jnp.where(t < nct, ctx_blk0 + b * nct + t, b * nb + t - nct)

    def krow(b, t, off):
        return b * nb + jnp.clip(t - nct + off, 0, nb - 1)

    kspec = lambda off: pl.BlockSpec((blk, kvw), lambda b, t: (krow(b, t, off), kcol))
    vspec = lambda off: pl.BlockSpec((blk, kvw), lambda b, t: (krow(b, t, off), vcol))
    return pl.pallas_call(
        functools.partial(_swa_attn_kernel, nct=nct, nb=nb, n_kv=n_kv),
        grid=(b_sz, nct + nb),
        in_specs=[
            pl.BlockSpec(memory_space=pltpu.SMEM),
            pl.BlockSpec((blk, n_q * hd), lambda b, t: (qrow(b, t), 0)),
            pl.BlockSpec((t_ctx, kvw), lambda b, t: (ctx_row0 + b, kcol)),
            pl.BlockSpec((t_ctx, kvw), lambda b, t: (ctx_row0 + b, vcol)),
            kspec(-1), kspec(0), kspec(1), vspec(-1), vspec(0), vspec(1),
        ],
        out_specs=pl.BlockSpec((blk, n_q * hd), lambda b, t: (qrow(b, t), 0)),
        out_shape=jax.ShapeDtypeStruct((n_tok, n_q * hd), BF16),
        compiler_params=_cparams(("parallel", "arbitrary")),
        name="swa_attention",
    )(sinks, qkv, qkv, qkv, qkv, qkv, qkv, qkv, qkv, qkv)


def kernel(x, c, ctx, c_ctx, w_ada, b_ada, g_norm, w_ffn_in, w_ffn_out, gla_w_in, gla_w_gate_down, gla_w_gate_up, gla_b_gate, gla_g_head, gla_w_out, mla_w_in, mla_g_q, mla_w_uq, mla_g_kv, mla_w_ukv, mla_w_out, swa_w_in, swa_sinks, swa_w_out):
    b_sz, t_lat, d = x.shape
    t_ctx = ctx.shape[1]
    depth = w_ada.shape[0]
    n_lat, n_ctx = b_sz * t_lat, b_sz * t_ctx
    n_tok = n_lat + n_ctx
    tm = ROW_TILE
    assert t_lat % tm == 0 and n_ctx % tm == 0 and b_sz < MOD_ROWS
    tpb = t_lat // tm
    n_lat_tiles = n_lat // tm

    hs = jnp.concatenate([x.reshape(n_lat, d), ctx.reshape(n_ctx, d)], axis=0)
    cond = jnp.zeros((MOD_ROWS, d), F32).at[:b_sz].set(c).at[b_sz].set(c_ctx)
    mod_all = _modulation(cond, w_ada, b_ada)
    cosf, sinf = _rope_tables(t_lat)
    common = dict(tiles_per_batch=tpb, nb=b_sz)
    tm_big = 2 * tm if t_lat % (2 * tm) == 0 and n_ctx % (2 * tm) == 0 else tm
    w_ffn_in_b, w_ffn_out_b = w_ffn_in.astype(BF16), w_ffn_out.astype(BF16)

    for i in range(depth):
        kind, j = i % 3, i // 3
        last = i == depth - 1
        n_out = n_lat if last else n_tok
        mod = mod_all[i].reshape(MOD_ROWS, 1, 6 * d)

        if kind == 0:
            rank = GLA_GATE_RANK
            qk_w = gla_w_gate_up.shape[-1]
            pad = LANES - 2 * rank
            w_gd = jnp.concatenate([gla_w_gate_down[j, 0], gla_w_gate_down[j, 1], jnp.zeros((d, pad), F32)],
                                   axis=1).astype(BF16)
            p, gd = _gla_in_proj(hs, g_norm[i, 0], mod, gla_w_in[j].astype(BF16), w_gd, tm_big, t_lat // tm_big, b_sz)
            wu_ext = jnp.zeros((2, LANES, qk_w), F32)
            for dr in range(2):
                wu_ext = wu_ext.at[dr, dr * rank:(dr + 1) * rank].set(gla_w_gate_up[j, dr])
            o_f, o_b = _gla_scan(p, gd, wu_ext.astype(BF16), gla_b_gate[j].reshape(2, 1, qk_w), b_sz, t_lat, t_ctx)
            a = a2 = _gla_combine(o_f, o_b, p, gla_g_head[j], n_out)
            w_o = gla_w_out[j]
        elif kind == 1:
            n_heads = mla_w_out.shape[1] // MLA_V
            w_in = jnp.concatenate([mla_w_in[j], jnp.zeros((d, LANES - ROPE_DIM), F32)], axis=1).astype(BF16)
            w_uq = mla_w_uq[j].reshape(MLA_Q_RANK, n_heads, MLA_NOPE + ROPE_DIM)
            w_uq = jnp.concatenate([w_uq[:, :, :MLA_NOPE].reshape(MLA_Q_RANK, -1),
                                    w_uq[:, :, MLA_NOPE:].reshape(MLA_Q_RANK, -1)], axis=1).astype(BF16)
            w_ukv = mla_w_ukv[j].reshape(MLA_KV_RANK, n_heads, MLA_NOPE + MLA_V)
            w_ukv = jnp.concatenate([w_ukv[:, :, :MLA_NOPE].reshape(MLA_KV_RANK, -1),
                                     w_ukv[:, :, MLA_NOPE:].reshape(MLA_KV_RANK, -1)], axis=1).astype(BF16)
            q_scale = (MLA_NOPE + ROPE_DIM) ** -0.5 * LOG2E
            qn, qr, kv, kr = _mla_in_proj(hs, g_norm[i, 0], mod, w_in, mla_g_q[j], w_uq, mla_g_kv[j], w_ukv,
                                          n_heads * MLA_NOPE, q_scale, cosf, sinf, n_lat_tiles, **common)
            a, a2 = _mla_attention(qn, qr, kv, kr, b_sz, t_lat, t_ctx)
            w_o = mla_w_out[j]
        else:
            n_q = swa_sinks.shape[1]
            qkw = (n_q + n_q // SWA_GROUP) * SWA_HEAD_DIM
            qkv = _swa_in_proj(hs, g_norm[i, 0], mod, swa_w_in[j].astype(BF16), cosf, sinf, qkw // LANES,
                               (n_q * SWA_HEAD_DIM) // LANES, SWA_HEAD_DIM ** -0.5 * LOG2E, n_lat_tiles, **common)
            a = a2 = _swa_attention(qkv, swa_sinks[j], b_sz, t_lat, t_ctx)
            w_o = swa_w_out[j]

        hs = _proj_resid(a, a2, w_o.astype(BF16), hs, g_norm[i, 1], mod, 2, n_out, name="mixer_out_proj", **common)
        hs = _ffn(hs, g_norm[i, 2], g_norm[i, 3], mod, w_ffn_in_b, w_ffn_out_b, i, n_out, tm_big, t_lat // tm_big, b_sz,
                  name="ffn")

    return hs.reshape(b_sz, t_lat, d)
```

```python
import functools

import numpy as np
import jax
import jax.numpy as jnp
from jax import lax
from jax.experimental import pallas as pl
from jax.experimental.pallas import tpu as pltpu

F32 = jnp.float32
BF16 = jnp.bfloat16

EPS = 1e-6
ROPE_THETA = 10000.0
ROPE_DIM = 64
GRID_W = 64
WINDOW = 128
GLA_HEADS = 4
GLA_GATE_RANK = 16
GLA_TAU = 16.0
MLA_Q_RANK = 512
MLA_KV_RANK = 512
MLA_NOPE = 128
MLA_V = 128
SWA_HEAD_DIM = 64
SWA_GROUP = 8

LANES = 128
SUBLANES = 8
VMEM_LIMIT = 56 * 1024 * 1024
ROW_TILE = 512
MOD_ROWS = 16
NEG_BIG = -1e30
LOG2E = 1.4426950408889634


def _cparams(sem):
    return pltpu.CompilerParams(dimension_semantics=sem, vmem_limit_bytes=VMEM_LIMIT)


def _dot(a, b):
    return jnp.dot(a, b, preferred_element_type=F32)


def _dot_nt(a, b):
    return lax.dot_general(a, b, (((1,), (1,)), ((), ())), preferred_element_type=F32)


def _dot_tn(a, b):
    return lax.dot_general(a, b, (((0,), (0,)), ((), ())), preferred_element_type=F32)


def _sigmoid(x):
    return 1.0 / (1.0 + jnp.exp(-x))


def _rms(x):
    return x * lax.rsqrt(jnp.mean(x * x, axis=-1, keepdims=True) + EPS)


def _mod_kernel(c_ref, w_ref, b_ref, o_ref):
    c = c_ref[...]
    s = (c * _sigmoid(c)).astype(BF16)
    o_ref[0] = _dot(s, w_ref[0].astype(BF16)) + b_ref[0]


def _modulation(cond, w_ada, b_ada):
    depth, d, n = w_ada.shape
    tn = 1024
    return pl.pallas_call(
        _mod_kernel,
        grid=(depth, n // tn),
        in_specs=[
            pl.BlockSpec((MOD_ROWS, d), lambda l, j: (0, 0)),
            pl.BlockSpec((1, d, tn), lambda l, j: (l, 0, j)),
            pl.BlockSpec((1, 1, tn), lambda l, j: (l, 0, j)),
        ],
        out_specs=pl.BlockSpec((1, MOD_ROWS, tn), lambda l, j: (l, 0, j)),
        out_shape=jax.ShapeDtypeStruct((depth, MOD_ROWS, n), F32),
        compiler_params=_cparams(("parallel", "parallel")),
        name="modulation",
    )(cond, w_ada, b_ada.reshape(depth, 1, n))


NORM_ROWS = 16


def _adaln_rows(dst_ref, x_ref, g_ref, mod_ref, shift_idx, scale_idx, d):
    m = mod_ref[0]
    gs = g_ref[...] * (1.0 + m[:, scale_idx * d:(scale_idx + 1) * d])
    sh = m[:, shift_idx * d:(shift_idx + 1) * d]
    for r in range(0, x_ref.shape[0], NORM_ROWS):
        rows = pl.ds(r, NORM_ROWS)
        dst_ref[rows, :] = (_rms(x_ref[rows, :]) * gs + sh).astype(dst_ref.dtype)


def _resid_norm_rows(o_ref, h_ref, y_ref, g_ref, mod_ref, gate_idx, d):
    gg = mod_ref[0][:, gate_idx * d:(gate_idx + 1) * d] * g_ref[...]
    for r in range(0, h_ref.shape[0], NORM_ROWS):
        rows = pl.ds(r, NORM_ROWS)
        o_ref[rows, :] = h_ref[rows, :] + _rms(y_ref[rows, :]) * gg


def _rope_chunk(y, cosf, sinf, first, is_lat):
    partner = jnp.where(first, pltpu.roll(y, LANES - ROPE_DIM // 2, 1), pltpu.roll(y, ROPE_DIM // 2, 1))
    return jnp.where(is_lat, y * cosf + partner * sinf, y)


def _first_half_lanes(shape):
    lane = lax.broadcasted_iota(jnp.int32, shape, 1)
    return (lane % ROPE_DIM) < (ROPE_DIM // 2)


def _mod_spec(mod, tiles_per_batch, nb):
    return pl.BlockSpec((1, 1, mod.shape[-1]), lambda i, *_: (jnp.minimum(i // tiles_per_batch, nb), 0, 0))


def _gla_in_proj_kernel(x_ref, g_ref, mod_ref, w_ref, wgd_ref, o_ref, gd_ref, a_scr, *, d):
    @pl.when(pl.program_id(1) == 0)
    def _():
        _adaln_rows(a_scr, x_ref, g_ref, mod_ref, 0, 1, d)
        gd_ref[...] = _dot(a_scr[...], wgd_ref[...])

    o_ref[...] = _dot(a_scr[...], w_ref[...]).astype(o_ref.dtype)


def _gla_in_proj(hs, g, mod, w, w_gd, tm, tiles_per_batch, nb):
    n_rows, d = hs.shape
    n = w.shape[1]
    tn = 1024
    n_side = w_gd.shape[1]
    return pl.pallas_call(
        functools.partial(_gla_in_proj_kernel, d=d),
        grid=(n_rows // tm, n // tn),
        in_specs=[
            pl.BlockSpec((tm, d), lambda i, j: (i, 0)),
            pl.BlockSpec((1, d), lambda i, j: (0, 0)),
            _mod_spec(mod, tiles_per_batch, nb),
            pl.BlockSpec((d, tn), lambda i, j: (0, j)),
            pl.BlockSpec((d, n_side), lambda i, j: (0, 0)),
        ],
        out_specs=[pl.BlockSpec((tm, tn), lambda i, j: (i, j)),
                   pl.BlockSpec((tm, n_side), lambda i, j: (i, 0))],
        out_shape=[jax.ShapeDtypeStruct((n_rows, n), BF16),
                   jax.ShapeDtypeStruct((n_rows, n_side), F32)],
        scratch_shapes=[pltpu.VMEM((tm, d), BF16)],
        compiler_params=_cparams(("parallel", "arbitrary")),
        name="gla_in_proj",
    )(hs, g.reshape(1, d), mod, w, w_gd)


def _swa_in_proj_kernel(x_ref, g_ref, mod_ref, cos_ref, sin_ref, w_ref, o_ref, a_scr, *, d, n_lat_tiles, n_roped,
                        n_scaled, out_scale):
    _adaln_rows(a_scr, x_ref, g_ref, mod_ref, 0, 1, d)
    y_all = _dot(a_scr[...], w_ref[...])
    is_lat = pl.program_id(0) < n_lat_tiles
    cosf, sinf = cos_ref[...], sin_ref[...]
    first = _first_half_lanes(cosf.shape)
    for c in range(y_all.shape[1] // LANES):
        y = y_all[:, c * LANES:(c + 1) * LANES]
        if c < n_roped:
            y = _rope_chunk(y, cosf, sinf, first, is_lat)
        if c < n_scaled:
            y = y * out_scale
        o_ref[:, c * LANES:(c + 1) * LANES] = y.astype(o_ref.dtype)


def _swa_in_proj(hs, g, mod, w, cosf, sinf, n_roped, n_scaled, out_scale, n_lat_tiles, tiles_per_batch, nb):
    n_rows, d = hs.shape
    n = w.shape[1]
    tm = ROW_TILE
    return pl.pallas_call(
        functools.partial(_swa_in_proj_kernel, d=d, n_lat_tiles=n_lat_tiles, n_roped=n_roped, n_scaled=n_scaled,
                          out_scale=out_scale),
        grid=(n_rows // tm,),
        in_specs=[
            pl.BlockSpec((tm, d), lambda i: (i, 0)),
            pl.BlockSpec((1, d), lambda i: (0, 0)),
            _mod_spec(mod, tiles_per_batch, nb),
            pl.BlockSpec((tm, LANES), lambda i: (i % tiles_per_batch, 0)),
            pl.BlockSpec((tm, LANES), lambda i: (i % tiles_per_batch, 0)),
            pl.BlockSpec((d, n), lambda i: (0, 0)),
        ],
        out_specs=pl.BlockSpec((tm, n), lambda i: (i, 0)),
        out_shape=jax.ShapeDtypeStruct((n_rows, n), BF16),
        scratch_shapes=[pltpu.VMEM((tm, d), BF16)],
        compiler_params=_cparams(("parallel",)),
        name="swa_in_proj",
    )(hs, g.reshape(1, d), mod, cosf, sinf, w)


def _mla_in_proj_kernel(x_ref, g_ref, mod_ref, cos_ref, sin_ref, win_ref, gq_ref, gkv_ref, wuq_ref, wukv_ref,
                        qn_ref, qr_ref, kv_ref, kr_ref, a_scr, *, d, n_lat_tiles, q_scale):
    _adaln_rows(a_scr, x_ref, g_ref, mod_ref, 0, 1, d)
    p1 = _dot(a_scr[...], win_ref[...])
    cq = (_rms(p1[:, 0:MLA_Q_RANK]) * gq_ref[...]).astype(BF16)
    ckv = (_rms(p1[:, MLA_Q_RANK:MLA_Q_RANK + MLA_KV_RANK]) * gkv_ref[...]).astype(BF16)
    q = _dot(cq, wuq_ref[...]) * q_scale
    n_nope = qn_ref.shape[1]
    qn_ref[...] = q[:, 0:n_nope].astype(qn_ref.dtype)
    kv_ref[...] = _dot(ckv, wukv_ref[...]).astype(kv_ref.dtype)
    is_lat = pl.program_id(0) < n_lat_tiles
    cosf, sinf = cos_ref[...], sin_ref[...]
    first = _first_half_lanes(cosf.shape)
    for c in range(qr_ref.shape[1] // LANES):
        y = q[:, n_nope + c * LANES:n_nope + (c + 1) * LANES]
        qr_ref[:, c * LANES:(c + 1) * LANES] = _rope_chunk(y, cosf, sinf, first, is_lat).astype(qr_ref.dtype)
    k_rope = p1[:, MLA_Q_RANK + MLA_KV_RANK:MLA_Q_RANK + MLA_KV_RANK + LANES]
    kr_ref[...] = _rope_chunk(k_rope, cosf, sinf, first, is_lat).astype(kr_ref.dtype)


def _mla_in_proj(hs, g, mod, w_in, g_q, w_uq, g_kv, w_ukv, n_nope, q_scale, cosf, sinf, n_lat_tiles, tiles_per_batch,
                 nb):
    n_rows, d = hs.shape
    tm = ROW_TILE
    n_q, n_kv = w_uq.shape[1], w_ukv.shape[1]
    const = lambda shape: pl.BlockSpec(shape, lambda i: (0, 0))
    row = lambda w: pl.BlockSpec((tm, w), lambda i: (i, 0))
    return pl.pallas_call(
        functools.partial(_mla_in_proj_kernel, d=d, n_lat_tiles=n_lat_tiles, q_scale=q_scale),
        grid=(n_rows // tm,),
        in_specs=[
            row(d), const((1, d)), _mod_spec(mod, tiles_per_batch, nb),
            pl.BlockSpec((tm, LANES), lambda i: (i % tiles_per_batch, 0)),
            pl.BlockSpec((tm, LANES), lambda i: (i % tiles_per_batch, 0)),
            const(w_in.shape), const((1, MLA_Q_RANK)), const((1, MLA_KV_RANK)), const(w_uq.shape), const(w_ukv.shape),
        ],
        out_specs=[row(n_nope), row(n_q - n_nope), row(n_kv), row(LANES)],
        out_shape=[jax.ShapeDtypeStruct((n_rows, n_nope), BF16), jax.ShapeDtypeStruct((n_rows, n_q - n_nope), BF16),
                   jax.ShapeDtypeStruct((n_rows, n_kv), BF16), jax.ShapeDtypeStruct((n_rows, LANES), BF16)],
        scratch_shapes=[pltpu.VMEM((tm, d), BF16)],
        compiler_params=_cparams(("parallel",)),
        name="mla_in_proj",
    )(hs, g.reshape(1, d), mod, cosf, sinf, w_in, g_q.reshape(1, -1), g_kv.reshape(1, -1), w_uq, w_ukv)


def _proj_resid_kernel(a_ref, a2_ref, w_ref, h_ref, g_ref, mod_ref, o_ref, y_scr, *, gate_idx, d, n1):
    def body(src_ref):
        y_scr[...] = _dot(src_ref[...].astype(BF16), w_ref[...])
        _resid_norm_rows(o_ref, h_ref, y_scr, g_ref, mod_ref, gate_idx, d)

    pl.when(pl.program_id(0) < n1)(lambda: body(a_ref))
    pl.when(pl.program_id(0) >= n1)(lambda: body(a2_ref))


def _proj_resid(a, a2, w, hs, g, mod, gate_idx, n_rows, tiles_per_batch, nb, name):
    k, d = w.shape
    tm = ROW_TILE
    n1 = a.shape[0] // tm
    return pl.pallas_call(
        functools.partial(_proj_resid_kernel, gate_idx=gate_idx, d=d, n1=n1),
        grid=(n_rows // tm,),
        in_specs=[
            pl.BlockSpec((tm, k), lambda i: (jnp.minimum(i, n1 - 1), 0)),
            pl.BlockSpec((tm, k), lambda i: (jnp.maximum(i - n1, 0), 0)),
            pl.BlockSpec((k, d), lambda i: (0, 0)),
            pl.BlockSpec((tm, d), lambda i: (i, 0)),
            pl.BlockSpec((1, d), lambda i: (0, 0)),
            _mod_spec(mod, tiles_per_batch, nb),
        ],
        out_specs=pl.BlockSpec((tm, d), lambda i: (i, 0)),
        out_shape=jax.ShapeDtypeStruct((n_rows, d), F32),
        scratch_shapes=[pltpu.VMEM((tm, d), F32)],
        compiler_params=_cparams(("parallel",)),
        name=name,
    )(a, a2, w, hs, g.reshape(1, d), mod)


def _ffn_kernel(h_ref, g2_ref, g3_ref, mod_ref, wg_ref, wu_ref, wo_ref, o_ref, a_scr, *, d):
    j = pl.program_id(1)

    @pl.when(j == 0)
    def _():
        _adaln_rows(a_scr, h_ref, g2_ref, mod_ref, 3, 4, d)
        o_ref[...] = jnp.zeros_like(o_ref)

    a = a_scr[...]
    gt = _dot(a, wg_ref[...])
    up = _dot(a, wu_ref[...])
    act = (gt * _sigmoid(gt) * up).astype(BF16)
    o_ref[...] += _dot(act, wo_ref[...])

    @pl.when(j == pl.num_programs(1) - 1)
    def _():
        _resid_norm_rows(o_ref, h_ref, o_ref, g3_ref, mod_ref, 5, d)


def _ffn(hs, g2, g3, mod, w_in, w_out, layer, n_rows, tm, tiles_per_batch, nb, name):
    d = hs.shape[1]
    f = w_out.shape[1]
    tf = 512
    nf = f // tf
    return pl.pallas_call(
        functools.partial(_ffn_kernel, d=d),
        grid=(n_rows // tm, nf),
        in_specs=[
            pl.BlockSpec((tm, d), lambda i, j: (i, 0)),
            pl.BlockSpec((1, d), lambda i, j: (0, 0)),
            pl.BlockSpec((1, d), lambda i, j: (0, 0)),
            _mod_spec(mod, tiles_per_batch, nb),
            pl.BlockSpec((None, d, tf), lambda i, j: (layer, 0, j)),
            pl.BlockSpec((None, d, tf), lambda i, j: (layer, 0, nf + j)),
            pl.BlockSpec((None, tf, d), lambda i, j: (layer, j, 0)),
        ],
        out_specs=pl.BlockSpec((tm, d), lambda i, j: (i, 0)),
        out_shape=jax.ShapeDtypeStruct((n_rows, d), F32),
        scratch_shapes=[pltpu.VMEM((tm, d), BF16)],
        compiler_params=_cparams(("parallel", "arbitrary")),
        name=name,
    )(hs, g2.reshape(1, d), g3.reshape(1, d), mod, w_in, w_in, w_out)


GLA_CHUNK = 128


def _gla_structure(cs, reverse):
    idx = np.arange(cs)
    ip = cs - 1 - idx if reverse else idx
    ii, jj = ip[:, None], ip[None, :]
    masks, dist = [], []
    s = cs // 2
    while s >= 1:
        masks.append(((ii // (2 * s)) == (jj // (2 * s))) & ((ii & s) != 0) & ((jj & s) == 0))
        pp = (ii // (2 * s)) * (2 * s) + s - 1
        dist.append((jj > np.minimum(ii, pp)) & (jj <= np.maximum(ii, pp)))
        s //= 2
    masks.append(ii == jj)
    return (np.stack(masks).astype(np.float32), (jj <= ii).astype(np.float32),
            np.concatenate(dist, axis=0).astype(np.float32))


def _gla_chunk(q_b, k_b, v_b, lg, msk_ref, tri, dist_ref, st_ref, *, cs, reverse):
    hi = lg.astype(BF16)
    lo = (lg - hi.astype(F32)).astype(BF16)
    b = _dot(tri, hi) + _dot(tri, lo)
    yield
    last = 0 if reverse else cs - 1
    b_last = b[last:last + 1, :]

    st = st_ref[...]
    o = _dot_nt(q_b * jnp.exp2(b).astype(BF16), st.astype(BF16))
    yield

    attn = jnp.zeros((cs, cs), F32)
    nlev = dist_ref.shape[0] // cs
    for lvl in range(nlev):
        nd = _dot(dist_ref[lvl * cs:(lvl + 1) * cs, :], hi)
        fac = jnp.exp2(nd).astype(BF16)
        attn = attn + msk_ref[lvl] * _dot_nt(q_b * fac, k_b * fac)
        yield
    attn = attn + msk_ref[nlev] * _dot_nt(q_b, k_b)

    k_dec = k_b * jnp.exp2(b_last - b).astype(BF16)
    st_ref[...] = st * jnp.exp2(b_last) + _dot_tn(v_b, k_dec)
    yield
    return o + _dot(attn.astype(BF16), v_b)


def _interleave(gens):
    results = [None] * len(gens)
    active = list(range(len(gens)))
    while active:
        for idx in list(active):
            try:
                next(gens[idx])
            except StopIteration as stop:
                results[idx] = stop.value
                active.remove(idx)
    return results


def _gla_scan_kernel(qf_ref, kf_ref, vf_ref, gf_ref, qb_ref, kb_ref, vb_ref, gb_ref, wu_ref, bg_ref,
                     msk_ref, tri_ref, dist_ref, of_ref, ob_ref, st_scr, *, cs, dk, dv):
    @pl.when(pl.program_id(1) == 0)
    def _():
        st_scr[...] = jnp.zeros_like(st_scr)

    dirs = ((qf_ref, kf_ref, vf_ref, gf_ref, of_ref), (qb_ref, kb_ref, vb_ref, gb_ref, ob_ref))
    chains, dests = [], []
    for dr, (q_ref, k_ref, v_ref, gd_ref, o_ref) in enumerate(dirs):
        z = _dot(gd_ref[...].astype(BF16), wu_ref[dr]) + bg_ref[dr]
        lg_all = (jnp.minimum(z, 0.0) - jnp.log1p(jnp.exp(-jnp.abs(z)))) * (LOG2E / GLA_TAU)
        for h in range(GLA_HEADS):
            ks = slice(h * dk, (h + 1) * dk)
            vs = slice(h * dv, (h + 1) * dv)
            chains.append(_gla_chunk(q_ref[:, ks] * (dk ** -0.5), k_ref[:, ks], v_ref[:, vs].astype(BF16),
                                     lg_all[:, ks], msk_ref.at[dr], tri_ref[dr], dist_ref.at[dr], st_scr.at[dr, h],
                                     cs=cs, reverse=bool(dr)))
            dests.append((o_ref, vs))
    for (o_ref, vs), o in zip(dests, _interleave(chains)):
        o_ref[:, vs] = o.astype(o_ref.dtype)


def _gla_scan(p, gd, wu_ext, bg, b_sz, t_lat, t_ctx):
    cs = GLA_CHUNK
    qkw = wu_ext.shape[-1]
    dk = qkw // GLA_HEADS
    vw = 2 * qkw
    dv = vw // GLA_HEADS
    n_tok = p.shape[0]
    ncc, ncl = t_ctx // cs, t_lat // cs
    ctx_blk0 = (b_sz * t_lat) // cs
    structs = [_gla_structure(cs, rev) for rev in (False, True)]
    masks, tri, dist = (np.stack(parts) for parts in zip(*structs))

    def rowblk(rev):
        def f(b, s):
            if rev:
                cc, lc = ncc - 1 - s, ncl - 1 - (s - ncc)
            else:
                cc, lc = s, s - ncc
            return jnp.where(s < ncc, ctx_blk0 + b * ncc + cc, b * ncl + lc)
        return f

    def dir_specs(rev):
        rb = rowblk(rev)
        return [pl.BlockSpec((cs, qkw), lambda b, s: (rb(b, s), 0)),
                pl.BlockSpec((cs, qkw), lambda b, s: (rb(b, s), 1)),
                pl.BlockSpec((cs, vw), lambda b, s: (rb(b, s), 1)),
                pl.BlockSpec((cs, LANES), lambda b, s: (rb(b, s), 0))]

    return pl.pallas_call(
        functools.partial(_gla_scan_kernel, cs=cs, dk=dk, dv=dv),
        grid=(b_sz, ncc + ncl),
        in_specs=dir_specs(False) + dir_specs(True) + [
            pl.BlockSpec((2, LANES, qkw), lambda b, s: (0, 0, 0)),
            pl.BlockSpec((2, 1, qkw), lambda b, s: (0, 0, 0)),
            pl.BlockSpec(masks.shape, lambda b, s: (0, 0, 0, 0)),
            pl.BlockSpec(tri.shape, lambda b, s: (0, 0, 0)),
            pl.BlockSpec(dist.shape, lambda b, s: (0, 0, 0)),
        ],
        out_specs=[pl.BlockSpec((cs, vw), lambda b, s: (rowblk(False)(b, s), 0)),
                   pl.BlockSpec((cs, vw), lambda b, s: (rowblk(True)(b, s), 0))],
        out_shape=[jax.ShapeDtypeStruct((n_tok, vw), BF16)] * 2,
        scratch_shapes=[pltpu.VMEM((2, GLA_HEADS, dv, dk), F32)],
        compiler_params=_cparams(("parallel", "arbitrary")),
        name="gla_scan",
    )(p, p, p, gd, p, p, p, gd, wu_ext, bg, jnp.asarray(masks), jnp.asarray(tri, dtype=BF16),
      jnp.asarray(dist, dtype=BF16))


def _gla_out_proj_kernel(of_ref, ob_ref, r_ref, gh_ref, w_ref, h_ref, g_ref, mod_ref, o_ref, a_scr, y_scr,
                         *, gate_idx, d, dv):
    gh = gh_ref[...]
    for r0 in range(0, a_scr.shape[0], NORM_ROWS):
        rows = pl.ds(r0, NORM_ROWS)
        for h in range(GLA_HEADS):
            sl = slice(h * dv, (h + 1) * dv)
            y = _rms(of_ref[rows, sl].astype(F32) + ob_ref[rows, sl].astype(F32)) * gh
            r = r_ref[rows, sl].astype(F32)
            a_scr[rows, sl] = (y * (r * _sigmoid(r))).astype(a_scr.dtype)
    y_scr[...] = _dot(a_scr[...], w_ref[...])
    _resid_norm_rows(o_ref, h_ref, y_scr, g_ref, mod_ref, gate_idx, d)


def _gla_out_proj(o_f, o_b, p, g_head, w, hs, g, mod, gate_idx, n_rows, tiles_per_batch, nb):
    d = o_f.shape[1]
    dv = d // GLA_HEADS
    tm = ROW_TILE
    rcol = p.shape[1] // d - 1
    row = lambda col: pl.BlockSpec((tm, d), lambda i: (i, col))
    return pl.pallas_call(
        functools.partial(_gla_out_proj_kernel, gate_idx=gate_idx, d=d, dv=dv),
        grid=(n_rows // tm,),
        in_specs=[
            row(0), row(0), row(rcol),
            pl.BlockSpec((1, dv), lambda i: (0, 0)),
            pl.BlockSpec((d, d), lambda i: (0, 0)),
            row(0),
            pl.BlockSpec((1, d), lambda i: (0, 0)),
            _mod_spec(mod, tiles_per_batch, nb),
        ],
        out_specs=row(0),
        out_shape=jax.ShapeDtypeStruct((n_rows, d), F32),
        scratch_shapes=[pltpu.VMEM((tm, d), BF16), pltpu.VMEM((tm, d), F32)],
        compiler_params=_cparams(("parallel",)),
        name="gla_out_proj",
    )(o_f, o_b, p, g_head.reshape(1, dv), w, hs, g.reshape(1, d), mod)


def _rope_tables(t_lat):
    t = jnp.arange(t_lat)
    row = (t // GRID_W).astype(F32)
    col = (t % GRID_W).astype(F32)
    n_freq = ROPE_DIM // 4
    inv_freq = ROPE_THETA ** (-jnp.arange(n_freq, dtype=F32) / n_freq)
    ang = jnp.concatenate([row[:, None] * inv_freq, col[:, None] * inv_freq], axis=-1)
    cos, sin = jnp.cos(ang), jnp.sin(ang)
    reps = LANES // ROPE_DIM
    return jnp.tile(jnp.concatenate([cos, cos], axis=-1), (1, reps)), jnp.tile(jnp.concatenate([-sin, sin], axis=-1), (1, reps))


MLA_TQ = 512
MLA_TK = 1024
MLA_HEADS_PER_STEP = 2


def _mla_attn_kernel(qn_ref, qr_ref, qnc_ref, qrc_ref, knc_ref, vc_ref, krc_ref, knl_ref, vl_ref, krl_ref,
                     o_ref, oc_ref, kcat, vext, *, tc, tl):
    t = pl.program_id(2)
    nh = MLA_HEADS_PER_STEP
    rd = ROPE_DIM
    vw = MLA_V + LANES

    def qcat(n_ref, r_ref, h):
        return jnp.concatenate([n_ref[:, h * MLA_NOPE:(h + 1) * MLA_NOPE], r_ref[:, h * rd:(h + 1) * rd]], axis=1)

    def finish(acc):
        return acc[:, 0:MLA_V] / acc[:, MLA_V:vw]

    @pl.when(t == 0)
    def _():
        for h in range(nh):
            ns = slice(h * MLA_NOPE, (h + 1) * MLA_NOPE)
            kcat[h, 0:tc, 0:MLA_NOPE] = knc_ref[:, ns]
            kcat[h, 0:tc, MLA_NOPE:MLA_NOPE + rd] = krc_ref[:, 0:rd]
            kcat[h, tc:tc + tl, 0:MLA_NOPE] = knl_ref[:, ns]
            kcat[h, tc:tc + tl, MLA_NOPE:MLA_NOPE + rd] = krl_ref[:, 0:rd]
            vext[0:tc, h * vw:h * vw + MLA_V] = vc_ref[:, h * MLA_V:(h + 1) * MLA_V]
            vext[tc:tc + tl, h * vw:h * vw + MLA_V] = vl_ref[:, h * MLA_V:(h + 1) * MLA_V]
            vext[:, h * vw + MLA_V:(h + 1) * vw] = jnp.ones((tc + tl, LANES), BF16)
        for h in range(nh):
            s = _dot_nt(qcat(qnc_ref, qrc_ref, h), kcat[h, 0:tc, :])
            p = jnp.exp2(s - jnp.max(s, axis=-1, keepdims=True))
            acc = _dot(p.astype(BF16), vext[0:tc, h * vw:(h + 1) * vw])
            oc_ref[:, h * MLA_V:(h + 1) * MLA_V] = finish(acc).astype(oc_ref.dtype)

    for h in range(nh):
        qc = qcat(qn_ref, qr_ref, h)
        s = _dot_nt(qc, kcat[h, 0:tc, :])
        m = jnp.max(s, axis=-1, keepdims=True)
        acc = _dot(jnp.exp2(s - m).astype(BF16), vext[0:tc, h * vw:(h + 1) * vw])
        for c0 in range(tc, tc + tl, MLA_TK):
            s = _dot_nt(qc, kcat[h, c0:c0 + MLA_TK, :])
            m_new = jnp.maximum(m, jnp.max(s, axis=-1, keepdims=True))
            acc = jnp.exp2(m - m_new) * acc + _dot(jnp.exp2(s - m_new).astype(BF16),
                                                   vext[c0:c0 + MLA_TK, h * vw:(h + 1) * vw])
            m = m_new
        o_ref[:, h * MLA_V:(h + 1) * MLA_V] = finish(acc).astype(o_ref.dtype)


def _mla_attention(qn, qr, kv, kr, b_sz, t_lat, t_ctx):
    hd = qn.shape[1]
    nh = MLA_HEADS_PER_STEP
    n_heads = hd // MLA_NOPE
    tq = MLA_TQ
    assert t_lat % tq == 0 and t_lat % MLA_TK == 0
    nlt = t_lat // tq
    ctx_blk0 = (b_sz * t_lat) // t_ctx
    vcol0 = n_heads // nh
    wn, wr = nh * MLA_NOPE, nh * ROPE_DIM
    return pl.pallas_call(
        functools.partial(_mla_attn_kernel, tc=t_ctx, tl=t_lat),
        grid=(b_sz, n_heads // nh, nlt),
        in_specs=[
            pl.BlockSpec((tq, wn), lambda b, h, t: (b * nlt + t, h)),
            pl.BlockSpec((tq, wr), lambda b, h, t: (b * nlt + t, h)),
            pl.BlockSpec((t_ctx, wn), lambda b, h, t: (ctx_blk0 + b, h)),
            pl.BlockSpec((t_ctx, wr), lambda b, h, t: (ctx_blk0 + b, h)),
            pl.BlockSpec((t_ctx, wn), lambda b, h, t: (ctx_blk0 + b, h)),
            pl.BlockSpec((t_ctx, wn), lambda b, h, t: (ctx_blk0 + b, vcol0 + h)),
            pl.BlockSpec((t_ctx, LANES), lambda b, h, t: (ctx_blk0 + b, 0)),
            pl.BlockSpec((t_lat, wn), lambda b, h, t: (b, h)),
            pl.BlockSpec((t_lat, wn), lambda b, h, t: (b, vcol0 + h)),
            pl.BlockSpec((t_lat, LANES), lambda b, h, t: (b, 0)),
        ],
        out_specs=[pl.BlockSpec((tq, nh * MLA_V), lambda b, h, t: (b * nlt + t, h)),
                   pl.BlockSpec((t_ctx, nh * MLA_V), lambda b, h, t: (b, h))],
        out_shape=[jax.ShapeDtypeStruct((b_sz * t_lat, n_heads * MLA_V), BF16),
                   jax.ShapeDtypeStruct((b_sz * t_ctx, n_heads * MLA_V), BF16)],
        scratch_shapes=[pltpu.VMEM((nh, t_ctx + t_lat, MLA_NOPE + ROPE_DIM), BF16),
                        pltpu.VMEM((t_ctx + t_lat, nh * (MLA_V + LANES)), BF16)],
        compiler_params=_cparams(("parallel", "parallel", "arbitrary")),
        name="mla_attention",
    )(qn, qr, qn, qr, kv, kv, kr, kv, kv, kr)


def _swa_attn_kernel(sink_ref, q_ref, kc_ref, vc_ref, kp_ref, kc0_ref, kn_ref, vp_ref, vc0_ref, vn_ref, o_ref,
                     *, nct, nb, n_kv):
    t = pl.program_id(1)
    n = t - nct
    blk = WINDOW
    hd = SWA_HEAD_DIM
    tc = kc_ref.shape[0]
    nk = tc + 3 * blk
    npair = SWA_GROUP // 2

    qi = lax.broadcasted_iota(jnp.int32, (blk, 3 * blk), 0)
    kj = lax.broadcasted_iota(jnp.int32, (blk, 3 * blk), 1)
    lo = jnp.where(n > 0, 0, blk)
    hi = jnp.where(t < nct, 0, jnp.where(n < nb - 1, 3 * blk, 2 * blk))
    win_ok = (jnp.abs(qi - kj + blk) <= WINDOW) & (kj >= lo) & (kj < hi)
    bias = jnp.concatenate([jnp.zeros((blk, tc), F32), jnp.where(win_ok, 0.0, NEG_BIG)], axis=1)
    bias = jnp.concatenate([bias] * npair, axis=0)
    lane = lax.broadcasted_iota(jnp.int32, (blk, 2 * hd), 1)
    zeros = jnp.zeros((nk, hd), BF16)
    ones = jnp.ones((nk, hd), BF16)

    def kv_chain(kvh):
        ks = slice(kvh * hd, (kvh + 1) * hd)
        k_all = jnp.concatenate([kc_ref[:, ks], kp_ref[:, ks], kc0_ref[:, ks], kn_ref[:, ks]], axis=0)
        v_all = jnp.concatenate([vc_ref[:, ks], vp_ref[:, ks], vc0_ref[:, ks], vn_ref[:, ks]], axis=0)
        k_bd = jnp.concatenate([jnp.concatenate([k_all, zeros], axis=1),
                                jnp.concatenate([zeros, k_all], axis=1)], axis=0)
        h0 = kvh * SWA_GROUP
        q_st = jnp.concatenate([q_ref[:, (h0 + 2 * j) * hd:(h0 + 2 * j + 2) * hd] for j in range(npair)], axis=0)
        s = _dot_nt(q_st, k_bd)
        yield
        ps, ms = [], []
        for e in range(2):
            sink_col = jnp.concatenate([jnp.full((blk, 1), sink_ref[h0 + 2 * j + e] * LOG2E, F32)
                                        for j in range(npair)], axis=0)
            s_e = s[:, e * nk:(e + 1) * nk] + bias
            m = jnp.maximum(jnp.max(s_e, axis=-1, keepdims=True), sink_col)
            p = jnp.exp2(s_e - m).astype(BF16)
            ps += [p[:, 0:tc], p[:, tc:]]
            ms.append(sink_col - m)

        def v_ext(r0, r1, e):
            v, z, o = v_all[r0:r1], zeros[r0:r1], ones[r0:r1]
            return jnp.concatenate([z, v, z, o] if e else [v, z, o, z], axis=1)

        tail = tc + 2 * blk
        acc = _dot(jnp.concatenate([ps[1][:, 2 * blk:], ps[3][:, 2 * blk:]], axis=1),
                   jnp.concatenate([v_ext(tail, nk, 0), v_ext(tail, nk, 1)], axis=0))
        for e in range(2):
            acc = acc + _dot(ps[2 * e], v_ext(0, tc, e)) + _dot(ps[2 * e + 1][:, 0:2 * blk], v_ext(tc, tail, e))
        for j in range(npair):
            r = slice(j * blk, (j + 1) * blk)
            sink_term = jnp.exp2(jnp.where(lane < hd, ms[0][r], ms[1][r]))
            o = acc[r, 0:2 * hd] / (acc[r, 2 * hd:4 * hd] + sink_term)
            o_ref[:, (h0 + 2 * j) * hd:(h0 + 2 * j + 2) * hd] = o.astype(o_ref.dtype)

    chains = [kv_chain(kvh) for kvh in range(n_kv)]
    next(chains[0])
    for kvh in range(n_kv):
        if kvh + 1 < n_kv:
            next(chains[kvh + 1])
        for _ in chains[kvh]:
            pass


def _swa_attention(qkv, sinks, b_sz, t_lat, t_ctx):
    n_tok = qkv.shape[0]
    blk = WINDOW
    hd = SWA_HEAD_DIM
    n_q = sinks.shape[0]
    n_kv = n_q // SWA_GROUP
    kvw = n_kv * hd
    kcol = (n_q * hd) // kvw
    vcol = kcol + 1
    nct, nb = t_ctx // blk, t_lat // blk
    ctx_blk0 = (b_sz * t_lat) // blk
    ctx_row0 = (b_sz * t_lat) // t_ctx

    def qrow(b, t):
        return jnp.where(t < nct, ctx_blk0 + b * nct + t, b * nb + t - nct)

    def krow(b, t, off):
        return b * nb + jnp.clip(t - nct + off, 0, nb - 1)

    kspec = lambda off: pl.BlockSpec((blk, kvw), lambda b, t: (krow(b, t, off), kcol))
    vspec = lambda off: pl.BlockSpec((blk, kvw), lambda b, t: (krow(b, t, off), vcol))
    return pl.pallas_call(
        functools.partial(_swa_attn_kernel, nct=nct, nb=nb, n_kv=n_kv),
        grid=(b_sz, nct + nb),
        in_specs=[
            pl.BlockSpec(memory_space=pltpu.SMEM),
            pl.BlockSpec((blk, n_q * hd), lambda b, t: (qrow(b, t), 0)),
            pl.BlockSpec((t_ctx, kvw), lambda b, t: (ctx_row0 + b, kcol)),
            pl.BlockSpec((t_ctx, kvw), lambda b, t: (ctx_row0 + b, vcol)),
            kspec(-1), kspec(0), kspec(1), vspec(-1), vspec(0), vspec(1),
        ],
        out_specs=pl.BlockSpec((blk, n_q * hd), lambda b, t: (qrow(b, t), 0)),
        out_shape=jax.ShapeDtypeStruct((n_tok, n_q * hd), BF16),
        compiler_params=_cparams(("parallel", "arbitrary")),
        name="swa_attention",
    )(sinks, qkv, qkv, qkv, qkv, qkv, qkv, qkv, qkv, qkv)


def kernel(x, c, ctx, c_ctx, w_ada, b_ada, g_norm, w_ffn_in, w_ffn_out, gla_w_in, gla_w_gate_down, gla_w_gate_up, gla_b_gate, gla_g_head, gla_w_out, mla_w_in, mla_g_q, mla_w_uq, mla_g_kv, mla_w_ukv, mla_w_out, swa_w_in, swa_sinks, swa_w_out):
    b_sz, t_lat, d = x.shape
    t_ctx = ctx.shape[1]
    depth = w_ada.shape[0]
    n_lat, n_ctx = b_sz * t_lat, b_sz * t_ctx
    n_tok = n_lat + n_ctx
    tm = ROW_TILE
    assert t_lat % tm == 0 and n_ctx % tm == 0 and b_sz < MOD_ROWS
    tpb = t_lat // tm
    n_lat_tiles = n_lat // tm

    hs = jnp.concatenate([x.reshape(n_lat, d), ctx.reshape(n_ctx, d)], axis=0)
    cond = jnp.zeros((MOD_ROWS, d), F32).at[:b_sz].set(c).at[b_sz].set(c_ctx)
    mod_all = _modulation(cond, w_ada, b_ada)
    cosf, sinf = _rope_tables(t_lat)
    common = dict(tiles_per_batch=tpb, nb=b_sz)
    tm_big = 2 * tm if t_lat % (2 * tm) == 0 and n_ctx % (2 * tm) == 0 else tm
    w_ffn_in_b, w_ffn_out_b = w_ffn_in.astype(BF16), w_ffn_out.astype(BF16)

    for i in range(depth):
        kind, j = i % 3, i // 3
        last = i == depth - 1
        n_out = n_lat if last else n_tok
        mod = mod_all[i].reshape(MOD_ROWS, 1, 6 * d)

        if kind == 0:
            rank = GLA_GATE_RANK
            qk_w = gla_w_gate_up.shape[-1]
            pad = LANES - 2 * rank
            w_gd = jnp.concatenate([gla_w_gate_down[j, 0], gla_w_gate_down[j, 1], jnp.zeros((d, pad), F32)],
                                   axis=1).astype(BF16)
            p, gd = _gla_in_proj(hs, g_norm[i, 0], mod, gla_w_in[j].astype(BF16), w_gd, tm_big, t_lat // tm_big, b_sz)
            wu_ext = jnp.zeros((2, LANES, qk_w), F32)
            for dr in range(2):
                wu_ext = wu_ext.at[dr, dr * rank:(dr + 1) * rank].set(gla_w_gate_up[j, dr])
            o_f, o_b = _gla_scan(p, gd, wu_ext.astype(BF16), gla_b_gate[j].reshape(2, 1, qk_w), b_sz, t_lat, t_ctx)
            hs = _gla_out_proj(o_f, o_b, p, gla_g_head[j], gla_w_out[j].astype(BF16), hs, g_norm[i, 1], mod, 2, n_out,
                               **common)
        elif kind == 1:
            n_heads = mla_w_out.shape[1] // MLA_V
            w_in = jnp.concatenate([mla_w_in[j], jnp.zeros((d, LANES - ROPE_DIM), F32)], axis=1).astype(BF16)
            w_uq = mla_w_uq[j].reshape(MLA_Q_RANK, n_heads, MLA_NOPE + ROPE_DIM)
            w_uq = jnp.concatenate([w_uq[:, :, :MLA_NOPE].reshape(MLA_Q_RANK, -1),
                                    w_uq[:, :, MLA_NOPE:].reshape(MLA_Q_RANK, -1)], axis=1).astype(BF16)
            w_ukv = mla_w_ukv[j].reshape(MLA_KV_RANK, n_heads, MLA_NOPE + MLA_V)
            w_ukv = jnp.concatenate([w_ukv[:, :, :MLA_NOPE].reshape(MLA_KV_RANK, -1),
                                     w_ukv[:, :, MLA_NOPE:].reshape(MLA_KV_RANK, -1)], axis=1).astype(BF16)
            q_scale = (MLA_NOPE + ROPE_DIM) ** -0.5 * LOG2E
            qn, qr, kv, kr = _mla_in_proj(hs, g_norm[i, 0], mod, w_in, mla_g_q[j], w_uq, mla_g_kv[j], w_ukv,
                                          n_heads * MLA_NOPE, q_scale, cosf, sinf, n_lat_tiles, **common)
            a, a2 = _mla_attention(qn, qr, kv, kr, b_sz, t_lat, t_ctx)
            w_o = mla_w_out[j]
        else:
            n_q = swa_sinks.shape[1]
            qkw = (n_q + n_q // SWA_GROUP) * SWA_HEAD_DIM
            qkv = _swa_in_proj(hs, g_norm[i, 0], mod, swa_w_in[j].astype(BF16), cosf, sinf, qkw // LANES,
                               (n_q * SWA_HEAD_DIM) // LANES, SWA_HEAD_DIM ** -0.5 * LOG2E, n_lat_tiles, **common)
            a = a2 = _swa_attention(qkv, swa_sinks[j], b_sz, t_lat, t_ctx)
            w_o = swa_w_out[j]

        if kind != 0:
            hs = _proj_resid(a, a2, w_o.astype(BF16), hs, g_norm[i, 1], mod, 2, n_out, name="mixer_out_proj", **common)
        hs = _ffn(hs, g_norm[i, 2], g_norm[i, 3], mod, w_ffn_in_b, w_ffn_out_b, i, n_out, tm_big, t_lat // tm_big, b_sz,
                  name="ffn")

    return hs.reshape(b_sz, t_lat, d)
```

```python
import functools

import numpy as np
import jax
import jax.numpy as jnp
from jax import lax
from jax.experimental import pallas as pl
from jax.experimental.pallas import tpu as pltpu

F32 = jnp.float32
BF16 = jnp.bfloat16

EPS = 1e-6
ROPE_THETA = 10000.0
ROPE_DIM = 64
GRID_W = 64
WINDOW = 128
GLA_HEADS = 4
GLA_GATE_RANK = 16
GLA_TAU = 16.0
MLA_Q_RANK = 512
MLA_KV_RANK = 512
MLA_NOPE = 128
MLA_V = 128
SWA_HEAD_DIM = 64
SWA_GROUP = 8

LANES = 128
SUBLANES = 8
VMEM_LIMIT = 56 * 1024 * 1024
ROW_TILE = 512
MOD_ROWS = 16
NEG_BIG = -1e30
LOG2E = 1.4426950408889634


def _cparams(sem):
    return pltpu.CompilerParams(dimension_semantics=sem, vmem_limit_bytes=VMEM_LIMIT)


def _dot(a, b):
    return jnp.dot(a, b, preferred_element_type=F32)


def _dot_nt(a, b):
    return lax.dot_general(a, b, (((1,), (1,)), ((), ())), preferred_element_type=F32)


def _dot_tn(a, b):
    return lax.dot_general(a, b, (((0,), (0,)), ((), ())), preferred_element_type=F32)


def _sigmoid(x):
    return 1.0 / (1.0 + jnp.exp(-x))


def _rms(x):
    return x * lax.rsqrt(jnp.mean(x * x, axis=-1, keepdims=True) + EPS)


def _mod_kernel(c_ref, w_ref, b_ref, o_ref):
    c = c_ref[...]
    s = (c * _sigmoid(c)).astype(BF16)
    o_ref[0] = _dot(s, w_ref[0].astype(BF16)) + b_ref[0]


def _modulation(cond, w_ada, b_ada):
    depth, d, n = w_ada.shape
    tn = 1024
    return pl.pallas_call(
        _mod_kernel,
        grid=(depth, n // tn),
        in_specs=[
            pl.BlockSpec((MOD_ROWS, d), lambda l, j: (0, 0)),
            pl.BlockSpec((1, d, tn), lambda l, j: (l, 0, j)),
            pl.BlockSpec((1, 1, tn), lambda l, j: (l, 0, j)),
        ],
        out_specs=pl.BlockSpec((1, MOD_ROWS, tn), lambda l, j: (l, 0, j)),
        out_shape=jax.ShapeDtypeStruct((depth, MOD_ROWS, n), F32),
        compiler_params=_cparams(("parallel", "parallel")),
        name="modulation",
    )(cond, w_ada, b_ada.reshape(depth, 1, n))


NORM_ROWS = 16


def _adaln_rows(dst_ref, x_ref, g_ref, mod_ref, shift_idx, scale_idx, d):
    m = mod_ref[0]
    gs = g_ref[...] * (1.0 + m[:, scale_idx * d:(scale_idx + 1) * d])
    sh = m[:, shift_idx * d:(shift_idx + 1) * d]
    for r in range(0, x_ref.shape[0], NORM_ROWS):
        rows = pl.ds(r, NORM_ROWS)
        dst_ref[rows, :] = (_rms(x_ref[rows, :]) * gs + sh).astype(dst_ref.dtype)


def _resid_norm_rows(o_ref, h_ref, y_ref, g_ref, mod_ref, gate_idx, d):
    gg = mod_ref[0][:, gate_idx * d:(gate_idx + 1) * d] * g_ref[...]
    for r in range(0, h_ref.shape[0], NORM_ROWS):
        rows = pl.ds(r, NORM_ROWS)
        o_ref[rows, :] = h_ref[rows, :] + _rms(y_ref[rows, :]) * gg


def _rope_chunk(y, cosf, sinf, first, is_lat):
    partner = jnp.where(first, pltpu.roll(y, LANES - ROPE_DIM // 2, 1), pltpu.roll(y, ROPE_DIM // 2, 1))
    return jnp.where(is_lat, y * cosf + partner * sinf, y)


def _first_half_lanes(shape):
    lane = lax.broadcasted_iota(jnp.int32, shape, 1)
    return (lane % ROPE_DIM) < (ROPE_DIM // 2)


def _mod_spec(mod, tiles_per_batch, nb):
    return pl.BlockSpec((1, 1, mod.shape[-1]), lambda i, *_: (jnp.minimum(i // tiles_per_batch, nb), 0, 0))


def _gla_in_proj_kernel(x_ref, g_ref, mod_ref, w_ref, wgd_ref, o_ref, gd_ref, a_scr, *, d):
    @pl.when(pl.program_id(1) == 0)
    def _():
        _adaln_rows(a_scr, x_ref, g_ref, mod_ref, 0, 1, d)
        gd_ref[...] = _dot(a_scr[...], wgd_ref[...])

    o_ref[...] = _dot(a_scr[...], w_ref[...]).astype(o_ref.dtype)


def _gla_in_proj(hs, g, mod, w, w_gd, tm, tiles_per_batch, nb):
    n_rows, d = hs.shape
    n = w.shape[1]
    tn = 1024
    n_side = w_gd.shape[1]
    return pl.pallas_call(
        functools.partial(_gla_in_proj_kernel, d=d),
        grid=(n_rows // tm, n // tn),
        in_specs=[
            pl.BlockSpec((tm, d), lambda i, j: (i, 0)),
            pl.BlockSpec((1, d), lambda i, j: (0, 0)),
            _mod_spec(mod, tiles_per_batch, nb),
            pl.BlockSpec((d, tn), lambda i, j: (0, j)),
            pl.BlockSpec((d, n_side), lambda i, j: (0, 0)),
        ],
        out_specs=[pl.BlockSpec((tm, tn), lambda i, j: (i, j)),
                   pl.BlockSpec((tm, n_side), lambda i, j: (i, 0))],
        out_shape=[jax.ShapeDtypeStruct((n_rows, n), BF16),
                   jax.ShapeDtypeStruct((n_rows, n_side), F32)],
        scratch_shapes=[pltpu.VMEM((tm, d), BF16)],
        compiler_params=_cparams(("parallel", "arbitrary")),
        name="gla_in_proj",
    )(hs, g.reshape(1, d), mod, w, w_gd)


def _swa_in_proj_kernel(x_ref, g_ref, mod_ref, cos_ref, sin_ref, w_ref, o_ref, a_scr, *, d, n_lat_tiles, n_roped,
                        n_scaled, out_scale):
    _adaln_rows(a_scr, x_ref, g_ref, mod_ref, 0, 1, d)
    y_all = _dot(a_scr[...], w_ref[...])
    is_lat = pl.program_id(0) < n_lat_tiles
    cosf, sinf = cos_ref[...], sin_ref[...]
    first = _first_half_lanes(cosf.shape)
    for c in range(y_all.shape[1] // LANES):
        y = y_all[:, c * LANES:(c + 1) * LANES]
        if c < n_roped:
            y = _rope_chunk(y, cosf, sinf, first, is_lat)
        if c < n_scaled:
            y = y * out_scale
        o_ref[:, c * LANES:(c + 1) * LANES] = y.astype(o_ref.dtype)


def _swa_in_proj(hs, g, mod, w, cosf, sinf, n_roped, n_scaled, out_scale, n_lat_tiles, tiles_per_batch, nb):
    n_rows, d = hs.shape
    n = w.shape[1]
    tm = ROW_TILE
    return pl.pallas_call(
        functools.partial(_swa_in_proj_kernel, d=d, n_lat_tiles=n_lat_tiles, n_roped=n_roped, n_scaled=n_scaled,
                          out_scale=out_scale),
        grid=(n_rows // tm,),
        in_specs=[
            pl.BlockSpec((tm, d), lambda i: (i, 0)),
            pl.BlockSpec((1, d), lambda i: (0, 0)),
            _mod_spec(mod, tiles_per_batch, nb),
            pl.BlockSpec((tm, LANES), lambda i: (i % tiles_per_batch, 0)),
            pl.BlockSpec((tm, LANES), lambda i: (i % tiles_per_batch, 0)),
            pl.BlockSpec((d, n), lambda i: (0, 0)),
        ],
        out_specs=pl.BlockSpec((tm, n), lambda i: (i, 0)),
        out_shape=jax.ShapeDtypeStruct((n_rows, n), BF16),
        scratch_shapes=[pltpu.VMEM((tm, d), BF16)],
        compiler_params=_cparams(("parallel",)),
        name="swa_in_proj",
    )(hs, g.reshape(1, d), mod, cosf, sinf, w)


def _mla_in_proj_kernel(x_ref, g_ref, mod_ref, cos_ref, sin_ref, win_ref, gq_ref, gkv_ref, wuq_ref, wukv_ref,
                        qn_ref, qr_ref, kv_ref, kr_ref, a_scr, *, d, n_lat_tiles, q_scale):
    _adaln_rows(a_scr, x_ref, g_ref, mod_ref, 0, 1, d)
    p1 = _dot(a_scr[...], win_ref[...])
    cq = (_rms(p1[:, 0:MLA_Q_RANK]) * gq_ref[...]).astype(BF16)
    ckv = (_rms(p1[:, MLA_Q_RANK:MLA_Q_RANK + MLA_KV_RANK]) * gkv_ref[...]).astype(BF16)
    q = _dot(cq, wuq_ref[...]) * q_scale
    n_nope = qn_ref.shape[1]
    qn_ref[...] = q[:, 0:n_nope].astype(qn_ref.dtype)
    kv_ref[...] = _dot(ckv, wukv_ref[...]).astype(kv_ref.dtype)
    is_lat = pl.program_id(0) < n_lat_tiles
    cosf, sinf = cos_ref[...], sin_ref[...]
    first = _first_half_lanes(cosf.shape)
    for c in range(qr_ref.shape[1] // LANES):
        y = q[:, n_nope + c * LANES:n_nope + (c + 1) * LANES]
        qr_ref[:, c * LANES:(c + 1) * LANES] = _rope_chunk(y, cosf, sinf, first, is_lat).astype(qr_ref.dtype)
    k_rope = p1[:, MLA_Q_RANK + MLA_KV_RANK:MLA_Q_RANK + MLA_KV_RANK + LANES]
    kr_ref[...] = _rope_chunk(k_rope, cosf, sinf, first, is_lat).astype(kr_ref.dtype)


def _mla_in_proj(hs, g, mod, w_in, g_q, w_uq, g_kv, w_ukv, n_nope, q_scale, cosf, sinf, n_lat_tiles, tiles_per_batch,
                 nb):
    n_rows, d = hs.shape
    tm = ROW_TILE
    n_q, n_kv = w_uq.shape[1], w_ukv.shape[1]
    const = lambda shape: pl.BlockSpec(shape, lambda i: (0, 0))
    row = lambda w: pl.BlockSpec((tm, w), lambda i: (i, 0))
    return pl.pallas_call(
        functools.partial(_mla_in_proj_kernel, d=d, n_lat_tiles=n_lat_tiles, q_scale=q_scale),
        grid=(n_rows // tm,),
        in_specs=[
            row(d), const((1, d)), _mod_spec(mod, tiles_per_batch, nb),
            pl.BlockSpec((tm, LANES), lambda i: (i % tiles_per_batch, 0)),
            pl.BlockSpec((tm, LANES), lambda i: (i % tiles_per_batch, 0)),
            const(w_in.shape), const((1, MLA_Q_RANK)), const((1, MLA_KV_RANK)), const(w_uq.shape), const(w_ukv.shape),
        ],
        out_specs=[row(n_nope), row(n_q - n_nope), row(n_kv), row(LANES)],
        out_shape=[jax.ShapeDtypeStruct((n_rows, n_nope), BF16), jax.ShapeDtypeStruct((n_rows, n_q - n_nope), BF16),
                   jax.ShapeDtypeStruct((n_rows, n_kv), BF16), jax.ShapeDtypeStruct((n_rows, LANES), BF16)],
        scratch_shapes=[pltpu.VMEM((tm, d), BF16)],
        compiler_params=_cparams(("parallel",)),
        name="mla_in_proj",
    )(hs, g.reshape(1, d), mod, cosf, sinf, w_in, g_q.reshape(1, -1), g_kv.reshape(1, -1), w_uq, w_ukv)


def _proj_resid_kernel(a_ref, a2_ref, w_ref, h_ref, g_ref, mod_ref, o_ref, y_scr, *, gate_idx, d, n1):
    def body(src_ref):
        y_scr[...] = _dot(src_ref[...].astype(BF16), w_ref[...])
        _resid_norm_rows(o_ref, h_ref, y_scr, g_ref, mod_ref, gate_idx, d)

    pl.when(pl.program_id(0) < n1)(lambda: body(a_ref))
    pl.when(pl.program_id(0) >= n1)(lambda: body(a2_ref))


def _proj_resid(a, a2, w, hs, g, mod, gate_idx, n_rows, tiles_per_batch, nb, name):
    k, d = w.shape
    tm = ROW_TILE
    n1 = a.shape[0] // tm
    return pl.pallas_call(
        functools.partial(_proj_resid_kernel, gate_idx=gate_idx, d=d, n1=n1),
        grid=(n_rows // tm,),
        in_specs=[
            pl.BlockSpec((tm, k), lambda i: (jnp.minimum(i, n1 - 1), 0)),
            pl.BlockSpec((tm, k), lambda i: (jnp.maximum(i - n1, 0), 0)),
            pl.BlockSpec((k, d), lambda i: (0, 0)),
            pl.BlockSpec((tm, d), lambda i: (i, 0)),
            pl.BlockSpec((1, d), lambda i: (0, 0)),
            _mod_spec(mod, tiles_per_batch, nb),
        ],
        out_specs=pl.BlockSpec((tm, d), lambda i: (i, 0)),
        out_shape=jax.ShapeDtypeStruct((n_rows, d), F32),
        scratch_shapes=[pltpu.VMEM((tm, d), F32)],
        compiler_params=_cparams(("parallel",)),
        name=name,
    )(a, a2, w, hs, g.reshape(1, d), mod)


def _ffn_kernel(h_ref, g2_ref, g3_ref, mod_ref, wg_ref, wu_ref, wo_ref, o_ref, a_scr, *, d):
    j = pl.program_id(1)

    @pl.when(j == 0)
    def _():
        _adaln_rows(a_scr, h_ref, g2_ref, mod_ref, 3, 4, d)
        o_ref[...] = jnp.zeros_like(o_ref)

    a = a_scr[...]
    gt = _dot(a, wg_ref[...])
    up = _dot(a, wu_ref[...])
    act = (gt * _sigmoid(gt) * up).astype(BF16)
    o_ref[...] += _dot(act, wo_ref[...])

    @pl.when(j == pl.num_programs(1) - 1)
    def _():
        _resid_norm_rows(o_ref, h_ref, o_ref, g3_ref, mod_ref, 5, d)


def _ffn(hs, g2, g3, mod, w_in, w_out, layer, n_rows, tm, tiles_per_batch, nb, name):
    d = hs.shape[1]
    f = w_out.shape[1]
    tf = 512
    nf = f // tf
    return pl.pallas_call(
        functools.partial(_ffn_kernel, d=d),
        grid=(n_rows // tm, nf),
        in_specs=[
            pl.BlockSpec((tm, d), lambda i, j: (i, 0)),
            pl.BlockSpec((1, d), lambda i, j: (0, 0)),
            pl.BlockSpec((1, d), lambda i, j: (0, 0)),
            _mod_spec(mod, tiles_per_batch, nb),
            pl.BlockSpec((None, d, tf), lambda i, j: (layer, 0, j)),
            pl.BlockSpec((None, d, tf), lambda i, j: (layer, 0, nf + j)),
            pl.BlockSpec((None, tf, d), lambda i, j: (layer, j, 0)),
        ],
        out_specs=pl.BlockSpec((tm, d), lambda i, j: (i, 0)),
        out_shape=jax.ShapeDtypeStruct((n_rows, d), F32),
        scratch_shapes=[pltpu.VMEM((tm, d), BF16)],
        compiler_params=_cparams(("parallel", "arbitrary")),
        name=name,
    )(hs, g2.reshape(1, d), g3.reshape(1, d), mod, w_in, w_in, w_out)


GLA_CHUNK = 128


def _gla_structure(cs, reverse):
    idx = np.arange(cs)
    ip = cs - 1 - idx if reverse else idx
    ii, jj = ip[:, None], ip[None, :]
    masks, dist = [], []
    s = cs // 2
    while s >= 1:
        masks.append(((ii // (2 * s)) == (jj // (2 * s))) & ((ii & s) != 0) & ((jj & s) == 0))
        pp = (ii // (2 * s)) * (2 * s) + s - 1
        dist.append((jj > np.minimum(ii, pp)) & (jj <= np.maximum(ii, pp)))
        s //= 2
    masks.append(ii == jj)
    return (np.stack(masks).astype(np.float32), (jj <= ii).astype(np.float32),
            np.concatenate(dist, axis=0).astype(np.float32))


def _gla_chunk(q_b, k_b, v_b, lg, msk_ref, tri, dist_ref, st_ref, *, cs, reverse):
    hi = lg.astype(BF16)
    lo = (lg - hi.astype(F32)).astype(BF16)
    b = _dot(tri, hi) + _dot(tri, lo)
    yield
    last = 0 if reverse else cs - 1
    b_last = b[last:last + 1, :]

    st = st_ref[...]
    o = _dot_nt(q_b * jnp.exp2(b).astype(BF16), st.astype(BF16))
    yield

    attn = jnp.zeros((cs, cs), F32)
    nlev = dist_ref.shape[0] // cs
    for lvl in range(nlev):
        nd = _dot(dist_ref[lvl * cs:(lvl + 1) * cs, :], hi)
        fac = jnp.exp2(nd).astype(BF16)
        attn = attn + msk_ref[lvl] * _dot_nt(q_b * fac, k_b * fac)
        yield
    attn = attn + msk_ref[nlev] * _dot_nt(q_b, k_b)

    k_dec = k_b * jnp.exp2(b_last - b).astype(BF16)
    st_ref[...] = st * jnp.exp2(b_last) + _dot_tn(v_b, k_dec)
    yield
    return o + _dot(attn.astype(BF16), v_b)


def _interleave(gens):
    results = [None] * len(gens)
    active = list(range(len(gens)))
    while active:
        for idx in list(active):
            try:
                next(gens[idx])
            except StopIteration as stop:
                results[idx] = stop.value
                active.remove(idx)
    return results


def _gla_scan_kernel(qf_ref, kf_ref, vf_ref, gf_ref, qb_ref, kb_ref, vb_ref, gb_ref, wu_ref, bg_ref,
                     msk_ref, tri_ref, dist_ref, of_ref, ob_ref, st_scr, *, cs, dk, dv):
    @pl.when(pl.program_id(1) == 0)
    def _():
        st_scr[...] = jnp.zeros_like(st_scr)

    dirs = ((qf_ref, kf_ref, vf_ref, gf_ref, of_ref), (qb_ref, kb_ref, vb_ref, gb_ref, ob_ref))
    chains, dests = [], []
    for dr, (q_ref, k_ref, v_ref, gd_ref, o_ref) in enumerate(dirs):
        z = _dot(gd_ref[...].astype(BF16), wu_ref[dr]) + bg_ref[dr]
        lg_all = (jnp.minimum(z, 0.0) - jnp.log1p(jnp.exp(-jnp.abs(z)))) * (LOG2E / GLA_TAU)
        for h in range(GLA_HEADS):
            ks = slice(h * dk, (h + 1) * dk)
            vs = slice(h * dv, (h + 1) * dv)
            chains.append(_gla_chunk(q_ref[:, ks] * (dk ** -0.5), k_ref[:, ks], v_ref[:, vs].astype(BF16),
                                     lg_all[:, ks], msk_ref.at[dr], tri_ref[dr], dist_ref.at[dr], st_scr.at[dr, h],
                                     cs=cs, reverse=bool(dr)))
            dests.append((o_ref, vs))
    for (o_ref, vs), o in zip(dests, _interleave(chains)):
        o_ref[:, vs] = o.astype(o_ref.dtype)


def _gla_scan(p, gd, wu_ext, bg, b_sz, t_lat, t_ctx):
    cs = GLA_CHUNK
    qkw = wu_ext.shape[-1]
    dk = qkw // GLA_HEADS
    vw = 2 * qkw
    dv = vw // GLA_HEADS
    n_tok = p.shape[0]
    ncc, ncl = t_ctx // cs, t_lat // cs
    ctx_blk0 = (b_sz * t_lat) // cs
    structs = [_gla_structure(cs, rev) for rev in (False, True)]
    masks, tri, dist = (np.stack(parts) for parts in zip(*structs))

    def rowblk(rev):
        def f(b, s):
            if rev:
                cc, lc = ncc - 1 - s, ncl - 1 - (s - ncc)
            else:
                cc, lc = s, s - ncc
            return jnp.where(s < ncc, ctx_blk0 + b * ncc + cc, b * ncl + lc)
        return f

    def dir_specs(rev):
        rb = rowblk(rev)
        return [pl.BlockSpec((cs, qkw), lambda b, s: (rb(b, s), 0)),
                pl.BlockSpec((cs, qkw), lambda b, s: (rb(b, s), 1)),
                pl.BlockSpec((cs, vw), lambda b, s: (rb(b, s), 1)),
                pl.BlockSpec((cs, LANES), lambda b, s: (rb(b, s), 0))]

    return pl.pallas_call(
        functools.partial(_gla_scan_kernel, cs=cs, dk=dk, dv=dv),
        grid=(b_sz, ncc + ncl),
        in_specs=dir_specs(False) + dir_specs(True) + [
            pl.BlockSpec((2, LANES, qkw), lambda b, s: (0, 0, 0)),
            pl.BlockSpec((2, 1, qkw), lambda b, s: (0, 0, 0)),
            pl.BlockSpec(masks.shape, lambda b, s: (0, 0, 0, 0)),
            pl.BlockSpec(tri.shape, lambda b, s: (0, 0, 0)),
            pl.BlockSpec(dist.shape, lambda b, s: (0, 0, 0)),
        ],
        out_specs=[pl.BlockSpec((cs, vw), lambda b, s: (rowblk(False)(b, s), 0)),
                   pl.BlockSpec((cs, vw), lambda b, s: (rowblk(True)(b, s), 0))],
        out_shape=[jax.ShapeDtypeStruct((n_tok, vw), BF16)] * 2,
        scratch_shapes=[pltpu.VMEM((2, GLA_HEADS, dv, dk), F32)],
        compiler_params=_cparams(("parallel", "arbitrary")),
        name="gla_scan",
    )(p, p, p, gd, p, p, p, gd, wu_ext, bg, jnp.asarray(masks), jnp.asarray(tri, dtype=BF16),
      jnp.asarray(dist, dtype=BF16))


def _gla_out_proj_kernel(of_ref, ob_ref, r_ref, gh_ref, w_ref, h_ref, g_ref, mod_ref, o_ref, a_scr, y_scr,
                         *, gate_idx, d, dv):
    gh = gh_ref[...]
    for r0 in range(0, a_scr.shape[0], NORM_ROWS):
        rows = pl.ds(r0, NORM_ROWS)
        for h in range(GLA_HEADS):
            sl = slice(h * dv, (h + 1) * dv)
            y = _rms(of_ref[rows, sl].astype(F32) + ob_ref[rows, sl].astype(F32)) * gh
            r = r_ref[rows, sl].astype(F32)
            a_scr[rows, sl] = (y * (r * _sigmoid(r))).astype(a_scr.dtype)
    y_scr[...] = _dot(a_scr[...], w_ref[...])
    _resid_norm_rows(o_ref, h_ref, y_scr, g_ref, mod_ref, gate_idx, d)


def _gla_out_proj(o_f, o_b, p, g_head, w, hs, g, mod, gate_idx, n_rows, tiles_per_batch, nb):
    d = o_f.shape[1]
    dv = d // GLA_HEADS
    tm = ROW_TILE
    rcol = p.shape[1] // d - 1
    row = lambda col: pl.BlockSpec((tm, d), lambda i: (i, col))
    return pl.pallas_call(
        functools.partial(_gla_out_proj_kernel, gate_idx=gate_idx, d=d, dv=dv),
        grid=(n_rows // tm,),
        in_specs=[
            row(0), row(0), row(rcol),
            pl.BlockSpec((1, dv), lambda i: (0, 0)),
            pl.BlockSpec((d, d), lambda i: (0, 0)),
            row(0),
            pl.BlockSpec((1, d), lambda i: (0, 0)),
            _mod_spec(mod, tiles_per_batch, nb),
        ],
        out_specs=row(0),
        out_shape=jax.ShapeDtypeStruct((n_rows, d), F32),
        scratch_shapes=[pltpu.VMEM((tm, d), BF16), pltpu.VMEM((tm, d), F32)],
        compiler_params=_cparams(("parallel",)),
        name="gla_out_proj",
    )(o_f, o_b, p, g_head.reshape(1, dv), w, hs, g.reshape(1, d), mod)


def _rope_tables(t_lat):
    t = jnp.arange(t_lat)
    row = (t // GRID_W).astype(F32)
    col = (t % GRID_W).astype(F32)
    n_freq = ROPE_DIM // 4
    inv_freq = ROPE_THETA ** (-jnp.arange(n_freq, dtype=F32) / n_freq)
    ang = jnp.concatenate([row[:, None] * inv_freq, col[:, None] * inv_freq], axis=-1)
    cos, sin = jnp.cos(ang), jnp.sin(ang)
    reps = LANES // ROPE_DIM
    return jnp.tile(jnp.concatenate([cos, cos], axis=-1), (1, reps)), jnp.tile(jnp.concatenate([-sin, sin], axis=-1), (1, reps))


MLA_TQ = 512
MLA_TK = 1024
MLA_HEADS_PER_STEP = 2


def _mla_attn_kernel(qn_ref, qr_ref, qnc_ref, qrc_ref, knc_ref, vc_ref, krc_ref, knl_ref, vl_ref, krl_ref,
                     o_ref, oc_ref, kcat, vext, *, tc, tl):
    t = pl.program_id(2)
    nh = MLA_HEADS_PER_STEP
    rd = ROPE_DIM
    vw = MLA_V + LANES

    def qcat(n_ref, r_ref, h):
        return jnp.concatenate([n_ref[:, h * MLA_NOPE:(h + 1) * MLA_NOPE], r_ref[:, h * rd:(h + 1) * rd]], axis=1)

    def finish(acc):
        return acc[:, 0:MLA_V] / acc[:, MLA_V:vw]

    @pl.when(t == 0)
    def _():
        for h in range(nh):
            ns = slice(h * MLA_NOPE, (h + 1) * MLA_NOPE)
            kcat[h, 0:tc, 0:MLA_NOPE] = knc_ref[:, ns]
            kcat[h, 0:tc, MLA_NOPE:MLA_NOPE + rd] = krc_ref[:, 0:rd]
            kcat[h, tc:tc + tl, 0:MLA_NOPE] = knl_ref[:, ns]
            kcat[h, tc:tc + tl, MLA_NOPE:MLA_NOPE + rd] = krl_ref[:, 0:rd]
            vext[0:tc, h * vw:h * vw + MLA_V] = vc_ref[:, h * MLA_V:(h + 1) * MLA_V]
            vext[tc:tc + tl, h * vw:h * vw + MLA_V] = vl_ref[:, h * MLA_V:(h + 1) * MLA_V]
            vext[:, h * vw + MLA_V:(h + 1) * vw] = jnp.ones((tc + tl, LANES), BF16)
        for h in range(nh):
            s = _dot_nt(qcat(qnc_ref, qrc_ref, h), kcat[h, 0:tc, :])
            p = jnp.exp2(s - jnp.max(s, axis=-1, keepdims=True))
            acc = _dot(p.astype(BF16), vext[0:tc, h * vw:(h + 1) * vw])
            oc_ref[:, h * MLA_V:(h + 1) * MLA_V] = finish(acc).astype(oc_ref.dtype)

    for h in range(nh):
        qc = qcat(qn_ref, qr_ref, h)
        chunks = [(0, tc)] + [(c0, c0 + MLA_TK) for c0 in range(tc, tc + tl, MLA_TK)]
        s_next = _dot_nt(qc, kcat[h, chunks[0][0]:chunks[0][1], :])
        m = acc = None
        for ci, (c0, c1) in enumerate(chunks):
            s = s_next
            if ci + 1 < len(chunks):
                s_next = _dot_nt(qc, kcat[h, chunks[ci + 1][0]:chunks[ci + 1][1], :])
            s_max = jnp.max(s, axis=-1, keepdims=True)
            m_new = s_max if m is None else jnp.maximum(m, s_max)
            pv = _dot(jnp.exp2(s - m_new).astype(BF16), vext[c0:c1, h * vw:(h + 1) * vw])
            acc = pv if acc is None else jnp.exp2(m - m_new) * acc + pv
            m = m_new
        o_ref[:, h * MLA_V:(h + 1) * MLA_V] = finish(acc).astype(o_ref.dtype)


def _mla_attention(qn, qr, kv, kr, b_sz, t_lat, t_ctx):
    hd = qn.shape[1]
    nh = MLA_HEADS_PER_STEP
    n_heads = hd // MLA_NOPE
    tq = MLA_TQ
    assert t_lat % tq == 0 and t_lat % MLA_TK == 0
    nlt = t_lat // tq
    ctx_blk0 = (b_sz * t_lat) // t_ctx
    vcol0 = n_heads // nh
    wn, wr = nh * MLA_NOPE, nh * ROPE_DIM
    return pl.pallas_call(
        functools.partial(_mla_attn_kernel, tc=t_ctx, tl=t_lat),
        grid=(b_sz, n_heads // nh, nlt),
        in_specs=[
            pl.BlockSpec((tq, wn), lambda b, h, t: (b * nlt + t, h)),
            pl.BlockSpec((tq, wr), lambda b, h, t: (b * nlt + t, h)),
            pl.BlockSpec((t_ctx, wn), lambda b, h, t: (ctx_blk0 + b, h)),
            pl.BlockSpec((t_ctx, wr), lambda b, h, t: (ctx_blk0 + b, h)),
            pl.BlockSpec((t_ctx, wn), lambda b, h, t: (ctx_blk0 + b, h)),
            pl.BlockSpec((t_ctx, wn), lambda b, h, t: (ctx_blk0 + b, vcol0 + h)),
            pl.BlockSpec((t_ctx, LANES), lambda b, h, t: (ctx_blk0 + b, 0)),
            pl.BlockSpec((t_lat, wn), lambda b, h, t: (b, h)),
            pl.BlockSpec((t_lat, wn), lambda b, h, t: (b, vcol0 + h)),
            pl.BlockSpec((t_lat, LANES), lambda b, h, t: (b, 0)),
        ],
        out_specs=[pl.BlockSpec((tq, nh * MLA_V), lambda b, h, t: (b * nlt + t, h)),
                   pl.BlockSpec((t_ctx, nh * MLA_V), lambda b, h, t: (b, h))],
        out_shape=[jax.ShapeDtypeStruct((b_sz * t_lat, n_heads * MLA_V), BF16),
                   jax.ShapeDtypeStruct((b_sz * t_ctx, n_heads * MLA_V), BF16)],
        scratch_shapes=[pltpu.VMEM((nh, t_ctx + t_lat, MLA_NOPE + ROPE_DIM), BF16),
                        pltpu.VMEM((t_ctx + t_lat, nh * (MLA_V + LANES)), BF16)],
        compiler_params=_cparams(("parallel", "parallel", "arbitrary")),
        name="mla_attention",
    )(qn, qr, qn, qr, kv, kv, kr, kv, kv, kr)


def _swa_attn_kernel(sink_ref, q_ref, kc_ref, vc_ref, kp_ref, kc0_ref, kn_ref, vp_ref, vc0_ref, vn_ref, o_ref,
                     *, nct, nb, n_kv):
    t = pl.program_id(1)
    n = t - nct
    blk = WINDOW
    hd = SWA_HEAD_DIM
    tc = kc_ref.shape[0]
    nk = tc + 3 * blk
    npair = SWA_GROUP // 2
    upair = 1

    qi = lax.broadcasted_iota(jnp.int32, (blk, 3 * blk), 0)
    kj = lax.broadcasted_iota(jnp.int32, (blk, 3 * blk), 1)
    lo = jnp.where(n > 0, 0, blk)
    hi = jnp.where(t < nct, 0, jnp.where(n < nb - 1, 3 * blk, 2 * blk))
    win_ok = (jnp.abs(qi - kj + blk) <= WINDOW) & (kj >= lo) & (kj < hi)
    bias = jnp.concatenate([jnp.zeros((blk, tc), F32), jnp.where(win_ok, 0.0, NEG_BIG)], axis=1)
    bias = jnp.concatenate([bias] * upair, axis=0)
    lane = lax.broadcasted_iota(jnp.int32, (blk, 2 * hd), 1)
    zeros = jnp.zeros((nk, hd), BF16)
    ones = jnp.ones((nk, hd), BF16)
    tail = tc + 2 * blk
    shared = {}

    def kv_operands(kvh):
        if kvh not in shared:
            ks = slice(kvh * hd, (kvh + 1) * hd)
            k_all = jnp.concatenate([kc_ref[:, ks], kp_ref[:, ks], kc0_ref[:, ks], kn_ref[:, ks]], axis=0)
            v_all = jnp.concatenate([vc_ref[:, ks], vp_ref[:, ks], vc0_ref[:, ks], vn_ref[:, ks]], axis=0)
            k_bd = jnp.concatenate([jnp.concatenate([k_all, zeros], axis=1),
                                    jnp.concatenate([zeros, k_all], axis=1)], axis=0)

            def v_ext(r0, r1, e):
                v, z, o = v_all[r0:r1], zeros[r0:r1], ones[r0:r1]
                return jnp.concatenate([z, v, z, o] if e else [v, z, o, z], axis=1)

            v_tail = jnp.concatenate([v_ext(tail, nk, 0), v_ext(tail, nk, 1)], axis=0)
            v_main = [(v_ext(0, tc, e), v_ext(tc, tail, e)) for e in range(2)]
            shared[kvh] = (k_bd, v_tail, v_main)
        return shared[kvh]

    def unit_chain(kvh, u):
        k_bd, v_tail, v_main = kv_operands(kvh)
        h0 = kvh * SWA_GROUP + 2 * upair * u
        q_st = jnp.concatenate([q_ref[:, (h0 + 2 * j) * hd:(h0 + 2 * j + 2) * hd] for j in range(upair)], axis=0)
        s = _dot_nt(q_st, k_bd)
        yield
        ps, ms = [], []
        for e in range(2):
            sink_col = jnp.concatenate([jnp.full((blk, 1), sink_ref[h0 + 2 * j + e] * LOG2E, F32)
                                        for j in range(upair)], axis=0)
            s_e = s[:, e * nk:(e + 1) * nk] + bias
            m = jnp.maximum(jnp.max(s_e, axis=-1, keepdims=True), sink_col)
            p = jnp.exp2(s_e - m).astype(BF16)
            ps += [p[:, 0:tc], p[:, tc:]]
            ms.append(sink_col - m)
        acc = _dot(jnp.concatenate([ps[1][:, 2 * blk:], ps[3][:, 2 * blk:]], axis=1), v_tail)
        for e in range(2):
            acc = acc + _dot(ps[2 * e], v_main[e][0]) + _dot(ps[2 * e + 1][:, 0:2 * blk], v_main[e][1])
        for j in range(upair):
            r = slice(j * blk, (j + 1) * blk)
            sink_term = jnp.exp2(jnp.where(lane < hd, ms[0][r], ms[1][r]))
            o = acc[r, 0:2 * hd] / (acc[r, 2 * hd:4 * hd] + sink_term)
            o_ref[:, (h0 + 2 * j) * hd:(h0 + 2 * j + 2) * hd] = o.astype(o_ref.dtype)

    chains = [unit_chain(kvh, u) for kvh in range(n_kv) for u in range(npair // upair)]
    next(chains[0])
    for n_unit, chain in enumerate(chains):
        if n_unit + 1 < len(chains):
            next(chains[n_unit + 1])
        for _ in chain:
            pass


def _swa_attention(qkv, sinks, b_sz, t_lat, t_ctx):
    n_tok = qkv.shape[0]
    blk = WINDOW
    hd = SWA_HEAD_DIM
    n_q = sinks.shape[0]
    n_kv = n_q // SWA_GROUP
    kvw = n_kv * hd
    kcol = (n_q * hd) // kvw
    vcol = kcol + 1
    nct, nb = t_ctx // blk, t_lat // blk
    ctx_blk0 = (b_sz * t_lat) // blk
    ctx_row0 = (b_sz * t_lat) // t_ctx

    def qrow(b, t):
        return jnp.where(t < nct, ctx_blk0 + b * nct + t, b * nb + t - nct)

    def krow(b, t, off):
        return b * nb + jnp.clip(t - nct + off, 0, nb - 1)

    kspec = lambda off: pl.BlockSpec((blk, kvw), lambda b, t: (krow(b, t, off), kcol))
    vspec = lambda off: pl.BlockSpec((blk, kvw), lambda b, t: (krow(b, t, off), vcol))
    return pl.pallas_call(
        functools.partial(_swa_attn_kernel, nct=nct, nb=nb, n_kv=n_kv),
        grid=(b_sz, nct + nb),
        in_specs=[
            pl.BlockSpec(memory_space=pltpu.SMEM),
            pl.BlockSpec((blk, n_q * hd), lambda b, t: (qrow(b, t), 0)),
            pl.BlockSpec((t_ctx, kvw), lambda b, t: (ctx_row0 + b, kcol)),
            pl.BlockSpec((t_ctx, kvw), lambda b, t: (ctx_row0 + b, vcol)),
            kspec(-1), kspec(0), kspec(1), vspec(-1), vspec(0), vspec(1),
        ],
        out_specs=pl.BlockSpec((blk, n_q * hd), lambda b, t: (qrow(b, t), 0)),
        out_shape=jax.ShapeDtypeStruct((n_tok, n_q * hd), BF16),
        compiler_params=_cparams(("parallel", "arbitrary")),
        name="swa_attention",
    )(sinks, qkv, qkv, qkv, qkv, qkv, qkv, qkv, qkv, qkv)


def kernel(x, c, ctx, c_ctx, w_ada, b_ada, g_norm, w_ffn_in, w_ffn_out, gla_w_in, gla_w_gate_down, gla_w_gate_up, gla_b_gate, gla_g_head, gla_w_out, mla_w_in, mla_g_q, mla_w_uq, mla_g_kv, mla_w_ukv, mla_w_out, swa_w_in, swa_sinks, swa_w_out):
    b_sz, t_lat, d = x.shape
    t_ctx = ctx.shape[1]
    depth = w_ada.shape[0]
    n_lat, n_ctx = b_sz * t_lat, b_sz * t_ctx
    n_tok = n_lat + n_ctx
    tm = ROW_TILE
    assert t_lat % tm == 0 and n_ctx % tm == 0 and b_sz < MOD_ROWS
    tpb = t_lat // tm
    n_lat_tiles = n_lat // tm

    hs = jnp.concatenate([x.reshape(n_lat, d), ctx.reshape(n_ctx, d)], axis=0)
    cond = jnp.zeros((MOD_ROWS, d), F32).at[:b_sz].set(c).at[b_sz].set(c_ctx)
    mod_all = _modulation(cond, w_ada, b_ada)
    cosf, sinf = _rope_tables(t_lat)
    common = dict(tiles_per_batch=tpb, nb=b_sz)
    tm_big = 2 * tm if t_lat % (2 * tm) == 0 and n_ctx % (2 * tm) == 0 else tm
    w_ffn_in_b, w_ffn_out_b = w_ffn_in.astype(BF16), w_ffn_out.astype(BF16)

    for i in range(depth):
        kind, j = i % 3, i // 3
        last = i == depth - 1
        n_out = n_lat if last else n_tok
        mod = mod_all[i].reshape(MOD_ROWS, 1, 6 * d)

        if kind == 0:
            rank = GLA_GATE_RANK
            qk_w = gla_w_gate_up.shape[-1]
            pad = LANES - 2 * rank
            w_gd = jnp.concatenate([gla_w_gate_down[j, 0], gla_w_gate_down[j, 1], jnp.zeros((d, pad), F32)],
                                   axis=1).astype(BF16)
            p, gd = _gla_in_proj(hs, g_norm[i, 0], mod, gla_w_in[j].astype(BF16), w_gd, tm_big, t_lat // tm_big, b_sz)
            wu_ext = jnp.zeros((2, LANES, qk_w), F32)
            for dr in range(2):
                wu_ext = wu_ext.at[dr, dr * rank:(dr + 1) * rank].set(gla_w_gate_up[j, dr])
            o_f, o_b = _gla_scan(p, gd, wu_ext.astype(BF16), gla_b_gate[j].reshape(2, 1, qk_w), b_sz, t_lat, t_ctx)
            hs = _gla_out_proj(o_f, o_b, p, gla_g_head[j], gla_w_out[j].astype(BF16), hs, g_norm[i, 1], mod, 2, n_out,
                               **common)
        elif kind == 1:
            n_heads = mla_w_out.shape[1] // MLA_V
            w_in = jnp.concatenate([mla_w_in[j], jnp.zeros((d, LANES - ROPE_DIM), F32)], axis=1).astype(BF16)
            w_uq = mla_w_uq[j].reshape(MLA_Q_RANK, n_heads, MLA_NOPE + ROPE_DIM)
            w_uq = jnp.concatenate([w_uq[:, :, :MLA_NOPE].reshape(MLA_Q_RANK, -1),
                                    w_uq[:, :, MLA_NOPE:].reshape(MLA_Q_RANK, -1)], axis=1).astype(BF16)
            w_ukv = mla_w_ukv[j].reshape(MLA_KV_RANK, n_heads, MLA_NOPE + MLA_V)
            w_ukv = jnp.concatenate([w_ukv[:, :, :MLA_NOPE].reshape(MLA_KV_RANK, -1),
                                     w_ukv[:, :, MLA_NOPE:].reshape(MLA_KV_RANK, -1)], axis=1).astype(BF16)
            q_scale = (MLA_NOPE + ROPE_DIM) ** -0.5 * LOG2E
            qn, qr, kv, kr = _mla_in_proj(hs, g_norm[i, 0], mod, w_in, mla_g_q[j], w_uq, mla_g_kv[j], w_ukv,
                                          n_heads * MLA_NOPE, q_scale, cosf, sinf, n_lat_tiles, **common)
            a, a2 = _mla_attention(qn, qr, kv, kr, b_sz, t_lat, t_ctx)
            w_o = mla_w_out[j]
        else:
            n_q = swa_sinks.shape[1]
            qkw = (n_q + n_q // SWA_GROUP) * SWA_HEAD_DIM
            qkv = _swa_in_proj(hs, g_norm[i, 0], mod, swa_w_in[j].astype(BF16), cosf, sinf, qkw // LANES,
                               (n_q * SWA_HEAD_DIM) // LANES, SWA_HEAD_DIM ** -0.5 * LOG2E, n_lat_tiles, **common)
            a = a2 = _swa_attention(qkv, swa_sinks[j], b_sz, t_lat, t_ctx)
            w_o = swa_w_out[j]

        if kind != 0:
            hs = _proj_resid(a, a2, w_o.astype(BF16), hs, g_norm[i, 1], mod, 2, n_out, name="mixer_out_proj", **common)
        hs = _ffn(hs, g_norm[i, 2], g_norm[i, 3], mod, w_ffn_in_b, w_ffn_out_b, i, n_out, tm_big, t_lat // tm_big, b_sz,
                  name="ffn")

    return hs.reshape(b_sz, t_lat, d)
```

```python
import functools

import numpy as np
import jax
import jax.numpy as jnp
from jax import lax
from jax.experimental import pallas as pl
from jax.experimental.pallas import tpu as pltpu

F32 = jnp.float32
BF16 = jnp.bfloat16

EPS = 1e-6
ROPE_THETA = 10000.0
ROPE_DIM = 64
GRID_W = 64
WINDOW = 128
GLA_HEADS = 4
GLA_GATE_RANK = 16
GLA_TAU = 16.0
MLA_Q_RANK = 512
MLA_KV_RANK = 512
MLA_NOPE = 128
MLA_V = 128
SWA_HEAD_DIM = 64
SWA_GROUP = 8

LANES = 128
VMEM_LIMIT = 56 * 1024 * 1024
ROW_TILE = 512
COL_TILE = 1024
FFN_TILE = 512
NORM_ROWS = 16
MOD_ROWS = 16
NEG_BIG = -1e30
LOG2E = 1.4426950408889634


def _cparams(sem):
    return pltpu.CompilerParams(dimension_semantics=sem, vmem_limit_bytes=VMEM_LIMIT)


def _dot(a, b):
    return jnp.dot(a, b, preferred_element_type=F32)


def _dot_nt(a, b):
    return lax.dot_general(a, b, (((1,), (1,)), ((), ())), preferred_element_type=F32)


def _dot_tn(a, b):
    return lax.dot_general(a, b, (((0,), (0,)), ((), ())), preferred_element_type=F32)


def _sigmoid(x):
    return 1.0 / (1.0 + jnp.exp(-x))


def _rms(x):
    return x * lax.rsqrt(jnp.mean(x * x, axis=-1, keepdims=True) + EPS)


def _mod_kernel(c_ref, w_ref, b_ref, o_ref):
    c = c_ref[...]
    s = (c * _sigmoid(c)).astype(BF16)
    o_ref[0] = _dot(s, w_ref[0].astype(BF16)) + b_ref[0]


def _modulation(cond, w_ada, b_ada):
    depth, d, n = w_ada.shape
    tn = COL_TILE
    return pl.pallas_call(
        _mod_kernel,
        grid=(depth, n // tn),
        in_specs=[
            pl.BlockSpec((MOD_ROWS, d), lambda l, j: (0, 0)),
            pl.BlockSpec((1, d, tn), lambda l, j: (l, 0, j)),
            pl.BlockSpec((1, 1, tn), lambda l, j: (l, 0, j)),
        ],
        out_specs=pl.BlockSpec((1, MOD_ROWS, tn), lambda l, j: (l, 0, j)),
        out_shape=jax.ShapeDtypeStruct((depth, MOD_ROWS, n), F32),
        compiler_params=_cparams(("parallel", "parallel")),
        name="modulation",
    )(cond, w_ada, b_ada.reshape(depth, 1, n))


def _adaln_rows(dst_ref, x_ref, g_ref, mod_ref, shift_idx, scale_idx, d):
    m = mod_ref[0]
    gs = g_ref[...] * (1.0 + m[:, scale_idx * d:(scale_idx + 1) * d])
    sh = m[:, shift_idx * d:(shift_idx + 1) * d]
    for r in range(0, x_ref.shape[0], NORM_ROWS):
        rows = pl.ds(r, NORM_ROWS)
        dst_ref[rows, :] = (_rms(x_ref[rows, :]) * gs + sh).astype(dst_ref.dtype)


def _resid_norm_rows(o_ref, h_ref, y_ref, g_ref, mod_ref, gate_idx, d):
    gg = mod_ref[0][:, gate_idx * d:(gate_idx + 1) * d] * g_ref[...]
    for r in range(0, h_ref.shape[0], NORM_ROWS):
        rows = pl.ds(r, NORM_ROWS)
        o_ref[rows, :] = h_ref[rows, :] + _rms(y_ref[rows, :]) * gg


def _rope_chunk(y, cosf, sinf, first, is_lat):
    partner = jnp.where(first, pltpu.roll(y, LANES - ROPE_DIM // 2, 1), pltpu.roll(y, ROPE_DIM // 2, 1))
    return jnp.where(is_lat, y * cosf + partner * sinf, y)


def _first_half_lanes(shape):
    lane = lax.broadcasted_iota(jnp.int32, shape, 1)
    return (lane % ROPE_DIM) < (ROPE_DIM // 2)


def _mod_spec(mod, tiles_per_batch, nb):
    return pl.BlockSpec((1, 1, mod.shape[-1]), lambda i, *_: (jnp.minimum(i // tiles_per_batch, nb), 0, 0))


def _gla_in_proj_kernel(x_ref, g_ref, mod_ref, w_ref, wgd_ref, o_ref, gd_ref, a_scr, *, d):
    @pl.when(pl.program_id(1) == 0)
    def _():
        _adaln_rows(a_scr, x_ref, g_ref, mod_ref, 0, 1, d)
        gd_ref[...] = _dot(a_scr[...], wgd_ref[...])

    o_ref[...] = _dot(a_scr[...], w_ref[...]).astype(o_ref.dtype)


def _gla_in_proj(hs, g, mod, w, w_gd, tm, tiles_per_batch, nb):
    n_rows, d = hs.shape
    n = w.shape[1]
    tn = COL_TILE
    n_side = w_gd.shape[1]
    return pl.pallas_call(
        functools.partial(_gla_in_proj_kernel, d=d),
        grid=(n_rows // tm, n // tn),
        in_specs=[
            pl.BlockSpec((tm, d), lambda i, j: (i, 0)),
            pl.BlockSpec((1, d), lambda i, j: (0, 0)),
            _mod_spec(mod, tiles_per_batch, nb),
            pl.BlockSpec((d, tn), lambda i, j: (0, j)),
            pl.BlockSpec((d, n_side), lambda i, j: (0, 0)),
        ],
        out_specs=[pl.BlockSpec((tm, tn), lambda i, j: (i, j)),
                   pl.BlockSpec((tm, n_side), lambda i, j: (i, 0))],
        out_shape=[jax.ShapeDtypeStruct((n_rows, n), BF16),
                   jax.ShapeDtypeStruct((n_rows, n_side), F32)],
        scratch_shapes=[pltpu.VMEM((tm, d), BF16)],
        compiler_params=_cparams(("parallel", "arbitrary")),
        name="gla_in_proj",
    )(hs, g.reshape(1, d), mod, w, w_gd)


def _swa_in_proj_kernel(x_ref, g_ref, mod_ref, cos_ref, sin_ref, w_ref, o_ref, a_scr, *, d, n_lat_tiles, n_roped,
                        n_scaled, out_scale):
    _adaln_rows(a_scr, x_ref, g_ref, mod_ref, 0, 1, d)
    y_all = _dot(a_scr[...], w_ref[...])
    is_lat = pl.program_id(0) < n_lat_tiles
    cosf, sinf = cos_ref[...], sin_ref[...]
    first = _first_half_lanes(cosf.shape)
    for c in range(y_all.shape[1] // LANES):
        y = y_all[:, c * LANES:(c + 1) * LANES]
        if c < n_roped:
            y = _rope_chunk(y, cosf, sinf, first, is_lat)
        if c < n_scaled:
            y = y * out_scale
        o_ref[:, c * LANES:(c + 1) * LANES] = y.astype(o_ref.dtype)


def _swa_in_proj(hs, g, mod, w, cosf, sinf, n_roped, n_scaled, out_scale, n_lat_tiles, tiles_per_batch, nb):
    n_rows, d = hs.shape
    n = w.shape[1]
    tm = ROW_TILE
    return pl.pallas_call(
        functools.partial(_swa_in_proj_kernel, d=d, n_lat_tiles=n_lat_tiles, n_roped=n_roped, n_scaled=n_scaled,
                          out_scale=out_scale),
        grid=(n_rows // tm,),
        in_specs=[
            pl.BlockSpec((tm, d), lambda i: (i, 0)),
            pl.BlockSpec((1, d), lambda i: (0, 0)),
            _mod_spec(mod, tiles_per_batch, nb),
            pl.BlockSpec((tm, LANES), lambda i: (i % tiles_per_batch, 0)),
            pl.BlockSpec((tm, LANES), lambda i: (i % tiles_per_batch, 0)),
            pl.BlockSpec((d, n), lambda i: (0, 0)),
        ],
        out_specs=pl.BlockSpec((tm, n), lambda i: (i, 0)),
        out_shape=jax.ShapeDtypeStruct((n_rows, n), BF16),
        scratch_shapes=[pltpu.VMEM((tm, d), BF16)],
        compiler_params=_cparams(("parallel",)),
        name="swa_in_proj",
    )(hs, g.reshape(1, d), mod, cosf, sinf, w)


def _mla_in_proj_kernel(x_ref, g_ref, mod_ref, cos_ref, sin_ref, win_ref, gq_ref, gkv_ref, wuq_ref, wukv_ref,
                        qn_ref, qr_ref, kv_ref, kr_ref, a_scr, *, d, n_lat_tiles, q_scale):
    _adaln_rows(a_scr, x_ref, g_ref, mod_ref, 0, 1, d)
    p1 = _dot(a_scr[...], win_ref[...])
    cq = (_rms(p1[:, 0:MLA_Q_RANK]) * gq_ref[...]).astype(BF16)
    ckv = (_rms(p1[:, MLA_Q_RANK:MLA_Q_RANK + MLA_KV_RANK]) * gkv_ref[...]).astype(BF16)
    q = _dot(cq, wuq_ref[...]) * q_scale
    n_nope = qn_ref.shape[1]
    qn_ref[...] = q[:, 0:n_nope].astype(qn_ref.dtype)
    kv_ref[...] = _dot(ckv, wukv_ref[...]).astype(kv_ref.dtype)
    is_lat = pl.program_id(0) < n_lat_tiles
    cosf, sinf = cos_ref[...], sin_ref[...]
    first = _first_half_lanes(cosf.shape)
    for c in range(qr_ref.shape[1] // LANES):
        y = q[:, n_nope + c * LANES:n_nope + (c + 1) * LANES]
        qr_ref[:, c * LANES:(c + 1) * LANES] = _rope_chunk(y, cosf, sinf, first, is_lat).astype(qr_ref.dtype)
    k_rope = p1[:, MLA_Q_RANK + MLA_KV_RANK:MLA_Q_RANK + MLA_KV_RANK + LANES]
    kr_ref[...] = _rope_chunk(k_rope, cosf, sinf, first, is_lat).astype(kr_ref.dtype)


def _mla_in_proj(hs, g, mod, w_in, g_q, w_uq, g_kv, w_ukv, n_nope, q_scale, cosf, sinf, n_lat_tiles, tiles_per_batch,
                 nb):
    n_rows, d = hs.shape
    tm = ROW_TILE
    n_q, n_kv = w_uq.shape[1], w_ukv.shape[1]
    const = lambda shape: pl.BlockSpec(shape, lambda i: (0, 0))
    row = lambda w: pl.BlockSpec((tm, w), lambda i: (i, 0))
    return pl.pallas_call(
        functools.partial(_mla_in_proj_kernel, d=d, n_lat_tiles=n_lat_tiles, q_scale=q_scale),
        grid=(n_rows // tm,),
        in_specs=[
            row(d), const((1, d)), _mod_spec(mod, tiles_per_batch, nb),
            pl.BlockSpec((tm, LANES), lambda i: (i % tiles_per_batch, 0)),
            pl.BlockSpec((tm, LANES), lambda i: (i % tiles_per_batch, 0)),
            const(w_in.shape), const((1, MLA_Q_RANK)), const((1, MLA_KV_RANK)), const(w_uq.shape), const(w_ukv.shape),
        ],
        out_specs=[row(n_nope), row(n_q - n_nope), row(n_kv), row(LANES)],
        out_shape=[jax.ShapeDtypeStruct((n_rows, n_nope), BF16), jax.ShapeDtypeStruct((n_rows, n_q - n_nope), BF16),
                   jax.ShapeDtypeStruct((n_rows, n_kv), BF16), jax.ShapeDtypeStruct((n_rows, LANES), BF16)],
        scratch_shapes=[pltpu.VMEM((tm, d), BF16)],
        compiler_params=_cparams(("parallel",)),
        name="mla_in_proj",
    )(hs, g.reshape(1, d), mod, cosf, sinf, w_in, g_q.reshape(1, -1), g_kv.reshape(1, -1), w_uq, w_ukv)


def _proj_resid_kernel(a_ref, a2_ref, w_ref, h_ref, g_ref, mod_ref, o_ref, y_scr, *, gate_idx, d, n1):
    def body(src_ref):
        y_scr[...] = _dot(src_ref[...].astype(BF16), w_ref[...])
        _resid_norm_rows(o_ref, h_ref, y_scr, g_ref, mod_ref, gate_idx, d)

    pl.when(pl.program_id(0) < n1)(lambda: body(a_ref))
    pl.when(pl.program_id(0) >= n1)(lambda: body(a2_ref))


def _proj_resid(a, a2, w, hs, g, mod, gate_idx, n_rows, tiles_per_batch, nb, name):
    k, d = w.shape
    tm = ROW_TILE
    n1 = a.shape[0] // tm
    return pl.pallas_call(
        functools.partial(_proj_resid_kernel, gate_idx=gate_idx, d=d, n1=n1),
        grid=(n_rows // tm,),
        in_specs=[
            pl.BlockSpec((tm, k), lambda i: (jnp.minimum(i, n1 - 1), 0)),
            pl.BlockSpec((tm, k), lambda i: (jnp.maximum(i - n1, 0), 0)),
            pl.BlockSpec((k, d), lambda i: (0, 0)),
            pl.BlockSpec((tm, d), lambda i: (i, 0)),
            pl.BlockSpec((1, d), lambda i: (0, 0)),
            _mod_spec(mod, tiles_per_batch, nb),
        ],
        out_specs=pl.BlockSpec((tm, d), lambda i: (i, 0)),
        out_shape=jax.ShapeDtypeStruct((n_rows, d), F32),
        scratch_shapes=[pltpu.VMEM((tm, d), F32)],
        compiler_params=_cparams(("parallel",)),
        name=name,
    )(a, a2, w, hs, g.reshape(1, d), mod)


def _ffn_kernel(h_ref, g2_ref, g3_ref, mod_ref, wg_ref, wu_ref, wo_ref, o_ref, a_scr, *, d):
    j = pl.program_id(1)

    @pl.when(j == 0)
    def _():
        _adaln_rows(a_scr, h_ref, g2_ref, mod_ref, 3, 4, d)
        o_ref[...] = jnp.zeros_like(o_ref)

    a = a_scr[...]
    gt = _dot(a, wg_ref[...])
    up = _dot(a, wu_ref[...])
    act = (gt * _sigmoid(gt) * up).astype(BF16)
    o_ref[...] += _dot(act, wo_ref[...])

    @pl.when(j == pl.num_programs(1) - 1)
    def _():
        _resid_norm_rows(o_ref, h_ref, o_ref, g3_ref, mod_ref, 5, d)


def _ffn(hs, g2, g3, mod, w_in, w_out, layer, n_rows, tm, tiles_per_batch, nb, name):
    d = hs.shape[1]
    f = w_out.shape[1]
    tf = FFN_TILE
    nf = f // tf
    return pl.pallas_call(
        functools.partial(_ffn_kernel, d=d),
        grid=(n_rows // tm, nf),
        in_specs=[
            pl.BlockSpec((tm, d), lambda i, j: (i, 0)),
            pl.BlockSpec((1, d), lambda i, j: (0, 0)),
            pl.BlockSpec((1, d), lambda i, j: (0, 0)),
            _mod_spec(mod, tiles_per_batch, nb),
            pl.BlockSpec((None, d, tf), lambda i, j: (layer, 0, j)),
            pl.BlockSpec((None, d, tf), lambda i, j: (layer, 0, nf + j)),
            pl.BlockSpec((None, tf, d), lambda i, j: (layer, j, 0)),
        ],
        out_specs=pl.BlockSpec((tm, d), lambda i, j: (i, 0)),
        out_shape=jax.ShapeDtypeStruct((n_rows, d), F32),
        scratch_shapes=[pltpu.VMEM((tm, d), BF16)],
        compiler_params=_cparams(("parallel", "arbitrary")),
        name=name,
    )(hs, g2.reshape(1, d), g3.reshape(1, d), mod, w_in, w_in, w_out)


GLA_CHUNK = 128


def _gla_structure(cs, reverse):
    idx = np.arange(cs)
    ip = cs - 1 - idx if reverse else idx
    ii, jj = ip[:, None], ip[None, :]
    masks, dist = [], []
    s = cs // 2
    while s >= 1:
        masks.append(((ii // (2 * s)) == (jj // (2 * s))) & ((ii & s) != 0) & ((jj & s) == 0))
        pp = (ii // (2 * s)) * (2 * s) + s - 1
        dist.append((jj > np.minimum(ii, pp)) & (jj <= np.maximum(ii, pp)))
        s //= 2
    masks.append(ii == jj)
    return (np.stack(masks).astype(np.float32), (jj <= ii).astype(np.float32),
            np.concatenate(dist, axis=0).astype(np.float32))


def _gla_chunk(q_b, k_b, v_b, lg, msk_ref, tri, dist_ref, st_ref, *, cs, reverse):
    hi = lg.astype(BF16)
    lo = (lg - hi.astype(F32)).astype(BF16)
    b = _dot(tri, hi) + _dot(tri, lo)
    yield
    last = 0 if reverse else cs - 1
    b_last = b[last:last + 1, :]

    st = st_ref[...]
    o = _dot_nt(q_b * jnp.exp2(b).astype(BF16), st.astype(BF16))
    yield

    attn = jnp.zeros((cs, cs), F32)
    nlev = dist_ref.shape[0] // cs
    for lvl in range(nlev):
        nd = _dot(dist_ref[lvl * cs:(lvl + 1) * cs, :], hi)
        fac = jnp.exp2(nd).astype(BF16)
        attn = attn + msk_ref[lvl] * _dot_nt(q_b * fac, k_b * fac)
        yield
    attn = attn + msk_ref[nlev] * _dot_nt(q_b, k_b)

    k_dec = k_b * jnp.exp2(b_last - b).astype(BF16)
    st_ref[...] = st * jnp.exp2(b_last) + _dot_tn(v_b, k_dec)
    yield
    return o + _dot(attn.astype(BF16), v_b)


def _interleave(gens):
    results = [None] * len(gens)
    active = list(range(len(gens)))
    while active:
        for idx in list(active):
            try:
                next(gens[idx])
            except StopIteration as stop:
                results[idx] = stop.value
                active.remove(idx)
    return results


def _gla_scan_kernel(qf_ref, kf_ref, vf_ref, gf_ref, qb_ref, kb_ref, vb_ref, gb_ref, wu_ref, bg_ref,
                     msk_ref, tri_ref, dist_ref, of_ref, ob_ref, st_scr, *, cs, dk, dv):
    @pl.when(pl.program_id(1) == 0)
    def _():
        st_scr[...] = jnp.zeros_like(st_scr)

    dirs = ((qf_ref, kf_ref, vf_ref, gf_ref, of_ref), (qb_ref, kb_ref, vb_ref, gb_ref, ob_ref))
    chains, dests = [], []
    for dr, (q_ref, k_ref, v_ref, gd_ref, o_ref) in enumerate(dirs):
        z = _dot(gd_ref[...].astype(BF16), wu_ref[dr]) + bg_ref[dr]
        lg_all = (jnp.minimum(z, 0.0) - jnp.log1p(jnp.exp(-jnp.abs(z)))) * (LOG2E / GLA_TAU)
        for h in range(GLA_HEADS):
            ks = slice(h * dk, (h + 1) * dk)
            vs = slice(h * dv, (h + 1) * dv)
            chains.append(_gla_chunk(q_ref[:, ks] * (dk ** -0.5), k_ref[:, ks], v_ref[:, vs].astype(BF16),
                                     lg_all[:, ks], msk_ref.at[dr], tri_ref[dr], dist_ref.at[dr], st_scr.at[dr, h],
                                     cs=cs, reverse=bool(dr)))
            dests.append((o_ref, vs))
    for (o_ref, vs), o in zip(dests, _interleave(chains)):
        o_ref[:, vs] = o.astype(o_ref.dtype)


def _gla_scan(p, gd, wu_ext, bg, b_sz, t_lat, t_ctx):
    cs = GLA_CHUNK
    qkw = wu_ext.shape[-1]
    dk = qkw // GLA_HEADS
    vw = 2 * qkw
    dv = vw // GLA_HEADS
    n_tok = p.shape[0]
    ncc, ncl = t_ctx // cs, t_lat // cs
    ctx_blk0 = (b_sz * t_lat) // cs
    structs = [_gla_structure(cs, rev) for rev in (False, True)]
    masks, tri, dist = (np.stack(parts) for parts in zip(*structs))

    def rowblk(rev):
        def f(b, s):
            if rev:
                cc, lc = ncc - 1 - s, ncl - 1 - (s - ncc)
            else:
                cc, lc = s, s - ncc
            return jnp.where(s < ncc, ctx_blk0 + b * ncc + cc, b * ncl + lc)
        return f

    def dir_specs(rev):
        rb = rowblk(rev)
        return [pl.BlockSpec((cs, qkw), lambda b, s: (rb(b, s), 0)),
                pl.BlockSpec((cs, qkw), lambda b, s: (rb(b, s), 1)),
                pl.BlockSpec((cs, vw), lambda b, s: (rb(b, s), 1)),
                pl.BlockSpec((cs, LANES), lambda b, s: (rb(b, s), 0))]

    return pl.pallas_call(
        functools.partial(_gla_scan_kernel, cs=cs, dk=dk, dv=dv),
        grid=(b_sz, ncc + ncl),
        in_specs=dir_specs(False) + dir_specs(True) + [
            pl.BlockSpec((2, LANES, qkw), lambda b, s: (0, 0, 0)),
            pl.BlockSpec((2, 1, qkw), lambda b, s: (0, 0, 0)),
            pl.BlockSpec(masks.shape, lambda b, s: (0, 0, 0, 0)),
            pl.BlockSpec(tri.shape, lambda b, s: (0, 0, 0)),
            pl.BlockSpec(dist.shape, lambda b, s: (0, 0, 0)),
        ],
        out_specs=[pl.BlockSpec((cs, vw), lambda b, s: (rowblk(False)(b, s), 0)),
                   pl.BlockSpec((cs, vw), lambda b, s: (rowblk(True)(b, s), 0))],
        out_shape=[jax.ShapeDtypeStruct((n_tok, vw), BF16)] * 2,
        scratch_shapes=[pltpu.VMEM((2, GLA_HEADS, dv, dk), F32)],
        compiler_params=_cparams(("parallel", "arbitrary")),
        name="gla_scan",
    )(p, p, p, gd, p, p, p, gd, wu_ext, bg, jnp.asarray(masks), jnp.asarray(tri, dtype=BF16),
      jnp.asarray(dist, dtype=BF16))


def _gla_out_proj_kernel(of_ref, ob_ref, r_ref, gh_ref, w_ref, h_ref, g_ref, mod_ref, o_ref, a_scr, y_scr,
                         *, gate_idx, d, dv):
    gh = gh_ref[...]
    for r0 in range(0, a_scr.shape[0], NORM_ROWS):
        rows = pl.ds(r0, NORM_ROWS)
        for h in range(GLA_HEADS):
            sl = slice(h * dv, (h + 1) * dv)
            y = _rms(of_ref[rows, sl].astype(F32) + ob_ref[rows, sl].astype(F32)) * gh
            r = r_ref[rows, sl].astype(F32)
            a_scr[rows, sl] = (y * (r * _sigmoid(r))).astype(a_scr.dtype)
    y_scr[...] = _dot(a_scr[...], w_ref[...])
    _resid_norm_rows(o_ref, h_ref, y_scr, g_ref, mod_ref, gate_idx, d)


def _gla_out_proj(o_f, o_b, p, g_head, w, hs, g, mod, gate_idx, n_rows, tiles_per_batch, nb):
    d = o_f.shape[1]
    dv = d // GLA_HEADS
    tm = ROW_TILE
    rcol = p.shape[1] // d - 1
    row = lambda col: pl.BlockSpec((tm, d), lambda i: (i, col))
    return pl.pallas_call(
        functools.partial(_gla_out_proj_kernel, gate_idx=gate_idx, d=d, dv=dv),
        grid=(n_rows // tm,),
        in_specs=[
            row(0), row(0), row(rcol),
            pl.BlockSpec((1, dv), lambda i: (0, 0)),
            pl.BlockSpec((d, d), lambda i: (0, 0)),
            row(0),
            pl.BlockSpec((1, d), lambda i: (0, 0)),
            _mod_spec(mod, tiles_per_batch, nb),
        ],
        out_specs=row(0),
        out_shape=jax.ShapeDtypeStruct((n_rows, d), F32),
        scratch_shapes=[pltpu.VMEM((tm, d), BF16), pltpu.VMEM((tm, d), F32)],
        compiler_params=_cparams(("parallel",)),
        name="gla_out_proj",
    )(o_f, o_b, p, g_head.reshape(1, dv), w, hs, g.reshape(1, d), mod)


def _rope_tables(t_lat):
    t = jnp.arange(t_lat)
    row = (t // GRID_W).astype(F32)
    col = (t % GRID_W).astype(F32)
    n_freq = ROPE_DIM // 4
    inv_freq = ROPE_THETA ** (-jnp.arange(n_freq, dtype=F32) / n_freq)
    ang = jnp.concatenate([row[:, None] * inv_freq, col[:, None] * inv_freq], axis=-1)
    cos, sin = jnp.cos(ang), jnp.sin(ang)
    reps = LANES // ROPE_DIM
    return jnp.tile(jnp.concatenate([cos, cos], axis=-1), (1, reps)), jnp.tile(jnp.concatenate([-sin, sin], axis=-1), (1, reps))


MLA_TQ = 512
MLA_TK = 1024
MLA_HEADS_PER_STEP = 2


def _mla_attn_kernel(qn_ref, qr_ref, qnc_ref, qrc_ref, knc_ref, vc_ref, krc_ref, knl_ref, vl_ref, krl_ref,
                     o_ref, oc_ref, kcat, vext, *, tc, tl):
    t = pl.program_id(2)
    nh = MLA_HEADS_PER_STEP
    rd = ROPE_DIM
    vw = MLA_V + LANES

    def qcat(n_ref, r_ref, h):
        return jnp.concatenate([n_ref[:, h * MLA_NOPE:(h + 1) * MLA_NOPE], r_ref[:, h * rd:(h + 1) * rd]], axis=1)

    def finish(acc):
        return acc[:, 0:MLA_V] / acc[:, MLA_V:vw]

    @pl.when(t == 0)
    def _():
        for h in range(nh):
            ns = slice(h * MLA_NOPE, (h + 1) * MLA_NOPE)
            kcat[h, 0:tc, 0:MLA_NOPE] = knc_ref[:, ns]
            kcat[h, 0:tc, MLA_NOPE:MLA_NOPE + rd] = krc_ref[:, 0:rd]
            kcat[h, tc:tc + tl, 0:MLA_NOPE] = knl_ref[:, ns]
            kcat[h, tc:tc + tl, MLA_NOPE:MLA_NOPE + rd] = krl_ref[:, 0:rd]
            vext[0:tc, h * vw:h * vw + MLA_V] = vc_ref[:, h * MLA_V:(h + 1) * MLA_V]
            vext[tc:tc + tl, h * vw:h * vw + MLA_V] = vl_ref[:, h * MLA_V:(h + 1) * MLA_V]
            vext[:, h * vw + MLA_V:(h + 1) * vw] = jnp.ones((tc + tl, LANES), BF16)
        for h in range(nh):
            s = _dot_nt(qcat(qnc_ref, qrc_ref, h), kcat[h, 0:tc, :])
            p = jnp.exp2(s - jnp.max(s, axis=-1, keepdims=True))
            acc = _dot(p.astype(BF16), vext[0:tc, h * vw:(h + 1) * vw])
            oc_ref[:, h * MLA_V:(h + 1) * MLA_V] = finish(acc).astype(oc_ref.dtype)

    chunks = [(0, tc)] + [(c0, c0 + MLA_TK) for c0 in range(tc, tc + tl, MLA_TK)]
    units = [(h, c0, c1) for h in range(nh) for (c0, c1) in chunks]
    qcs = [qcat(qn_ref, qr_ref, h) for h in range(nh)]

    def scores(unit):
        h, c0, c1 = unit
        return _dot_nt(qcs[h], kcat[h, c0:c1, :])

    s_next = scores(units[0])
    m = acc = None
    for idx, (h, c0, c1) in enumerate(units):
        s = s_next
        if idx + 1 < len(units):
            s_next = scores(units[idx + 1])
        s_max = jnp.max(s, axis=-1, keepdims=True)
        m_new = s_max if c0 == 0 else jnp.maximum(m, s_max)
        pv = _dot(jnp.exp2(s - m_new).astype(BF16), vext[c0:c1, h * vw:(h + 1) * vw])
        acc = pv if c0 == 0 else jnp.exp2(m - m_new) * acc + pv
        m = m_new
        if c1 == tc + tl:
            o_ref[:, h * MLA_V:(h + 1) * MLA_V] = finish(acc).astype(o_ref.dtype)


def _mla_attention(qn, qr, kv, kr, b_sz, t_lat, t_ctx):
    hd = qn.shape[1]
    nh = MLA_HEADS_PER_STEP
    n_heads = hd // MLA_NOPE
    tq = MLA_TQ
    assert t_lat % tq == 0 and t_lat % MLA_TK == 0
    nlt = t_lat // tq
    ctx_blk0 = (b_sz * t_lat) // t_ctx
    vcol0 = n_heads // nh
    wn, wr = nh * MLA_NOPE, nh * ROPE_DIM
    return pl.pallas_call(
        functools.partial(_mla_attn_kernel, tc=t_ctx, tl=t_lat),
        grid=(b_sz, n_heads // nh, nlt),
        in_specs=[
            pl.BlockSpec((tq, wn), lambda b, h, t: (b * nlt + t, h)),
            pl.BlockSpec((tq, wr), lambda b, h, t: (b * nlt + t, h)),
            pl.BlockSpec((t_ctx, wn), lambda b, h, t: (ctx_blk0 + b, h)),
            pl.BlockSpec((t_ctx, wr), lambda b, h, t: (ctx_blk0 + b, h)),
            pl.BlockSpec((t_ctx, wn), lambda b, h, t: (ctx_blk0 + b, h)),
            pl.BlockSpec((t_ctx, wn), lambda b, h, t: (ctx_blk0 + b, vcol0 + h)),
            pl.BlockSpec((t_ctx, LANES), lambda b, h, t: (ctx_blk0 + b, 0)),
            pl.BlockSpec((t_lat, wn), lambda b, h, t: (b, h)),
            pl.BlockSpec((t_lat, wn), lambda b, h, t: (b, vcol0 + h)),
            pl.BlockSpec((t_lat, LANES), lambda b, h, t: (b, 0)),
        ],
        out_specs=[pl.BlockSpec((tq, nh * MLA_V), lambda b, h, t: (b * nlt + t, h)),
                   pl.BlockSpec((t_ctx, nh * MLA_V), lambda b, h, t: (b, h))],
        out_shape=[jax.ShapeDtypeStruct((b_sz * t_lat, n_heads * MLA_V), BF16),
                   jax.ShapeDtypeStruct((b_sz * t_ctx, n_heads * MLA_V), BF16)],
        scratch_shapes=[pltpu.VMEM((nh, t_ctx + t_lat, MLA_NOPE + ROPE_DIM), BF16),
                        pltpu.VMEM((t_ctx + t_lat, nh * (MLA_V + LANES)), BF16)],
        compiler_params=_cparams(("parallel", "parallel", "arbitrary")),
        name="mla_attention",
    )(qn, qr, qn, qr, kv, kv, kr, kv, kv, kr)


def _swa_attn_kernel(sink_ref, q_ref, kc_ref, vc_ref, kp_ref, kc0_ref, kn_ref, vp_ref, vc0_ref, vn_ref, o_ref,
                     *, nct, nb, n_kv):
    t = pl.program_id(1)
    n = t - nct
    blk = WINDOW
    hd = SWA_HEAD_DIM
    tc = kc_ref.shape[0]
    nk = tc + 3 * blk
    npair = SWA_GROUP // 2
    upair = 1

    qi = lax.broadcasted_iota(jnp.int32, (blk, 3 * blk), 0)
    kj = lax.broadcasted_iota(jnp.int32, (blk, 3 * blk), 1)
    lo = jnp.where(n > 0, 0, blk)
    hi = jnp.where(t < nct, 0, jnp.where(n < nb - 1, 3 * blk, 2 * blk))
    win_ok = (jnp.abs(qi - kj + blk) <= WINDOW) & (kj >= lo) & (kj < hi)
    bias = jnp.concatenate([jnp.zeros((blk, tc), F32), jnp.where(win_ok, 0.0, NEG_BIG)], axis=1)
    bias = jnp.concatenate([bias] * upair, axis=0)
    lane = lax.broadcasted_iota(jnp.int32, (blk, 2 * hd), 1)
    zeros = jnp.zeros((nk, hd), BF16)
    ones = jnp.ones((nk, hd), BF16)
    tail = tc + 2 * blk
    shared = {}

    def kv_operands(kvh):
        if kvh not in shared:
            ks = slice(kvh * hd, (kvh + 1) * hd)
            k_all = jnp.concatenate([kc_ref[:, ks], kp_ref[:, ks], kc0_ref[:, ks], kn_ref[:, ks]], axis=0)
            v_all = jnp.concatenate([vc_ref[:, ks], vp_ref[:, ks], vc0_ref[:, ks], vn_ref[:, ks]], axis=0)
            k_bd = jnp.concatenate([jnp.concatenate([k_all, zeros], axis=1),
                                    jnp.concatenate([zeros, k_all], axis=1)], axis=0)

            def v_ext(r0, r1, e):
                v, z, o = v_all[r0:r1], zeros[r0:r1], ones[r0:r1]
                return jnp.concatenate([z, v, z, o] if e else [v, z, o, z], axis=1)

            v_tail = jnp.concatenate([v_ext(tail, nk, 0), v_ext(tail, nk, 1)], axis=0)
            v_main = [(v_ext(0, tc, e), v_ext(tc, tail, e)) for e in range(2)]
            shared[kvh] = (k_bd, v_tail, v_main)
        return shared[kvh]

    def unit_chain(kvh, u):
        k_bd, v_tail, v_main = kv_operands(kvh)
        h0 = kvh * SWA_GROUP + 2 * upair * u
        q_st = jnp.concatenate([q_ref[:, (h0 + 2 * j) * hd:(h0 + 2 * j + 2) * hd] for j in range(upair)], axis=0)
        s = _dot_nt(q_st, k_bd)
        yield
        ps, ms = [], []
        for e in range(2):
            sink_col = jnp.concatenate([jnp.full((blk, 1), sink_ref[h0 + 2 * j + e] * LOG2E, F32)
                                        for j in range(upair)], axis=0)
            s_e = s[:, e * nk:(e + 1) * nk] + bias
            m = jnp.maximum(jnp.max(s_e, axis=-1, keepdims=True), sink_col)
            p = jnp.exp2(s_e - m).astype(BF16)
            ps += [p[:, 0:tc], p[:, tc:]]
            ms.append(sink_col - m)
        acc = _dot(jnp.concatenate([ps[1][:, 2 * blk:], ps[3][:, 2 * blk:]], axis=1), v_tail)
        for e in range(2):
            acc = acc + _dot(ps[2 * e], v_main[e][0]) + _dot(ps[2 * e + 1][:, 0:2 * blk], v_main[e][1])
        for j in range(upair):
            r = slice(j * blk, (j + 1) * blk)
            sink_term = jnp.exp2(jnp.where(lane < hd, ms[0][r], ms[1][r]))
            o = acc[r, 0:2 * hd] / (acc[r, 2 * hd:4 * hd] + sink_term)
            o_ref[:, (h0 + 2 * j) * hd:(h0 + 2 * j + 2) * hd] = o.astype(o_ref.dtype)

    chains = [unit_chain(kvh, u) for kvh in range(n_kv) for u in range(npair // upair)]
    next(chains[0])
    for n_unit, chain in enumerate(chains):
        if n_unit + 1 < len(chains):
            next(chains[n_unit + 1])
        for _ in chain:
            pass


def _swa_attention(qkv, sinks, b_sz, t_lat, t_ctx):
    n_tok = qkv.shape[0]
    blk = WINDOW
    hd = SWA_HEAD_DIM
    n_q = sinks.shape[0]
    n_kv = n_q // SWA_GROUP
    kvw = n_kv * hd
    kcol = (n_q * hd) // kvw
    vcol = kcol + 1
    nct, nb = t_ctx // blk, t_lat // blk
    ctx_blk0 = (b_sz * t_lat) // blk
    ctx_row0 = (b_sz * t_lat) // t_ctx

    def qrow(b, t):
        return jnp.where(t < nct, ctx_blk0 + b * nct + t, b * nb + t - nct)

    def krow(b, t, off):
        return b * nb + jnp.clip(t - nct + off, 0, nb - 1)

    kspec = lambda off: pl.BlockSpec((blk, kvw), lambda b, t: (krow(b, t, off), kcol))
    vspec = lambda off: pl.BlockSpec((blk, kvw), lambda b, t: (krow(b, t, off), vcol))
    return pl.pallas_call(
        functools.partial(_swa_attn_kernel, nct=nct, nb=nb, n_kv=n_kv),
        grid=(b_sz, nct + nb),
        in_specs=[
            pl.BlockSpec(memory_space=pltpu.SMEM),
            pl.BlockSpec((blk, n_q * hd), lambda b, t: (qrow(b, t), 0)),
            pl.BlockSpec((t_ctx, kvw), lambda b, t: (ctx_row0 + b, kcol)),
            pl.BlockSpec((t_ctx, kvw), lambda b, t: (ctx_row0 + b, vcol)),
            kspec(-1), kspec(0), kspec(1), vspec(-1), vspec(0), vspec(1),
        ],
        out_specs=pl.BlockSpec((blk, n_q * hd), lambda b, t: (qrow(b, t), 0)),
        out_shape=jax.ShapeDtypeStruct((n_tok, n_q * hd), BF16),
        compiler_params=_cparams(("parallel", "arbitrary")),
        name="swa_attention",
    )(sinks, qkv, qkv, qkv, qkv, qkv, qkv, qkv, qkv, qkv)


def kernel(x, c, ctx, c_ctx, w_ada, b_ada, g_norm, w_ffn_in, w_ffn_out, gla_w_in, gla_w_gate_down, gla_w_gate_up, gla_b_gate, gla_g_head, gla_w_out, mla_w_in, mla_g_q, mla_w_uq, mla_g_kv, mla_w_ukv, mla_w_out, swa_w_in, swa_sinks, swa_w_out):
    b_sz, t_lat, d = x.shape
    t_ctx = ctx.shape[1]
    depth = w_ada.shape[0]
    n_lat, n_ctx = b_sz * t_lat, b_sz * t_ctx
    n_tok = n_lat + n_ctx
    tm = ROW_TILE
    assert t_lat % tm == 0 and n_ctx % tm == 0 and b_sz < MOD_ROWS
    tpb = t_lat // tm
    n_lat_tiles = n_lat // tm

    hs = jnp.concatenate([x.reshape(n_lat, d), ctx.reshape(n_ctx, d)], axis=0)
    cond = jnp.zeros((MOD_ROWS, d), F32).at[:b_sz].set(c).at[b_sz].set(c_ctx)
    mod_all = _modulation(cond, w_ada, b_ada)
    cosf, sinf = _rope_tables(t_lat)
    common = dict(tiles_per_batch=tpb, nb=b_sz)
    tm_big = 2 * tm if t_lat % (2 * tm) == 0 and n_ctx % (2 * tm) == 0 else tm
    w_ffn_in_b, w_ffn_out_b = w_ffn_in.astype(BF16), w_ffn_out.astype(BF16)

    for i in range(depth):
        kind, j = i % 3, i // 3
        last = i == depth - 1
        n_out = n_lat if last else n_tok
        mod = mod_all[i].reshape(MOD_ROWS, 1, 6 * d)

        if kind == 0:
            rank = GLA_GATE_RANK
            qk_w = gla_w_gate_up.shape[-1]
            pad = LANES - 2 * rank
            w_gd = jnp.concatenate([gla_w_gate_down[j, 0], gla_w_gate_down[j, 1], jnp.zeros((d, pad), F32)],
                                   axis=1).astype(BF16)
            p, gd = _gla_in_proj(hs, g_norm[i, 0], mod, gla_w_in[j].astype(BF16), w_gd, tm_big, t_lat // tm_big, b_sz)
            wu_ext = jnp.zeros((2, LANES, qk_w), F32)
            for dr in range(2):
                wu_ext = wu_ext.at[dr, dr * rank:(dr + 1) * rank].set(gla_w_gate_up[j, dr])
            o_f, o_b = _gla_scan(p, gd, wu_ext.astype(BF16), gla_b_gate[j].reshape(2, 1, qk_w), b_sz, t_lat, t_ctx)
            hs = _gla_out_proj(o_f, o_b, p, gla_g_head[j], gla_w_out[j].astype(BF16), hs, g_norm[i, 1], mod, 2, n_out,
                               **common)
        elif kind == 1:
            n_heads = mla_w_out.shape[1] // MLA_V
            w_in = jnp.concatenate([mla_w_in[j], jnp.zeros((d, LANES - ROPE_DIM), F32)], axis=1).astype(BF16)
            w_uq = mla_w_uq[j].reshape(MLA_Q_RANK, n_heads, MLA_NOPE + ROPE_DIM)
            w_uq = jnp.concatenate([w_uq[:, :, :MLA_NOPE].reshape(MLA_Q_RANK, -1),
                                    w_uq[:, :, MLA_NOPE:].reshape(MLA_Q_RANK, -1)], axis=1).astype(BF16)
            w_ukv = mla_w_ukv[j].reshape(MLA_KV_RANK, n_heads, MLA_NOPE + MLA_V)
            w_ukv = jnp.concatenate([w_ukv[:, :, :MLA_NOPE].reshape(MLA_KV_RANK, -1),
                                     w_ukv[:, :, MLA_NOPE:].reshape(MLA_KV_RANK, -1)], axis=1).astype(BF16)
            q_scale = (MLA_NOPE + ROPE_DIM) ** -0.5 * LOG2E
            qn, qr, kv, kr = _mla_in_proj(hs, g_norm[i, 0], mod, w_in, mla_g_q[j], w_uq, mla_g_kv[j], w_ukv,
                                          n_heads * MLA_NOPE, q_scale, cosf, sinf, n_lat_tiles, **common)
            a, a2 = _mla_attention(qn, qr, kv, kr, b_sz, t_lat, t_ctx)
            w_o = mla_w_out[j]
        else:
            n_q = swa_sinks.shape[1]
            qkw = (n_q + n_q // SWA_GROUP) * SWA_HEAD_DIM
            qkv = _swa_in_proj(hs, g_norm[i, 0], mod, swa_w_in[j].astype(BF16), cosf, sinf, qkw // LANES,
                               (n_q * SWA_HEAD_DIM) // LANES, SWA_HEAD_DIM ** -0.5 * LOG2E, n_lat_tiles, **common)
            a = a2 = _swa_attention(qkv, swa_sinks[j], b_sz, t_lat, t_ctx)
            w_o = swa_w_out[j]

        if kind != 0:
            hs = _proj_resid(a, a2, w_o.astype(BF16), hs, g_norm[i, 1], mod, 2, n_out, name="mixer_out_proj", **common)
        hs = _ffn(hs, g_norm[i, 2], g_norm[i, 3], mod, w_ffn_in_b, w_ffn_out_b, i, n_out, tm_big, t_lat // tm_big, b_sz,
                  name="ffn")

    return hs.reshape(b_sz, t_lat, d)
```

```python
import functools

import numpy as np
import jax
import jax.numpy as jnp
from jax import lax
from jax.experimental import pallas as pl
from jax.experimental.pallas import tpu as pltpu

F32 = jnp.float32
BF16 = jnp.bfloat16

EPS = 1e-6
ROPE_THETA = 10000.0
ROPE_DIM = 64
GRID_W = 64
WINDOW = 128
GLA_HEADS = 4
GLA_GATE_RANK = 16
GLA_TAU = 16.0
MLA_Q_RANK = 512
MLA_KV_RANK = 512
MLA_NOPE = 128
MLA_V = 128
SWA_HEAD_DIM = 64
SWA_GROUP = 8

LANES = 128
VMEM_LIMIT = 56 * 1024 * 1024
ROW_TILE = 512
COL_TILE = 1024
FFN_TILE = 512
NORM_ROWS = 16
MOD_ROWS = 16
NEG_BIG = -1e30
LOG2E = 1.4426950408889634


def _cparams(sem):
    return pltpu.CompilerParams(dimension_semantics=sem, vmem_limit_bytes=VMEM_LIMIT)


def _dot(a, b):
    return jnp.dot(a, b, preferred_element_type=F32)


def _dot_nt(a, b):
    return lax.dot_general(a, b, (((1,), (1,)), ((), ())), preferred_element_type=F32)


def _dot_tn(a, b):
    return lax.dot_general(a, b, (((0,), (0,)), ((), ())), preferred_element_type=F32)


def _sigmoid(x):
    return 1.0 / (1.0 + jnp.exp(-x))


def _rms(x):
    return x * lax.rsqrt(jnp.mean(x * x, axis=-1, keepdims=True) + EPS)


def _mod_kernel(c_ref, w_ref, b_ref, o_ref):
    c = c_ref[...]
    s = (c * _sigmoid(c)).astype(BF16)
    o_ref[0] = _dot(s, w_ref[0].astype(BF16)) + b_ref[0]


def _modulation(cond, w_ada, b_ada):
    depth, d, n = w_ada.shape
    tn = COL_TILE
    return pl.pallas_call(
        _mod_kernel,
        grid=(depth, n // tn),
        in_specs=[
            pl.BlockSpec((MOD_ROWS, d), lambda l, j: (0, 0)),
            pl.BlockSpec((1, d, tn), lambda l, j: (l, 0, j)),
            pl.BlockSpec((1, 1, tn), lambda l, j: (l, 0, j)),
        ],
        out_specs=pl.BlockSpec((1, MOD_ROWS, tn), lambda l, j: (l, 0, j)),
        out_shape=jax.ShapeDtypeStruct((depth, MOD_ROWS, n), F32),
        compiler_params=_cparams(("parallel", "parallel")),
        name="modulation",
    )(cond, w_ada, b_ada.reshape(depth, 1, n))


def _adaln_rows(dst_ref, x_ref, g_ref, mod_ref, shift_idx, scale_idx, d):
    m = mod_ref[0]
    gs = g_ref[...] * (1.0 + m[:, scale_idx * d:(scale_idx + 1) * d])
    sh = m[:, shift_idx * d:(shift_idx + 1) * d]
    for r in range(0, x_ref.shape[0], NORM_ROWS):
        rows = pl.ds(r, NORM_ROWS)
        dst_ref[rows, :] = (_rms(x_ref[rows, :]) * gs + sh).astype(dst_ref.dtype)


def _resid_norm_rows(o_ref, h_ref, y_ref, g_ref, mod_ref, gate_idx, d):
    gg = mod_ref[0][:, gate_idx * d:(gate_idx + 1) * d] * g_ref[...]
    for r in range(0, h_ref.shape[0], NORM_ROWS):
        rows = pl.ds(r, NORM_ROWS)
        o_ref[rows, :] = h_ref[rows, :] + _rms(y_ref[rows, :]) * gg


def _rope_chunk(y, cosf, sinf, first, is_lat):
    partner = jnp.where(first, pltpu.roll(y, LANES - ROPE_DIM // 2, 1), pltpu.roll(y, ROPE_DIM // 2, 1))
    return jnp.where(is_lat, y * cosf + partner * sinf, y)


def _first_half_lanes(shape):
    lane = lax.broadcasted_iota(jnp.int32, shape, 1)
    return (lane % ROPE_DIM) < (ROPE_DIM // 2)


def _mod_spec(mod, tiles_per_batch, nb):
    return pl.BlockSpec((1, 1, mod.shape[-1]), lambda i, *_: (jnp.minimum(i // tiles_per_batch, nb), 0, 0))


def _segment_specs(segs, tm, width):
    n1 = segs[0].shape[0] // tm
    if len(segs) == 1:
        return n1, [pl.BlockSpec((tm, width), lambda i, *_: (i, 0))]
    return n1, [pl.BlockSpec((tm, width), lambda i, *_: (jnp.minimum(i, n1 - 1), 0)),
                pl.BlockSpec((tm, width), lambda i, *_: (jnp.maximum(i - n1, 0), 0))]


def _on_segment(n1, seg_refs, fn, pred=None):
    i = pl.program_id(0)
    if len(seg_refs) == 1:
        fn(seg_refs[0]) if pred is None else pl.when(pred)(lambda: fn(seg_refs[0]))
        return
    in_first = i < n1
    pl.when(in_first if pred is None else in_first & pred)(lambda: fn(seg_refs[0]))
    pl.when(~in_first if pred is None else ~in_first & pred)(lambda: fn(seg_refs[1]))


def _gla_in_proj_kernel(*refs, d, n1, n_seg):
    x_refs = refs[:n_seg]
    g_ref, mod_ref, w_ref, wgd_ref, o_ref, gd_ref, a_scr = refs[n_seg:]
    first = pl.program_id(1) == 0
    _on_segment(n1, x_refs, lambda x_ref: _adaln_rows(a_scr, x_ref, g_ref, mod_ref, 0, 1, d), first)

    @pl.when(first)
    def _():
        gd_ref[...] = _dot(a_scr[...], wgd_ref[...])

    o_ref[...] = _dot(a_scr[...], w_ref[...]).astype(o_ref.dtype)


def _gla_in_proj(hs_segs, n_rows, g, mod, w, w_gd, tm, tiles_per_batch, nb):
    d = hs_segs[0].shape[1]
    n = w.shape[1]
    tn = COL_TILE
    n_side = w_gd.shape[1]
    n1, seg_specs = _segment_specs(hs_segs, tm, d)
    return pl.pallas_call(
        functools.partial(_gla_in_proj_kernel, d=d, n1=n1, n_seg=len(hs_segs)),
        grid=(n_rows // tm, n // tn),
        in_specs=seg_specs + [
            pl.BlockSpec((1, d), lambda i, j: (0, 0)),
            _mod_spec(mod, tiles_per_batch, nb),
            pl.BlockSpec((d, tn), lambda i, j: (0, j)),
            pl.BlockSpec((d, n_side), lambda i, j: (0, 0)),
        ],
        out_specs=[pl.BlockSpec((tm, tn), lambda i, j: (i, j)),
                   pl.BlockSpec((tm, n_side), lambda i, j: (i, 0))],
        out_shape=[jax.ShapeDtypeStruct((n_rows, n), BF16),
                   jax.ShapeDtypeStruct((n_rows, n_side), F32)],
        scratch_shapes=[pltpu.VMEM((tm, d), BF16)],
        compiler_params=_cparams(("parallel", "arbitrary")),
        name="gla_in_proj",
    )(*hs_segs, g.reshape(1, d), mod, w, w_gd)


def _swa_in_proj_kernel(x_ref, g_ref, mod_ref, cos_ref, sin_ref, w_ref, o_ref, a_scr, *, d, n_lat_tiles, n_roped,
                        n_scaled, out_scale):
    _adaln_rows(a_scr, x_ref, g_ref, mod_ref, 0, 1, d)
    y_all = _dot(a_scr[...], w_ref[...])
    is_lat = pl.program_id(0) < n_lat_tiles
    cosf, sinf = cos_ref[...], sin_ref[...]
    first = _first_half_lanes(cosf.shape)
    for c in range(y_all.shape[1] // LANES):
        y = y_all[:, c * LANES:(c + 1) * LANES]
        if c < n_roped:
            y = _rope_chunk(y, cosf, sinf, first, is_lat)
        if c < n_scaled:
            y = y * out_scale
        o_ref[:, c * LANES:(c + 1) * LANES] = y.astype(o_ref.dtype)


def _swa_in_proj(hs, g, mod, w, cosf, sinf, n_roped, n_scaled, out_scale, n_lat_tiles, tiles_per_batch, nb):
    n_rows, d = hs.shape
    n = w.shape[1]
    tm = ROW_TILE
    return pl.pallas_call(
        functools.partial(_swa_in_proj_kernel, d=d, n_lat_tiles=n_lat_tiles, n_roped=n_roped, n_scaled=n_scaled,
                          out_scale=out_scale),
        grid=(n_rows // tm,),
        in_specs=[
            pl.BlockSpec((tm, d), lambda i: (i, 0)),
            pl.BlockSpec((1, d), lambda i: (0, 0)),
            _mod_spec(mod, tiles_per_batch, nb),
            pl.BlockSpec((tm, LANES), lambda i: (i % tiles_per_batch, 0)),
            pl.BlockSpec((tm, LANES), lambda i: (i % tiles_per_batch, 0)),
            pl.BlockSpec((d, n), lambda i: (0, 0)),
        ],
        out_specs=pl.BlockSpec((tm, n), lambda i: (i, 0)),
        out_shape=jax.ShapeDtypeStruct((n_rows, n), BF16),
        scratch_shapes=[pltpu.VMEM((tm, d), BF16)],
        compiler_params=_cparams(("parallel",)),
        name="swa_in_proj",
    )(hs, g.reshape(1, d), mod, cosf, sinf, w)


def _mla_in_proj_kernel(x_ref, g_ref, mod_ref, cos_ref, sin_ref, win_ref, gq_ref, gkv_ref, wuq_ref, wukv_ref,
                        qn_ref, qr_ref, kv_ref, kr_ref, a_scr, *, d, n_lat_tiles, q_scale):
    _adaln_rows(a_scr, x_ref, g_ref, mod_ref, 0, 1, d)
    p1 = _dot(a_scr[...], win_ref[...])
    cq = (_rms(p1[:, 0:MLA_Q_RANK]) * gq_ref[...]).astype(BF16)
    ckv = (_rms(p1[:, MLA_Q_RANK:MLA_Q_RANK + MLA_KV_RANK]) * gkv_ref[...]).astype(BF16)
    q = _dot(cq, wuq_ref[...]) * q_scale
    n_nope = qn_ref.shape[1]
    qn_ref[...] = q[:, 0:n_nope].astype(qn_ref.dtype)
    kv_ref[...] = _dot(ckv, wukv_ref[...]).astype(kv_ref.dtype)
    is_lat = pl.program_id(0) < n_lat_tiles
    cosf, sinf = cos_ref[...], sin_ref[...]
    first = _first_half_lanes(cosf.shape)
    for c in range(qr_ref.shape[1] // LANES):
        y = q[:, n_nope + c * LANES:n_nope + (c + 1) * LANES]
        qr_ref[:, c * LANES:(c + 1) * LANES] = _rope_chunk(y, cosf, sinf, first, is_lat).astype(qr_ref.dtype)
    k_rope = p1[:, MLA_Q_RANK + MLA_KV_RANK:MLA_Q_RANK + MLA_KV_RANK + LANES]
    kr_ref[...] = _rope_chunk(k_rope, cosf, sinf, first, is_lat).astype(kr_ref.dtype)


def _mla_in_proj(hs, g, mod, w_in, g_q, w_uq, g_kv, w_ukv, n_nope, q_scale, cosf, sinf, n_lat_tiles, tiles_per_batch,
                 nb):
    n_rows, d = hs.shape
    tm = ROW_TILE
    n_q, n_kv = w_uq.shape[1], w_ukv.shape[1]
    const = lambda shape: pl.BlockSpec(shape, lambda i: (0, 0))
    row = lambda w: pl.BlockSpec((tm, w), lambda i: (i, 0))
    return pl.pallas_call(
        functools.partial(_mla_in_proj_kernel, d=d, n_lat_tiles=n_lat_tiles, q_scale=q_scale),
        grid=(n_rows // tm,),
        in_specs=[
            row(d), const((1, d)), _mod_spec(mod, tiles_per_batch, nb),
            pl.BlockSpec((tm, LANES), lambda i: (i % tiles_per_batch, 0)),
            pl.BlockSpec((tm, LANES), lambda i: (i % tiles_per_batch, 0)),
            const(w_in.shape), const((1, MLA_Q_RANK)), const((1, MLA_KV_RANK)), const(w_uq.shape), const(w_ukv.shape),
        ],
        out_specs=[row(n_nope), row(n_q - n_nope), row(n_kv), row(LANES)],
        out_shape=[jax.ShapeDtypeStruct((n_rows, n_nope), BF16), jax.ShapeDtypeStruct((n_rows, n_q - n_nope), BF16),
                   jax.ShapeDtypeStruct((n_rows, n_kv), BF16), jax.ShapeDtypeStruct((n_rows, LANES), BF16)],
        scratch_shapes=[pltpu.VMEM((tm, d), BF16)],
        compiler_params=_cparams(("parallel",)),
        name="mla_in_proj",
    )(hs, g.reshape(1, d), mod, cosf, sinf, w_in, g_q.reshape(1, -1), g_kv.reshape(1, -1), w_uq, w_ukv)


def _proj_resid_kernel(*refs, gate_idx, d, n1, n_seg):
    a_refs = refs[:n_seg]
    w_ref, h_ref, g_ref, mod_ref, o_ref = refs[n_seg:]

    def body(src_ref):
        o_ref[...] = _dot(src_ref[...].astype(BF16), w_ref[...])
        _resid_norm_rows(o_ref, h_ref, o_ref, g_ref, mod_ref, gate_idx, d)

    _on_segment(n1, a_refs, body)


def _proj_resid(a_segs, w, hs, g, mod, gate_idx, n_rows, tiles_per_batch, nb, name):
    k, d = w.shape
    tm = ROW_TILE
    n1, seg_specs = _segment_specs(a_segs, tm, k)
    return pl.pallas_call(
        functools.partial(_proj_resid_kernel, gate_idx=gate_idx, d=d, n1=n1, n_seg=len(a_segs)),
        grid=(n_rows // tm,),
        in_specs=seg_specs + [
            pl.BlockSpec((k, d), lambda i: (0, 0)),
            pl.BlockSpec((tm, d), lambda i: (i, 0)),
            pl.BlockSpec((1, d), lambda i: (0, 0)),
            _mod_spec(mod, tiles_per_batch, nb),
        ],
        out_specs=pl.BlockSpec((tm, d), lambda i: (i, 0)),
        out_shape=jax.ShapeDtypeStruct((n_rows, d), F32),
        compiler_params=_cparams(("parallel",)),
        name=name,
    )(*a_segs, w, hs, g.reshape(1, d), mod)


def _ffn_kernel(h_ref, g2_ref, g3_ref, mod_ref, wg_ref, wu_ref, wo_ref, o_ref, a_scr, *, d):
    j = pl.program_id(1)

    @pl.when(j == 0)
    def _():
        _adaln_rows(a_scr, h_ref, g2_ref, mod_ref, 3, 4, d)
        o_ref[...] = jnp.zeros_like(o_ref)

    a = a_scr[...]
    gt = _dot(a, wg_ref[...])
    up = _dot(a, wu_ref[...])
    act = (gt * _sigmoid(gt) * up).astype(BF16)
    o_ref[...] += _dot(act, wo_ref[...])

    @pl.when(j == pl.num_programs(1) - 1)
    def _():
        _resid_norm_rows(o_ref, h_ref, o_ref, g3_ref, mod_ref, 5, d)


def _ffn(hs, g2, g3, mod, w_in, w_out, layer, n_rows, tm, tiles_per_batch, nb, name):
    d = hs.shape[1]
    f = w_out.shape[1]
    tf = FFN_TILE
    nf = f // tf
    return pl.pallas_call(
        functools.partial(_ffn_kernel, d=d),
        grid=(n_rows // tm, nf),
        in_specs=[
            pl.BlockSpec((tm, d), lambda i, j: (i, 0)),
            pl.BlockSpec((1, d), lambda i, j: (0, 0)),
            pl.BlockSpec((1, d), lambda i, j: (0, 0)),
            _mod_spec(mod, tiles_per_batch, nb),
            pl.BlockSpec((None, d, tf), lambda i, j: (layer, 0, j)),
            pl.BlockSpec((None, d, tf), lambda i, j: (layer, 0, nf + j)),
            pl.BlockSpec((None, tf, d), lambda i, j: (layer, j, 0)),
        ],
        out_specs=pl.BlockSpec((tm, d), lambda i, j: (i, 0)),
        out_shape=jax.ShapeDtypeStruct((n_rows, d), F32),
        scratch_shapes=[pltpu.VMEM((tm, d), BF16)],
        compiler_params=_cparams(("parallel", "arbitrary")),
        name=name,
    )(hs, g2.reshape(1, d), g3.reshape(1, d), mod, w_in, w_in, w_out)


GLA_CHUNK = 256


def _gla_structure(cs, reverse):
    idx = np.arange(cs)
    ip = cs - 1 - idx if reverse else idx
    ii, jj = ip[:, None], ip[None, :]
    masks, dist = [], []
    s = cs // 2
    while s >= 1:
        masks.append(((ii // (2 * s)) == (jj // (2 * s))) & ((ii & s) != 0) & ((jj & s) == 0))
        pp = (ii // (2 * s)) * (2 * s) + s - 1
        dist.append((jj > np.minimum(ii, pp)) & (jj <= np.maximum(ii, pp)))
        s //= 2
    masks.append(ii == jj)
    return (np.stack(masks).astype(np.float32), (jj <= ii).astype(np.float32),
            np.concatenate(dist, axis=0).astype(np.float32))


def _gla_chunk(q_b, k_b, v_b, lg, msk_ref, tri, dist_ref, st_ref, *, cs, reverse):
    hi = lg.astype(BF16)
    lo = (lg - hi.astype(F32)).astype(BF16)
    b = _dot(tri, hi) + _dot(tri, lo)
    yield
    last = 0 if reverse else cs - 1
    b_last = b[last:last + 1, :]

    st = st_ref[...]
    o = _dot_nt(q_b * jnp.exp2(b).astype(BF16), st.astype(BF16))
    yield

    attn = jnp.zeros((cs, cs), F32)
    nlev = dist_ref.shape[0] // cs
    for lvl in range(nlev):
        nd = _dot(dist_ref[lvl * cs:(lvl + 1) * cs, :], hi)
        fac = jnp.exp2(nd).astype(BF16)
        attn = attn + msk_ref[lvl] * _dot_nt(q_b * fac, k_b * fac)
        yield
    attn = attn + msk_ref[nlev] * _dot_nt(q_b, k_b)

    k_dec = k_b * jnp.exp2(b_last - b).astype(BF16)
    st_ref[...] = st * jnp.exp2(b_last) + _dot_tn(v_b, k_dec)
    yield
    return o + _dot(attn.astype(BF16), v_b)


def _interleave(gens):
    results = [None] * len(gens)
    active = list(range(len(gens)))
    while active:
        for idx in list(active):
            try:
                next(gens[idx])
            except StopIteration as stop:
                results[idx] = stop.value
                active.remove(idx)
    return results


def _gla_scan_kernel(qf_ref, kf_ref, vf_ref, gf_ref, qb_ref, kb_ref, vb_ref, gb_ref, wu_ref, bg_ref,
                     msk_ref, tri_ref, dist_ref, of_ref, ob_ref, st_scr, *, cs, dk, dv):
    @pl.when(pl.program_id(1) == 0)
    def _():
        st_scr[...] = jnp.zeros_like(st_scr)

    dirs = ((qf_ref, kf_ref, vf_ref, gf_ref, of_ref), (qb_ref, kb_ref, vb_ref, gb_ref, ob_ref))
    chains, dests = [], []
    for dr, (q_ref, k_ref, v_ref, gd_ref, o_ref) in enumerate(dirs):
        z = _dot(gd_ref[...].astype(BF16), wu_ref[dr]) + bg_ref[dr]
        lg_all = (jnp.minimum(z, 0.0) - jnp.log1p(jnp.exp(-jnp.abs(z)))) * (LOG2E / GLA_TAU)
        for h in range(GLA_HEADS):
            ks = slice(h * dk, (h + 1) * dk)
            vs = slice(h * dv, (h + 1) * dv)
            chains.append(_gla_chunk(q_ref[:, ks] * (dk ** -0.5), k_ref[:, ks], v_ref[:, vs].astype(BF16),
                                     lg_all[:, ks], msk_ref.at[dr], tri_ref[dr], dist_ref.at[dr], st_scr.at[dr, h],
                                     cs=cs, reverse=bool(dr)))
            dests.append((o_ref, vs))
    for (o_ref, vs), o in zip(dests, _interleave(chains)):
        o_ref[:, vs] = o.astype(o_ref.dtype)


def _gla_scan(p, gd, wu_ext, bg, b_sz, t_lat, t_ctx):
    cs = GLA_CHUNK
    qkw = wu_ext.shape[-1]
    dk = qkw // GLA_HEADS
    vw = 2 * qkw
    dv = vw // GLA_HEADS
    n_tok = p.shape[0]
    ncc, ncl = t_ctx // cs, t_lat // cs
    ctx_blk0 = (b_sz * t_lat) // cs
    structs = [_gla_structure(cs, rev) for rev in (False, True)]
    masks, tri, dist = (np.stack(parts) for parts in zip(*structs))

    def rowblk(rev):
        def f(b, s):
            if rev:
                cc, lc = ncc - 1 - s, ncl - 1 - (s - ncc)
            else:
                cc, lc = s, s - ncc
            return jnp.where(s < ncc, ctx_blk0 + b * ncc + cc, b * ncl + lc)
        return f

    def dir_specs(rev):
        rb = rowblk(rev)
        return [pl.BlockSpec((cs, qkw), lambda b, s: (rb(b, s), 0)),
                pl.BlockSpec((cs, qkw), lambda b, s: (rb(b, s), 1)),
                pl.BlockSpec((cs, vw), lambda b, s: (rb(b, s), 1)),
                pl.BlockSpec((cs, LANES), lambda b, s: (rb(b, s), 0))]

    return pl.pallas_call(
        functools.partial(_gla_scan_kernel, cs=cs, dk=dk, dv=dv),
        grid=(b_sz, ncc + ncl),
        in_specs=dir_specs(False) + dir_specs(True) + [
            pl.BlockSpec((2, LANES, qkw), lambda b, s: (0, 0, 0)),
            pl.BlockSpec((2, 1, qkw), lambda b, s: (0, 0, 0)),
            pl.BlockSpec(masks.shape, lambda b, s: (0, 0, 0, 0)),
            pl.BlockSpec(tri.shape, lambda b, s: (0, 0, 0)),
            pl.BlockSpec(dist.shape, lambda b, s: (0, 0, 0)),
        ],
        out_specs=[pl.BlockSpec((cs, vw), lambda b, s: (rowblk(False)(b, s), 0)),
                   pl.BlockSpec((cs, vw), lambda b, s: (rowblk(True)(b, s), 0))],
        out_shape=[jax.ShapeDtypeStruct((n_tok, vw), BF16)] * 2,
        scratch_shapes=[pltpu.VMEM((2, GLA_HEADS, dv, dk), F32)],
        compiler_params=_cparams(("parallel", "arbitrary")),
        name="gla_scan",
    )(p, p, p, gd, p, p, p, gd, wu_ext, bg, jnp.asarray(masks), jnp.asarray(tri, dtype=BF16),
      jnp.asarray(dist, dtype=BF16))


def _gla_out_proj_kernel(*refs, gate_idx, d, dv, n1, n_seg):
    of_ref, ob_ref, r_ref, gh_ref, w_ref = refs[:5]
    h_refs = refs[5:5 + n_seg]
    g_ref, mod_ref, o_ref, a_scr = refs[5 + n_seg:]
    gh = gh_ref[...]
    for r0 in range(0, a_scr.shape[0], NORM_ROWS):
        rows = pl.ds(r0, NORM_ROWS)
        for h in range(GLA_HEADS):
            sl = slice(h * dv, (h + 1) * dv)
            y = _rms(of_ref[rows, sl].astype(F32) + ob_ref[rows, sl].astype(F32)) * gh
            r = r_ref[rows, sl].astype(F32)
            a_scr[rows, sl] = (y * (r * _sigmoid(r))).astype(a_scr.dtype)
    o_ref[...] = _dot(a_scr[...], w_ref[...])
    _on_segment(n1, h_refs, lambda h_ref: _resid_norm_rows(o_ref, h_ref, o_ref, g_ref, mod_ref, gate_idx, d))


def _gla_out_proj(o_f, o_b, p, g_head, w, hs_segs, g, mod, gate_idx, n_rows, tiles_per_batch, nb):
    d = o_f.shape[1]
    dv = d // GLA_HEADS
    tm = ROW_TILE
    rcol = p.shape[1] // d - 1
    row = lambda col: pl.BlockSpec((tm, d), lambda i: (i, col))
    n1, seg_specs = _segment_specs(hs_segs, tm, d)
    return pl.pallas_call(
        functools.partial(_gla_out_proj_kernel, gate_idx=gate_idx, d=d, dv=dv, n1=n1, n_seg=len(hs_segs)),
        grid=(n_rows // tm,),
        in_specs=[
            row(0), row(0), row(rcol),
            pl.BlockSpec((1, dv), lambda i: (0, 0)),
            pl.BlockSpec((d, d), lambda i: (0, 0)),
        ] + seg_specs + [
            pl.BlockSpec((1, d), lambda i: (0, 0)),
            _mod_spec(mod, tiles_per_batch, nb),
        ],
        out_specs=row(0),
        out_shape=jax.ShapeDtypeStruct((n_rows, d), F32),
        scratch_shapes=[pltpu.VMEM((tm, d), BF16)],
        compiler_params=_cparams(("parallel",)),
        name="gla_out_proj",
    )(o_f, o_b, p, g_head.reshape(1, dv), w, *hs_segs, g.reshape(1, d), mod)


def _rope_tables(t_lat):
    t = jnp.arange(t_lat)
    row = (t // GRID_W).astype(F32)
    col = (t % GRID_W).astype(F32)
    n_freq = ROPE_DIM // 4
    inv_freq = ROPE_THETA ** (-jnp.arange(n_freq, dtype=F32) / n_freq)
    ang = jnp.concatenate([row[:, None] * inv_freq, col[:, None] * inv_freq], axis=-1)
    cos, sin = jnp.cos(ang), jnp.sin(ang)
    reps = LANES // ROPE_DIM
    return jnp.tile(jnp.concatenate([cos, cos], axis=-1), (1, reps)), jnp.tile(jnp.concatenate([-sin, sin], axis=-1), (1, reps))


MLA_TQ = 512
MLA_TK = 1024
MLA_HEADS_PER_STEP = 2


def _mla_attn_kernel(qn_ref, qr_ref, qnc_ref, qrc_ref, knc_ref, vc_ref, krc_ref, knl_ref, vl_ref, krl_ref,
                     o_ref, oc_ref, kcat, vext, *, tc, tl):
    t = pl.program_id(2)
    nh = MLA_HEADS_PER_STEP
    rd = ROPE_DIM
    vw = MLA_V + LANES

    def qcat(n_ref, r_ref, h):
        return jnp.concatenate([n_ref[:, h * MLA_NOPE:(h + 1) * MLA_NOPE], r_ref[:, h * rd:(h + 1) * rd]], axis=1)

    def finish(acc):
        return acc[:, 0:MLA_V] / acc[:, MLA_V:vw]

    @pl.when(t == 0)
    def _():
        for h in range(nh):
            ns = slice(h * MLA_NOPE, (h + 1) * MLA_NOPE)
            kcat[h, 0:tc, 0:MLA_NOPE] = knc_ref[:, ns]
            kcat[h, 0:tc, MLA_NOPE:MLA_NOPE + rd] = krc_ref[:, 0:rd]
            kcat[h, tc:tc + tl, 0:MLA_NOPE] = knl_ref[:, ns]
            kcat[h, tc:tc + tl, MLA_NOPE:MLA_NOPE + rd] = krl_ref[:, 0:rd]
            vext[0:tc, h * vw:h * vw + MLA_V] = vc_ref[:, h * MLA_V:(h + 1) * MLA_V]
            vext[tc:tc + tl, h * vw:h * vw + MLA_V] = vl_ref[:, h * MLA_V:(h + 1) * MLA_V]
            vext[:, h * vw + MLA_V:(h + 1) * vw] = jnp.ones((tc + tl, LANES), BF16)
        for h in range(nh):
            s = _dot_nt(qcat(qnc_ref, qrc_ref, h), kcat[h, 0:tc, :])
            p = jnp.exp2(s - jnp.max(s, axis=-1, keepdims=True))
            acc = _dot(p.astype(BF16), vext[0:tc, h * vw:(h + 1) * vw])
            oc_ref[:, h * MLA_V:(h + 1) * MLA_V] = finish(acc).astype(oc_ref.dtype)

    chunks = [(0, tc)] + [(c0, c0 + MLA_TK) for c0 in range(tc, tc + tl, MLA_TK)]
    units = [(h, c0, c1) for h in range(nh) for (c0, c1) in chunks]
    qcs = [qcat(qn_ref, qr_ref, h) for h in range(nh)]

    def scores(unit):
        h, c0, c1 = unit
        return _dot_nt(qcs[h], kcat[h, c0:c1, :])

    s_next = scores(units[0])
    m = acc = None
    for idx, (h, c0, c1) in enumerate(units):
        s = s_next
        if idx + 1 < len(units):
            s_next = scores(units[idx + 1])
        s_max = jnp.max(s, axis=-1, keepdims=True)
        m_new = s_max if c0 == 0 else jnp.maximum(m, s_max)
        pv = _dot(jnp.exp2(s - m_new).astype(BF16), vext[c0:c1, h * vw:(h + 1) * vw])
        acc = pv if c0 == 0 else jnp.exp2(m - m_new) * acc + pv
        m = m_new
        if c1 == tc + tl:
            o_ref[:, h * MLA_V:(h + 1) * MLA_V] = finish(acc).astype(o_ref.dtype)


def _mla_attention(qn, qr, kv, kr, b_sz, t_lat, t_ctx):
    hd = qn.shape[1]
    nh = MLA_HEADS_PER_STEP
    n_heads = hd // MLA_NOPE
    tq = MLA_TQ
    assert t_lat % tq == 0 and t_lat % MLA_TK == 0
    nlt = t_lat // tq
    ctx_blk0 = (b_sz * t_lat) // t_ctx
    vcol0 = n_heads // nh
    wn, wr = nh * MLA_NOPE, nh * ROPE_DIM
    return pl.pallas_call(
        functools.partial(_mla_attn_kernel, tc=t_ctx, tl=t_lat),
        grid=(b_sz, n_heads // nh, nlt),
        in_specs=[
            pl.BlockSpec((tq, wn), lambda b, h, t: (b * nlt + t, h)),
            pl.BlockSpec((tq, wr), lambda b, h, t: (b * nlt + t, h)),
            pl.BlockSpec((t_ctx, wn), lambda b, h, t: (ctx_blk0 + b, h)),
            pl.BlockSpec((t_ctx, wr), lambda b, h, t: (ctx_blk0 + b, h)),
            pl.BlockSpec((t_ctx, wn), lambda b, h, t: (ctx_blk0 + b, h)),
            pl.BlockSpec((t_ctx, wn), lambda b, h, t: (ctx_blk0 + b, vcol0 + h)),
            pl.BlockSpec((t_ctx, LANES), lambda b, h, t: (ctx_blk0 + b, 0)),
            pl.BlockSpec((t_lat, wn), lambda b, h, t: (b, h)),
            pl.BlockSpec((t_lat, wn), lambda b, h, t: (b, vcol0 + h)),
            pl.BlockSpec((t_lat, LANES), lambda b, h, t: (b, 0)),
        ],
        out_specs=[pl.BlockSpec((tq, nh * MLA_V), lambda b, h, t: (b * nlt + t, h)),
                   pl.BlockSpec((t_ctx, nh * MLA_V), lambda b, h, t: (b, h))],
        out_shape=[jax.ShapeDtypeStruct((b_sz * t_lat, n_heads * MLA_V), BF16),
                   jax.ShapeDtypeStruct((b_sz * t_ctx, n_heads * MLA_V), BF16)],
        scratch_shapes=[pltpu.VMEM((nh, t_ctx + t_lat, MLA_NOPE + ROPE_DIM), BF16),
                        pltpu.VMEM((t_ctx + t_lat, nh * (MLA_V + LANES)), BF16)],
        compiler_params=_cparams(("parallel", "parallel", "arbitrary")),
        name="mla_attention",
    )(qn, qr, qn, qr, kv, kv, kr, kv, kv, kr)


def _swa_attn_kernel(sink_ref, q_ref, kc_ref, vc_ref, kp_ref, kc0_ref, kn_ref, vp_ref, vc0_ref, vn_ref, o_ref,
                     *, nct, nb, n_kv):
    t = pl.program_id(1)
    n = t - nct
    blk = WINDOW
    hd = SWA_HEAD_DIM
    tc = kc_ref.shape[0]
    nk = tc + 3 * blk
    npair = SWA_GROUP // 2
    upair = 1

    qi = lax.broadcasted_iota(jnp.int32, (blk, 3 * blk), 0)
    kj = lax.broadcasted_iota(jnp.int32, (blk, 3 * blk), 1)
    lo = jnp.where(n > 0, 0, blk)
    hi = jnp.where(t < nct, 0, jnp.where(n < nb - 1, 3 * blk, 2 * blk))
    win_ok = (jnp.abs(qi - kj + blk) <= WINDOW) & (kj >= lo) & (kj < hi)
    bias = jnp.concatenate([jnp.zeros((blk, tc), F32), jnp.where(win_ok, 0.0, NEG_BIG)], axis=1)
    bias = jnp.concatenate([bias] * upair, axis=0)
    lane = lax.broadcasted_iota(jnp.int32, (blk, 2 * hd), 1)
    zeros = jnp.zeros((nk, hd), BF16)
    ones = jnp.ones((nk, hd), BF16)
    tail = tc + 2 * blk
    shared = {}

    def kv_operands(kvh):
        if kvh not in shared:
            ks = slice(kvh * hd, (kvh + 1) * hd)
            k_all = jnp.concatenate([kc_ref[:, ks], kp_ref[:, ks], kc0_ref[:, ks], kn_ref[:, ks]], axis=0)
            v_all = jnp.concatenate([vc_ref[:, ks], vp_ref[:, ks], vc0_ref[:, ks], vn_ref[:, ks]], axis=0)
            k_bd = jnp.concatenate([jnp.concatenate([k_all, zeros], axis=1),
                                    jnp.concatenate([zeros, k_all], axis=1)], axis=0)

            def v_ext(r0, r1, e):
                v, z, o = v_all[r0:r1], zeros[r0:r1], ones[r0:r1]
                return jnp.concatenate([z, v, z, o] if e else [v, z, o, z], axis=1)

            v_tail = jnp.concatenate([v_ext(tail, nk, 0), v_ext(tail, nk, 1)], axis=0)
            v_main = [(v_ext(0, tc, e), v_ext(tc, tail, e)) for e in range(2)]
            shared[kvh] = (k_bd, v_tail, v_main)
        return shared[kvh]

    def unit_chain(kvh, u):
        k_bd, v_tail, v_main = kv_operands(kvh)
        h0 = kvh * SWA_GROUP + 2 * upair * u
        q_st = jnp.concatenate([q_ref[:, (h0 + 2 * j) * hd:(h0 + 2 * j + 2) * hd] for j in range(upair)], axis=0)
        s = _dot_nt(q_st, k_bd)
        yield
        ps, ms = [], []
        for e in range(2):
            sink_col = jnp.concatenate([jnp.full((blk, 1), sink_ref[h0 + 2 * j + e] * LOG2E, F32)
                                        for j in range(upair)], axis=0)
            s_e = s[:, e * nk:(e + 1) * nk] + bias
            m = jnp.maximum(jnp.max(s_e, axis=-1, keepdims=True), sink_col)
            p = jnp.exp2(s_e - m).astype(BF16)
            ps += [p[:, 0:tc], p[:, tc:]]
            ms.append(sink_col - m)
        acc = _dot(jnp.concatenate([ps[1][:, 2 * blk:], ps[3][:, 2 * blk:]], axis=1), v_tail)
        for e in range(2):
            acc = acc + _dot(ps[2 * e], v_main[e][0]) + _dot(ps[2 * e + 1][:, 0:2 * blk], v_main[e][1])
        for j in range(upair):
            r = slice(j * blk, (j + 1) * blk)
            sink_term = jnp.exp2(jnp.where(lane < hd, ms[0][r], ms[1][r]))
            o = acc[r, 0:2 * hd] / (acc[r, 2 * hd:4 * hd] + sink_term)
            o_ref[:, (h0 + 2 * j) * hd:(h0 + 2 * j + 2) * hd] = o.astype(o_ref.dtype)

    chains = [unit_chain(kvh, u) for kvh in range(n_kv) for u in range(npair // upair)]
    next(chains[0])
    for n_unit, chain in enumerate(chains):
        if n_unit + 1 < len(chains):
            next(chains[n_unit + 1])
        for _ in chain:
            pass


def _swa_attention(qkv, sinks, b_sz, t_lat, t_ctx):
    n_tok = qkv.shape[0]
    blk = WINDOW
    hd = SWA_HEAD_DIM
    n_q = sinks.shape[0]
    n_kv = n_q // SWA_GROUP
    kvw = n_kv * hd
    kcol = (n_q * hd) // kvw
    vcol = kcol + 1
    nct, nb = t_ctx // blk, t_lat // blk
    ctx_blk0 = (b_sz * t_lat) // blk
    ctx_row0 = (b_sz * t_lat) // t_ctx

    def qrow(b, t):
        return jnp.where(t < nct, ctx_blk0 + b * nct + t, b * nb + t - nct)

    def krow(b, t, off):
        return b * nb + jnp.clip(t - nct + off, 0, nb - 1)

    kspec = lambda off: pl.BlockSpec((blk, kvw), lambda b, t: (krow(b, t, off), kcol))
    vspec = lambda off: pl.BlockSpec((blk, kvw), lambda b, t: (krow(b, t, off), vcol))
    return pl.pallas_call(
        functools.partial(_swa_attn_kernel, nct=nct, nb=nb, n_kv=n_kv),
        grid=(b_sz, nct + nb),
        in_specs=[
            pl.BlockSpec(memory_space=pltpu.SMEM),
            pl.BlockSpec((blk, n_q * hd), lambda b, t: (qrow(b, t), 0)),
            pl.BlockSpec((t_ctx, kvw), lambda b, t: (ctx_row0 + b, kcol)),
            pl.BlockSpec((t_ctx, kvw), lambda b, t: (ctx_row0 + b, vcol)),
            kspec(-1), kspec(0), kspec(1), vspec(-1), vspec(0), vspec(1),
        ],
        out_specs=pl.BlockSpec((blk, n_q * hd), lambda b, t: (qrow(b, t), 0)),
        out_shape=jax.ShapeDtypeStruct((n_tok, n_q * hd), BF16),
        compiler_params=_cparams(("parallel", "arbitrary")),
        name="swa_attention",
    )(sinks, qkv, qkv, qkv, qkv, qkv, qkv, qkv, qkv, qkv)


def kernel(x, c, ctx, c_ctx, w_ada, b_ada, g_norm, w_ffn_in, w_ffn_out, gla_w_in, gla_w_gate_down, gla_w_gate_up, gla_b_gate, gla_g_head, gla_w_out, mla_w_in, mla_g_q, mla_w_uq, mla_g_kv, mla_w_ukv, mla_w_out, swa_w_in, swa_sinks, swa_w_out):
    b_sz, t_lat, d = x.shape
    t_ctx = ctx.shape[1]
    depth = w_ada.shape[0]
    n_lat, n_ctx = b_sz * t_lat, b_sz * t_ctx
    n_tok = n_lat + n_ctx
    tm = ROW_TILE
    assert t_lat % tm == 0 and n_ctx % tm == 0 and b_sz < MOD_ROWS
    tpb = t_lat // tm
    n_lat_tiles = n_lat // tm

    hs = None
    cond = jnp.zeros((MOD_ROWS, d), F32).at[:b_sz].set(c).at[b_sz].set(c_ctx)
    mod_all = _modulation(cond, w_ada, b_ada)
    cosf, sinf = _rope_tables(t_lat)
    common = dict(tiles_per_batch=tpb, nb=b_sz)
    tm_big = 2 * tm if t_lat % (2 * tm) == 0 and n_ctx % (2 * tm) == 0 else tm
    w_ffn_in_b, w_ffn_out_b = w_ffn_in.astype(BF16), w_ffn_out.astype(BF16)

    for i in range(depth):
        kind, j = i % 3, i // 3
        last = i == depth - 1
        n_out = n_lat if last else n_tok
        mod = mod_all[i].reshape(MOD_ROWS, 1, 6 * d)

        if kind == 0:
            rank = GLA_GATE_RANK
            qk_w = gla_w_gate_up.shape[-1]
            pad = LANES - 2 * rank
            w_gd = jnp.concatenate([gla_w_gate_down[j, 0], gla_w_gate_down[j, 1], jnp.zeros((d, pad), F32)],
                                   axis=1).astype(BF16)
            hs_segs = (x.reshape(n_lat, d), ctx.reshape(n_ctx, d)) if i == 0 else (hs,)
            p, gd = _gla_in_proj(hs_segs, n_tok, g_norm[i, 0], mod, gla_w_in[j].astype(BF16), w_gd, tm_big,
                                 t_lat // tm_big, b_sz)
            wu_ext = jnp.zeros((2, LANES, qk_w), F32)
            for dr in range(2):
                wu_ext = wu_ext.at[dr, dr * rank:(dr + 1) * rank].set(gla_w_gate_up[j, dr])
            o_f, o_b = _gla_scan(p, gd, wu_ext.astype(BF16), gla_b_gate[j].reshape(2, 1, qk_w), b_sz, t_lat, t_ctx)
            hs = _gla_out_proj(o_f, o_b, p, gla_g_head[j], gla_w_out[j].astype(BF16), hs_segs, g_norm[i, 1], mod, 2,
                               n_out, **common)
        elif kind == 1:
            n_heads = mla_w_out.shape[1] // MLA_V
            w_in = jnp.concatenate([mla_w_in[j], jnp.zeros((d, LANES - ROPE_DIM), F32)], axis=1).astype(BF16)
            w_uq = mla_w_uq[j].reshape(MLA_Q_RANK, n_heads, MLA_NOPE + ROPE_DIM)
            w_uq = jnp.concatenate([w_uq[:, :, :MLA_NOPE].reshape(MLA_Q_RANK, -1),
                                    w_uq[:, :, MLA_NOPE:].reshape(MLA_Q_RANK, -1)], axis=1).astype(BF16)
            w_ukv = mla_w_ukv[j].reshape(MLA_KV_RANK, n_heads, MLA_NOPE + MLA_V)
            w_ukv = jnp.concatenate([w_ukv[:, :, :MLA_NOPE].reshape(MLA_KV_RANK, -1),
                                     w_ukv[:, :, MLA_NOPE:].reshape(MLA_KV_RANK, -1)], axis=1).astype(BF16)
            q_scale = (MLA_NOPE + ROPE_DIM) ** -0.5 * LOG2E
            qn, qr, kv, kr = _mla_in_proj(hs, g_norm[i, 0], mod, w_in, mla_g_q[j], w_uq, mla_g_kv[j], w_ukv,
                                          n_heads * MLA_NOPE, q_scale, cosf, sinf, n_lat_tiles, **common)
            a_segs = _mla_attention(qn, qr, kv, kr, b_sz, t_lat, t_ctx)
            w_o = mla_w_out[j]
        else:
            n_q = swa_sinks.shape[1]
            qkw = (n_q + n_q // SWA_GROUP) * SWA_HEAD_DIM
            qkv = _swa_in_proj(hs, g_norm[i, 0], mod, swa_w_in[j].astype(BF16), cosf, sinf, qkw // LANES,
                               (n_q * SWA_HEAD_DIM) // LANES, SWA_HEAD_DIM ** -0.5 * LOG2E, n_lat_tiles, **common)
            a_segs = (_swa_attention(qkv, swa_sinks[j], b_sz, t_lat, t_ctx),)
            w_o = swa_w_out[j]

        if kind != 0:
            hs = _proj_resid(tuple(a_segs), w_o.astype(BF16), hs, g_norm[i, 1], mod, 2, n_out, name="mixer_out_proj",
                             **common)
        hs = _ffn(hs, g_norm[i, 2], g_norm[i, 3], mod, w_ffn_in_b, w_ffn_out_b, i, n_out, tm_big, t_lat // tm_big, b_sz,
                  name="ffn")

    return hs.reshape(b_sz, t_lat, d)
```

```python
import functools

import numpy as np
import jax
import jax.numpy as jnp
from jax import lax
from jax.experimental import pallas as pl
from jax.experimental.pallas import tpu as pltpu

F32 = jnp.float32
BF16 = jnp.bfloat16

EPS = 1e-6
ROPE_THETA = 10000.0
ROPE_DIM = 64
GRID_W = 64
WINDOW = 128
GLA_HEADS = 4
GLA_GATE_RANK = 16
GLA_TAU = 16.0
MLA_Q_RANK = 512
MLA_KV_RANK = 512
MLA_NOPE = 128
MLA_V = 128
SWA_HEAD_DIM = 64
SWA_GROUP = 8

LANES = 128
VMEM_LIMIT = 56 * 1024 * 1024
ROW_TILE = 512
COL_TILE = 1024
FFN_TILE = 512
NORM_ROWS = 16
MOD_ROWS = 16
NEG_BIG = -1e30
LOG2E = 1.4426950408889634


def _cparams(sem):
    return pltpu.CompilerParams(dimension_semantics=sem, vmem_limit_bytes=VMEM_LIMIT)


def _dot(a, b):
    return jnp.dot(a, b, preferred_element_type=F32)


def _dot_nt(a, b):
    return lax.dot_general(a, b, (((1,), (1,)), ((), ())), preferred_element_type=F32)


def _dot_tn(a, b):
    return lax.dot_general(a, b, (((0,), (0,)), ((), ())), preferred_element_type=F32)


def _sigmoid(x):
    return 1.0 / (1.0 + jnp.exp(-x))


def _rms(x):
    return x * lax.rsqrt(jnp.mean(x * x, axis=-1, keepdims=True) + EPS)


def _mod_kernel(c_ref, w_ref, b_ref, o_ref):
    c = c_ref[...]
    s = (c * _sigmoid(c)).astype(BF16)
    o_ref[0] = _dot(s, w_ref[0].astype(BF16)) + b_ref[0]


def _modulation(cond, w_ada, b_ada):
    depth, d, n = w_ada.shape
    tn = COL_TILE
    return pl.pallas_call(
        _mod_kernel,
        grid=(depth, n // tn),
        in_specs=[
            pl.BlockSpec((MOD_ROWS, d), lambda l, j: (0, 0)),
            pl.BlockSpec((1, d, tn), lambda l, j: (l, 0, j)),
            pl.BlockSpec((1, 1, tn), lambda l, j: (l, 0, j)),
        ],
        out_specs=pl.BlockSpec((1, MOD_ROWS, tn), lambda l, j: (l, 0, j)),
        out_shape=jax.ShapeDtypeStruct((depth, MOD_ROWS, n), F32),
        compiler_params=_cparams(("parallel", "parallel")),
        name="modulation",
    )(cond, w_ada, b_ada.reshape(depth, 1, n))


def _adaln_rows(dst_ref, x_ref, g_ref, mod_ref, shift_idx, scale_idx, d):
    m = mod_ref[0]
    gs = g_ref[...] * (1.0 + m[:, scale_idx * d:(scale_idx + 1) * d])
    sh = m[:, shift_idx * d:(shift_idx + 1) * d]
    for r in range(0, x_ref.shape[0], NORM_ROWS):
        rows = pl.ds(r, NORM_ROWS)
        dst_ref[rows, :] = (_rms(x_ref[rows, :]) * gs + sh).astype(dst_ref.dtype)


def _resid_norm_rows(o_ref, h_ref, y_ref, g_ref, mod_ref, gate_idx, d):
    gg = mod_ref[0][:, gate_idx * d:(gate_idx + 1) * d] * g_ref[...]
    for r in range(0, h_ref.shape[0], NORM_ROWS):
        rows = pl.ds(r, NORM_ROWS)
        o_ref[rows, :] = h_ref[rows, :] + _rms(y_ref[rows, :]) * gg


def _rope_chunk(y, cosf, sinf, first, is_lat):
    partner = jnp.where(first, pltpu.roll(y, LANES - ROPE_DIM // 2, 1), pltpu.roll(y, ROPE_DIM // 2, 1))
    return jnp.where(is_lat, y * cosf + partner * sinf, y)


def _first_half_lanes(shape):
    lane = lax.broadcasted_iota(jnp.int32, shape, 1)
    return (lane % ROPE_DIM) < (ROPE_DIM // 2)


def _mod_spec(mod, tiles_per_batch, nb):
    return pl.BlockSpec((1, 1, mod.shape[-1]), lambda i, *_: (jnp.minimum(i // tiles_per_batch, nb), 0, 0))


def _segment_specs(segs, tm, width):
    n1 = segs[0].shape[0] // tm
    if len(segs) == 1:
        return n1, [pl.BlockSpec((tm, width), lambda i, *_: (i, 0))]
    return n1, [pl.BlockSpec((tm, width), lambda i, *_: (jnp.minimum(i, n1 - 1), 0)),
                pl.BlockSpec((tm, width), lambda i, *_: (jnp.maximum(i - n1, 0), 0))]


def _on_segment(n1, seg_refs, fn, pred=None):
    i = pl.program_id(0)
    if len(seg_refs) == 1:
        fn(seg_refs[0]) if pred is None else pl.when(pred)(lambda: fn(seg_refs[0]))
        return
    in_first = i < n1
    pl.when(in_first if pred is None else in_first & pred)(lambda: fn(seg_refs[0]))
    pl.when(~in_first if pred is None else ~in_first & pred)(lambda: fn(seg_refs[1]))


def _gla_in_proj_kernel(*refs, d, n1, n_seg):
    x_refs = refs[:n_seg]
    g_ref, mod_ref, w_ref, wgd_ref, o_ref, gd_ref, a_scr = refs[n_seg:]
    first = pl.program_id(1) == 0
    _on_segment(n1, x_refs, lambda x_ref: _adaln_rows(a_scr, x_ref, g_ref, mod_ref, 0, 1, d), first)

    @pl.when(first)
    def _():
        gd_ref[...] = _dot(a_scr[...], wgd_ref[...])

    o_ref[...] = _dot(a_scr[...], w_ref[...]).astype(o_ref.dtype)


def _gla_in_proj(hs_segs, n_rows, g, mod, w, w_gd, tm, tiles_per_batch, nb):
    d = hs_segs[0].shape[1]
    n = w.shape[1]
    tn = COL_TILE
    n_side = w_gd.shape[1]
    n1, seg_specs = _segment_specs(hs_segs, tm, d)
    return pl.pallas_call(
        functools.partial(_gla_in_proj_kernel, d=d, n1=n1, n_seg=len(hs_segs)),
        grid=(n_rows // tm, n // tn),
        in_specs=seg_specs + [
            pl.BlockSpec((1, d), lambda i, j: (0, 0)),
            _mod_spec(mod, tiles_per_batch, nb),
            pl.BlockSpec((d, tn), lambda i, j: (0, j)),
            pl.BlockSpec((d, n_side), lambda i, j: (0, 0)),
        ],
        out_specs=[pl.BlockSpec((tm, tn), lambda i, j: (i, j)),
                   pl.BlockSpec((tm, n_side), lambda i, j: (i, 0))],
        out_shape=[jax.ShapeDtypeStruct((n_rows, n), BF16),
                   jax.ShapeDtypeStruct((n_rows, n_side), F32)],
        scratch_shapes=[pltpu.VMEM((tm, d), BF16)],
        compiler_params=_cparams(("parallel", "arbitrary")),
        name="gla_in_proj",
    )(*hs_segs, g.reshape(1, d), mod, w, w_gd)


def _swa_in_proj_kernel(x_ref, g_ref, mod_ref, cos_ref, sin_ref, w_ref, o_ref, a_scr, *, d, n_lat_tiles, n_roped,
                        n_scaled, out_scale):
    _adaln_rows(a_scr, x_ref, g_ref, mod_ref, 0, 1, d)
    y_all = _dot(a_scr[...], w_ref[...])
    is_lat = pl.program_id(0) < n_lat_tiles
    cosf, sinf = cos_ref[...], sin_ref[...]
    first = _first_half_lanes(cosf.shape)
    for c in range(y_all.shape[1] // LANES):
        y = y_all[:, c * LANES:(c + 1) * LANES]
        if c < n_roped:
            y = _rope_chunk(y, cosf, sinf, first, is_lat)
        if c < n_scaled:
            y = y * out_scale
        o_ref[:, c * LANES:(c + 1) * LANES] = y.astype(o_ref.dtype)


def _swa_in_proj(hs, g, mod, w, cosf, sinf, n_roped, n_scaled, out_scale, n_lat_tiles, tiles_per_batch, nb):
    n_rows, d = hs.shape
    n = w.shape[1]
    tm = ROW_TILE
    return pl.pallas_call(
        functools.partial(_swa_in_proj_kernel, d=d, n_lat_tiles=n_lat_tiles, n_roped=n_roped, n_scaled=n_scaled,
                          out_scale=out_scale),
        grid=(n_rows // tm,),
        in_specs=[
            pl.BlockSpec((tm, d), lambda i: (i, 0)),
            pl.BlockSpec((1, d), lambda i: (0, 0)),
            _mod_spec(mod, tiles_per_batch, nb),
            pl.BlockSpec((tm, LANES), lambda i: (i % tiles_per_batch, 0)),
            pl.BlockSpec((tm, LANES), lambda i: (i % tiles_per_batch, 0)),
            pl.BlockSpec((d, n), lambda i: (0, 0)),
        ],
        out_specs=pl.BlockSpec((tm, n), lambda i: (i, 0)),
        out_shape=jax.ShapeDtypeStruct((n_rows, n), BF16),
        scratch_shapes=[pltpu.VMEM((tm, d), BF16)],
        compiler_params=_cparams(("parallel",)),
        name="swa_in_proj",
    )(hs, g.reshape(1, d), mod, cosf, sinf, w)


def _mla_in_proj_kernel(x_ref, g_ref, mod_ref, cos_ref, sin_ref, win_ref, gq_ref, gkv_ref, wuq_ref, wukv_ref,
                        qn_ref, qr_ref, kv_ref, kr_ref, a_scr, *, d, n_lat_tiles, q_scale):
    _adaln_rows(a_scr, x_ref, g_ref, mod_ref, 0, 1, d)
    p1 = _dot(a_scr[...], win_ref[...])
    cq = (_rms(p1[:, 0:MLA_Q_RANK]) * gq_ref[...]).astype(BF16)
    ckv = (_rms(p1[:, MLA_Q_RANK:MLA_Q_RANK + MLA_KV_RANK]) * gkv_ref[...]).astype(BF16)
    q = _dot(cq, wuq_ref[...]) * q_scale
    n_nope = qn_ref.shape[1]
    qn_ref[...] = q[:, 0:n_nope].astype(qn_ref.dtype)
    kv_ref[...] = _dot(ckv, wukv_ref[...]).astype(kv_ref.dtype)
    is_lat = pl.program_id(0) < n_lat_tiles
    cosf, sinf = cos_ref[...], sin_ref[...]
    first = _first_half_lanes(cosf.shape)
    for c in range(qr_ref.shape[1] // LANES):
        y = q[:, n_nope + c * LANES:n_nope + (c + 1) * LANES]
        qr_ref[:, c * LANES:(c + 1) * LANES] = _rope_chunk(y, cosf, sinf, first, is_lat).astype(qr_ref.dtype)
    k_rope = p1[:, MLA_Q_RANK + MLA_KV_RANK:MLA_Q_RANK + MLA_KV_RANK + LANES]
    kr_ref[...] = _rope_chunk(k_rope, cosf, sinf, first, is_lat).astype(kr_ref.dtype)


def _mla_in_proj(hs, g, mod, w_in, g_q, w_uq, g_kv, w_ukv, n_nope, q_scale, cosf, sinf, n_lat_tiles, tiles_per_batch,
                 nb):
    n_rows, d = hs.shape
    tm = ROW_TILE
    n_q, n_kv = w_uq.shape[1], w_ukv.shape[1]
    const = lambda shape: pl.BlockSpec(shape, lambda i: (0, 0))
    row = lambda w: pl.BlockSpec((tm, w), lambda i: (i, 0))
    return pl.pallas_call(
        functools.partial(_mla_in_proj_kernel, d=d, n_lat_tiles=n_lat_tiles, q_scale=q_scale),
        grid=(n_rows // tm,),
        in_specs=[
            row(d), const((1, d)), _mod_spec(mod, tiles_per_batch, nb),
            pl.BlockSpec((tm, LANES), lambda i: (i % tiles_per_batch, 0)),
            pl.BlockSpec((tm, LANES), lambda i: (i % tiles_per_batch, 0)),
            const(w_in.shape), const((1, MLA_Q_RANK)), const((1, MLA_KV_RANK)), const(w_uq.shape), const(w_ukv.shape),
        ],
        out_specs=[row(n_nope), row(n_q - n_nope), row(n_kv), row(LANES)],
        out_shape=[jax.ShapeDtypeStruct((n_rows, n_nope), BF16), jax.ShapeDtypeStruct((n_rows, n_q - n_nope), BF16),
                   jax.ShapeDtypeStruct((n_rows, n_kv), BF16), jax.ShapeDtypeStruct((n_rows, LANES), BF16)],
        scratch_shapes=[pltpu.VMEM((tm, d), BF16)],
        compiler_params=_cparams(("parallel",)),
        name="mla_in_proj",
    )(hs, g.reshape(1, d), mod, cosf, sinf, w_in, g_q.reshape(1, -1), g_kv.reshape(1, -1), w_uq, w_ukv)


def _proj_resid_kernel(*refs, gate_idx, d, n1, n_seg):
    a_refs = refs[:n_seg]
    w_ref, h_ref, g_ref, mod_ref, o_ref = refs[n_seg:]

    def body(src_ref):
        o_ref[...] = _dot(src_ref[...].astype(BF16), w_ref[...])
        _resid_norm_rows(o_ref, h_ref, o_ref, g_ref, mod_ref, gate_idx, d)

    _on_segment(n1, a_refs, body)


def _proj_resid(a_segs, w, hs, g, mod, gate_idx, n_rows, tiles_per_batch, nb, name):
    k, d = w.shape
    tm = ROW_TILE
    n1, seg_specs = _segment_specs(a_segs, tm, k)
    return pl.pallas_call(
        functools.partial(_proj_resid_kernel, gate_idx=gate_idx, d=d, n1=n1, n_seg=len(a_segs)),
        grid=(n_rows // tm,),
        in_specs=seg_specs + [
            pl.BlockSpec((k, d), lambda i: (0, 0)),
            pl.BlockSpec((tm, d), lambda i: (i, 0)),
            pl.BlockSpec((1, d), lambda i: (0, 0)),
            _mod_spec(mod, tiles_per_batch, nb),
        ],
        out_specs=pl.BlockSpec((tm, d), lambda i: (i, 0)),
        out_shape=jax.ShapeDtypeStruct((n_rows, d), F32),
        compiler_params=_cparams(("parallel",)),
        name=name,
    )(*a_segs, w, hs, g.reshape(1, d), mod)


def _ffn_kernel(h_ref, g2_ref, g3_ref, mod_ref, wg_ref, wu_ref, wo_ref, o_ref, a_scr, *, d):
    j = pl.program_id(1)

    @pl.when(j == 0)
    def _():
        _adaln_rows(a_scr, h_ref, g2_ref, mod_ref, 3, 4, d)
        o_ref[...] = jnp.zeros_like(o_ref)

    a = a_scr[...]
    gt = _dot(a, wg_ref[...])
    up = _dot(a, wu_ref[...])
    act = (gt * _sigmoid(gt) * up).astype(BF16)
    o_ref[...] += _dot(act, wo_ref[...])

    @pl.when(j == pl.num_programs(1) - 1)
    def _():
        _resid_norm_rows(o_ref, h_ref, o_ref, g3_ref, mod_ref, 5, d)


def _ffn(hs, g2, g3, mod, w_in, w_out, layer, n_rows, tm, tiles_per_batch, nb, name):
    d = hs.shape[1]
    f = w_out.shape[1]
    tf = FFN_TILE
    nf = f // tf
    return pl.pallas_call(
        functools.partial(_ffn_kernel, d=d),
        grid=(n_rows // tm, nf),
        in_specs=[
            pl.BlockSpec((tm, d), lambda i, j: (i, 0)),
            pl.BlockSpec((1, d), lambda i, j: (0, 0)),
            pl.BlockSpec((1, d), lambda i, j: (0, 0)),
            _mod_spec(mod, tiles_per_batch, nb),
            pl.BlockSpec((None, d, tf), lambda i, j: (layer, 0, j)),
            pl.BlockSpec((None, d, tf), lambda i, j: (layer, 0, nf + j)),
            pl.BlockSpec((None, tf, d), lambda i, j: (layer, j, 0)),
        ],
        out_specs=pl.BlockSpec((tm, d), lambda i, j: (i, 0)),
        out_shape=jax.ShapeDtypeStruct((n_rows, d), F32),
        scratch_shapes=[pltpu.VMEM((tm, d), BF16)],
        compiler_params=_cparams(("parallel", "arbitrary")),
        name=name,
    )(hs, g2.reshape(1, d), g3.reshape(1, d), mod, w_in, w_in, w_out)


GLA_CHUNK = 256


def _gla_structure(cs, reverse):
    idx = np.arange(cs)
    ip = cs - 1 - idx if reverse else idx
    ii, jj = ip[:, None], ip[None, :]
    masks, dist = [], []
    s = cs // 2
    while s >= 1:
        masks.append(((ii // (2 * s)) == (jj // (2 * s))) & ((ii & s) != 0) & ((jj & s) == 0))
        pp = (ii // (2 * s)) * (2 * s) + s - 1
        dist.append((jj > np.minimum(ii, pp)) & (jj <= np.maximum(ii, pp)))
        s //= 2
    masks.append(ii == jj)
    return (np.stack(masks).astype(np.float32), (jj <= ii).astype(np.float32),
            np.concatenate(dist, axis=0).astype(np.float32))


def _gla_chunk(q_b, k_b, v_b, lg, msk_ref, tri, dist_ref, st_ref, *, cs, reverse):
    hi = lg.astype(BF16)
    lo = (lg - hi.astype(F32)).astype(BF16)
    b = _dot(tri, hi) + _dot(tri, lo)
    yield
    last = 0 if reverse else cs - 1
    b_last = b[last:last + 1, :]

    st = st_ref[...]
    o = _dot_nt(q_b * jnp.exp2(b).astype(BF16), st.astype(BF16))
    yield

    attn = jnp.zeros((cs, cs), F32)
    nlev = dist_ref.shape[0] // cs
    for lvl in range(nlev):
        nd = _dot(dist_ref[lvl * cs:(lvl + 1) * cs, :], hi)
        fac = jnp.exp2(nd).astype(BF16)
        attn = attn + msk_ref[lvl] * _dot_nt(q_b * fac, k_b * fac)
        yield
    attn = attn + msk_ref[nlev] * _dot_nt(q_b, k_b)

    k_dec = k_b * jnp.exp2(b_last - b).astype(BF16)
    st_ref[...] = st * jnp.exp2(b_last) + _dot_tn(v_b, k_dec)
    yield
    return o + _dot(attn.astype(BF16), v_b)


def _interleave(gens):
    results = [None] * len(gens)
    active = list(range(len(gens)))
    while active:
        for idx in list(active):
            try:
                next(gens[idx])
            except StopIteration as stop:
                results[idx] = stop.value
                active.remove(idx)
    return results


def _gla_scan_kernel(qf_ref, kf_ref, vf_ref, gf_ref, qb_ref, kb_ref, vb_ref, gb_ref, wu_ref, bg_ref,
                     msk_ref, tri_ref, dist_ref, of_ref, ob_ref, st_scr, *, cs, dk, dv):
    @pl.when(pl.program_id(1) == 0)
    def _():
        st_scr[...] = jnp.zeros_like(st_scr)

    dirs = ((qf_ref, kf_ref, vf_ref, gf_ref, of_ref), (qb_ref, kb_ref, vb_ref, gb_ref, ob_ref))
    chains, dests = [], []
    for dr, (q_ref, k_ref, v_ref, gd_ref, o_ref) in enumerate(dirs):
        z = _dot(gd_ref[...].astype(BF16), wu_ref[dr]) + bg_ref[dr]
        lg_all = (jnp.minimum(z, 0.0) - jnp.log1p(jnp.exp(-jnp.abs(z)))) * (LOG2E / GLA_TAU)
        for h in range(GLA_HEADS):
            ks = slice(h * dk, (h + 1) * dk)
            vs = slice(h * dv, (h + 1) * dv)
            chains.append(_gla_chunk(q_ref[:, ks] * (dk ** -0.5), k_ref[:, ks], v_ref[:, vs].astype(BF16),
                                     lg_all[:, ks], msk_ref.at[dr], tri_ref[dr], dist_ref.at[dr], st_scr.at[dr, h],
                                     cs=cs, reverse=bool(dr)))
            dests.append((o_ref, vs))
    for (o_ref, vs), o in zip(dests, _interleave(chains)):
        o_ref[:, vs] = o.astype(o_ref.dtype)


def _gla_scan(p, gd, wu_ext, bg, b_sz, t_lat, t_ctx):
    cs = GLA_CHUNK
    qkw = wu_ext.shape[-1]
    dk = qkw // GLA_HEADS
    vw = 2 * qkw
    dv = vw // GLA_HEADS
    n_tok = p.shape[0]
    ncc, ncl = t_ctx // cs, t_lat // cs
    ctx_blk0 = (b_sz * t_lat) // cs
    structs = [_gla_structure(cs, rev) for rev in (False, True)]
    masks, tri, dist = (np.stack(parts) for parts in zip(*structs))

    def rowblk(rev):
        def f(b, s):
            if rev:
                cc, lc = ncc - 1 - s, ncl - 1 - (s - ncc)
            else:
                cc, lc = s, s - ncc
            return jnp.where(s < ncc, ctx_blk0 + b * ncc + cc, b * ncl + lc)
        return f

    def dir_specs(rev):
        rb = rowblk(rev)
        return [pl.BlockSpec((cs, qkw), lambda b, s: (rb(b, s), 0)),
                pl.BlockSpec((cs, qkw), lambda b, s: (rb(b, s), 1)),
                pl.BlockSpec((cs, vw), lambda b, s: (rb(b, s), 1)),
                pl.BlockSpec((cs, LANES), lambda b, s: (rb(b, s), 0))]

    return pl.pallas_call(
        functools.partial(_gla_scan_kernel, cs=cs, dk=dk, dv=dv),
        grid=(b_sz, ncc + ncl),
        in_specs=dir_specs(False) + dir_specs(True) + [
            pl.BlockSpec((2, LANES, qkw), lambda b, s: (0, 0, 0)),
            pl.BlockSpec((2, 1, qkw), lambda b, s: (0, 0, 0)),
            pl.BlockSpec(masks.shape, lambda b, s: (0, 0, 0, 0)),
            pl.BlockSpec(tri.shape, lambda b, s: (0, 0, 0)),
            pl.BlockSpec(dist.shape, lambda b, s: (0, 0, 0)),
        ],
        out_specs=[pl.BlockSpec((cs, vw), lambda b, s: (rowblk(False)(b, s), 0)),
                   pl.BlockSpec((cs, vw), lambda b, s: (rowblk(True)(b, s), 0))],
        out_shape=[jax.ShapeDtypeStruct((n_tok, vw), BF16)] * 2,
        scratch_shapes=[pltpu.VMEM((2, GLA_HEADS, dv, dk), F32)],
        compiler_params=_cparams(("parallel", "arbitrary")),
        name="gla_scan",
    )(p, p, p, gd, p, p, p, gd, wu_ext, bg, jnp.asarray(masks), jnp.asarray(tri, dtype=BF16),
      jnp.asarray(dist, dtype=BF16))


def _gla_out_proj_kernel(*refs, gate_idx, d, dv, n1, n_seg):
    of_ref, ob_ref, r_ref, gh_ref, w_ref = refs[:5]
    h_refs = refs[5:5 + n_seg]
    g_ref, mod_ref, o_ref, a_scr = refs[5 + n_seg:]
    gh = gh_ref[...]
    for r0 in range(0, a_scr.shape[0], NORM_ROWS):
        rows = pl.ds(r0, NORM_ROWS)
        for h in range(GLA_HEADS):
            sl = slice(h * dv, (h + 1) * dv)
            y = _rms(of_ref[rows, sl].astype(F32) + ob_ref[rows, sl].astype(F32)) * gh
            r = r_ref[rows, sl].astype(F32)
            a_scr[rows, sl] = (y * (r * _sigmoid(r))).astype(a_scr.dtype)
    o_ref[...] = _dot(a_scr[...], w_ref[...])
    _on_segment(n1, h_refs, lambda h_ref: _resid_norm_rows(o_ref, h_ref, o_ref, g_ref, mod_ref, gate_idx, d))


def _gla_out_proj(o_f, o_b, p, g_head, w, hs_segs, g, mod, gate_idx, n_rows, tiles_per_batch, nb):
    d = o_f.shape[1]
    dv = d // GLA_HEADS
    tm = ROW_TILE
    rcol = p.shape[1] // d - 1
    row = lambda col: pl.BlockSpec((tm, d), lambda i: (i, col))
    n1, seg_specs = _segment_specs(hs_segs, tm, d)
    return pl.pallas_call(
        functools.partial(_gla_out_proj_kernel, gate_idx=gate_idx, d=d, dv=dv, n1=n1, n_seg=len(hs_segs)),
        grid=(n_rows // tm,),
        in_specs=[
            row(0), row(0), row(rcol),
            pl.BlockSpec((1, dv), lambda i: (0, 0)),
            pl.BlockSpec((d, d), lambda i: (0, 0)),
        ] + seg_specs + [
            pl.BlockSpec((1, d), lambda i: (0, 0)),
            _mod_spec(mod, tiles_per_batch, nb),
        ],
        out_specs=row(0),
        out_shape=jax.ShapeDtypeStruct((n_rows, d), F32),
        scratch_shapes=[pltpu.VMEM((tm, d), BF16)],
        compiler_params=_cparams(("parallel",)),
        name="gla_out_proj",
    )(o_f, o_b, p, g_head.reshape(1, dv), w, *hs_segs, g.reshape(1, d), mod)


def _rope_tables(t_lat):
    t = jnp.arange(t_lat)
    row = (t // GRID_W).astype(F32)
    col = (t % GRID_W).astype(F32)
    n_freq = ROPE_DIM // 4
    inv_freq = ROPE_THETA ** (-jnp.arange(n_freq, dtype=F32) / n_freq)
    ang = jnp.concatenate([row[:, None] * inv_freq, col[:, None] * inv_freq], axis=-1)
    cos, sin = jnp.cos(ang), jnp.sin(ang)
    reps = LANES // ROPE_DIM
    return jnp.tile(jnp.concatenate([cos, cos], axis=-1), (1, reps)), jnp.tile(jnp.concatenate([-sin, sin], axis=-1), (1, reps))


MLA_TQ = 512
MLA_TK = 2048
MLA_HEADS_PER_STEP = 2


def _mla_attn_kernel(qn_ref, qr_ref, qnc_ref, qrc_ref, knc_ref, vc_ref, krc_ref, knl_ref, vl_ref, krl_ref,
                     o_ref, oc_ref, kcat, vext, *, tc, tl):
    t = pl.program_id(2)
    nh = MLA_HEADS_PER_STEP
    rd = ROPE_DIM
    vw = MLA_V + LANES

    def qcat(n_ref, r_ref, h):
        return jnp.concatenate([n_ref[:, h * MLA_NOPE:(h + 1) * MLA_NOPE], r_ref[:, h * rd:(h + 1) * rd]], axis=1)

    def finish(acc):
        return acc[:, 0:MLA_V] / acc[:, MLA_V:vw]

    @pl.when(t == 0)
    def _():
        for h in range(nh):
            ns = slice(h * MLA_NOPE, (h + 1) * MLA_NOPE)
            kcat[h, 0:tc, 0:MLA_NOPE] = knc_ref[:, ns]
            kcat[h, 0:tc, MLA_NOPE:MLA_NOPE + rd] = krc_ref[:, 0:rd]
            kcat[h, tc:tc + tl, 0:MLA_NOPE] = knl_ref[:, ns]
            kcat[h, tc:tc + tl, MLA_NOPE:MLA_NOPE + rd] = krl_ref[:, 0:rd]
            vext[0:tc, h * vw:h * vw + MLA_V] = vc_ref[:, h * MLA_V:(h + 1) * MLA_V]
            vext[tc:tc + tl, h * vw:h * vw + MLA_V] = vl_ref[:, h * MLA_V:(h + 1) * MLA_V]
            vext[:, h * vw + MLA_V:(h + 1) * vw] = jnp.ones((tc + tl, LANES), BF16)
        for h in range(nh):
            s = _dot_nt(qcat(qnc_ref, qrc_ref, h), kcat[h, 0:tc, :])
            p = jnp.exp2(s - jnp.max(s, axis=-1, keepdims=True))
            acc = _dot(p.astype(BF16), vext[0:tc, h * vw:(h + 1) * vw])
            oc_ref[:, h * MLA_V:(h + 1) * MLA_V] = finish(acc).astype(oc_ref.dtype)

    chunks = [(0, tc)] + [(c0, c0 + MLA_TK) for c0 in range(tc, tc + tl, MLA_TK)]
    units = [(h, c0, c1) for h in range(nh) for (c0, c1) in chunks]
    qcs = [qcat(qn_ref, qr_ref, h) for h in range(nh)]

    def scores(unit):
        h, c0, c1 = unit
        return _dot_nt(qcs[h], kcat[h, c0:c1, :])

    s_next = scores(units[0])
    m = acc = None
    for idx, (h, c0, c1) in enumerate(units):
        s = s_next
        if idx + 1 < len(units):
            s_next = scores(units[idx + 1])
        s_max = jnp.max(s, axis=-1, keepdims=True)
        m_new = s_max if c0 == 0 else jnp.maximum(m, s_max)
        pv = _dot(jnp.exp2(s - m_new).astype(BF16), vext[c0:c1, h * vw:(h + 1) * vw])
        acc = pv if c0 == 0 else jnp.exp2(m - m_new) * acc + pv
        m = m_new
        if c1 == tc + tl:
            o_ref[:, h * MLA_V:(h + 1) * MLA_V] = finish(acc).astype(o_ref.dtype)


def _mla_attention(qn, qr, kv, kr, b_sz, t_lat, t_ctx):
    hd = qn.shape[1]
    nh = MLA_HEADS_PER_STEP
    n_heads = hd // MLA_NOPE
    tq = MLA_TQ
    assert t_lat % tq == 0 and t_lat % MLA_TK == 0
    nlt = t_lat // tq
    ctx_blk0 = (b_sz * t_lat) // t_ctx
    vcol0 = n_heads // nh
    wn, wr = nh * MLA_NOPE, nh * ROPE_DIM
    return pl.pallas_call(
        functools.partial(_mla_attn_kernel, tc=t_ctx, tl=t_lat),
        grid=(b_sz, n_heads // nh, nlt),
        in_specs=[
            pl.BlockSpec((tq, wn), lambda b, h, t: (b * nlt + t, h)),
            pl.BlockSpec((tq, wr), lambda b, h, t: (b * nlt + t, h)),
            pl.BlockSpec((t_ctx, wn), lambda b, h, t: (ctx_blk0 + b, h)),
            pl.BlockSpec((t_ctx, wr), lambda b, h, t: (ctx_blk0 + b, h)),
            pl.BlockSpec((t_ctx, wn), lambda b, h, t: (ctx_blk0 + b, h)),
            pl.BlockSpec((t_ctx, wn), lambda b, h, t: (ctx_blk0 + b, vcol0 + h)),
            pl.BlockSpec((t_ctx, LANES), lambda b, h, t: (ctx_blk0 + b, 0)),
            pl.BlockSpec((t_lat, wn), lambda b, h, t: (b, h)),
            pl.BlockSpec((t_lat, wn), lambda b, h, t: (b, vcol0 + h)),
            pl.BlockSpec((t_lat, LANES), lambda b, h, t: (b, 0)),
        ],
        out_specs=[pl.BlockSpec((tq, nh * MLA_V), lambda b, h, t: (b * nlt + t, h)),
                   pl.BlockSpec((t_ctx, nh * MLA_V), lambda b, h, t: (b, h))],
        out_shape=[jax.ShapeDtypeStruct((b_sz * t_lat, n_heads * MLA_V), BF16),
                   jax.ShapeDtypeStruct((b_sz * t_ctx, n_heads * MLA_V), BF16)],
        scratch_shapes=[pltpu.VMEM((nh, t_ctx + t_lat, MLA_NOPE + ROPE_DIM), BF16),
                        pltpu.VMEM((t_ctx + t_lat, nh * (MLA_V + LANES)), BF16)],
        compiler_params=_cparams(("parallel", "parallel", "arbitrary")),
        name="mla_attention",
    )(qn, qr, qn, qr, kv, kv, kr, kv, kv, kr)


def _swa_attn_kernel(sink_ref, q_ref, kc_ref, vc_ref, kp_ref, kc0_ref, kn_ref, vp_ref, vc0_ref, vn_ref, o_ref,
                     *, nct, nb, n_kv):
    t = pl.program_id(1)
    n = t - nct
    blk = WINDOW
    hd = SWA_HEAD_DIM
    tc = kc_ref.shape[0]
    nk = tc + 3 * blk
    npair = SWA_GROUP // 2
    upair = 1

    qi = lax.broadcasted_iota(jnp.int32, (blk, 3 * blk), 0)
    kj = lax.broadcasted_iota(jnp.int32, (blk, 3 * blk), 1)
    lo = jnp.where(n > 0, 0, blk)
    hi = jnp.where(t < nct, 0, jnp.where(n < nb - 1, 3 * blk, 2 * blk))
    win_ok = (jnp.abs(qi - kj + blk) <= WINDOW) & (kj >= lo) & (kj < hi)
    bias = jnp.concatenate([jnp.zeros((blk, tc), F32), jnp.where(win_ok, 0.0, NEG_BIG)], axis=1)
    bias = jnp.concatenate([bias] * upair, axis=0)
    lane = lax.broadcasted_iota(jnp.int32, (blk, 2 * hd), 1)
    zeros = jnp.zeros((nk, hd), BF16)
    ones = jnp.ones((nk, hd), BF16)
    tail = tc + 2 * blk
    shared = {}

    def kv_operands(kvh):
        if kvh not in shared:
            ks = slice(kvh * hd, (kvh + 1) * hd)
            k_all = jnp.concatenate([kc_ref[:, ks], kp_ref[:, ks], kc0_ref[:, ks], kn_ref[:, ks]], axis=0)
            v_all = jnp.concatenate([vc_ref[:, ks], vp_ref[:, ks], vc0_ref[:, ks], vn_ref[:, ks]], axis=0)
            k_bd = jnp.concatenate([jnp.concatenate([k_all, zeros], axis=1),
                                    jnp.concatenate([zeros, k_all], axis=1)], axis=0)

            def v_ext(r0, r1, e):
                v, z, o = v_all[r0:r1], zeros[r0:r1], ones[r0:r1]
                return jnp.concatenate([z, v, z, o] if e else [v, z, o, z], axis=1)

            v_tail = jnp.concatenate([v_ext(tail, nk, 0), v_ext(tail, nk, 1)], axis=0)
            v_main = [(v_ext(0, tc, e), v_ext(tc, tail, e)) for e in range(2)]
            shared[kvh] = (k_bd, v_tail, v_main)
        return shared[kvh]

    def unit_chain(kvh, u):
        k_bd, v_tail, v_main = kv_operands(kvh)
        h0 = kvh * SWA_GROUP + 2 * upair * u
        q_st = jnp.concatenate([q_ref[:, (h0 + 2 * j) * hd:(h0 + 2 * j + 2) * hd] for j in range(upair)], axis=0)
        s = _dot_nt(q_st, k_bd)
        yield
        ps, ms = [], []
        for e in range(2):
            sink_col = jnp.concatenate([jnp.full((blk, 1), sink_ref[h0 + 2 * j + e] * LOG2E, F32)
                                        for j in range(upair)], axis=0)
            s_e = s[:, e * nk:(e + 1) * nk] + bias
            m = jnp.maximum(jnp.max(s_e, axis=-1, keepdims=True), sink_col)
            p = jnp.exp2(s_e - m).astype(BF16)
            ps += [p[:, 0:tc], p[:, tc:]]
            ms.append(sink_col - m)
        acc = _dot(jnp.concatenate([ps[1][:, 2 * blk:], ps[3][:, 2 * blk:]], axis=1), v_tail)
        for e in range(2):
            acc = acc + _dot(ps[2 * e], v_main[e][0]) + _dot(ps[2 * e + 1][:, 0:2 * blk], v_main[e][1])
        for j in range(upair):
            r = slice(j * blk, (j + 1) * blk)
            sink_term = jnp.exp2(jnp.where(lane < hd, ms[0][r], ms[1][r]))
            o = acc[r, 0:2 * hd] / (acc[r, 2 * hd:4 * hd] + sink_term)
            o_ref[:, (h0 + 2 * j) * hd:(h0 + 2 * j + 2) * hd] = o.astype(o_ref.dtype)

    chains = [unit_chain(kvh, u) for kvh in range(n_kv) for u in range(npair // upair)]
    next(chains[0])
    for n_unit, chain in enumerate(chains):
        if n_unit + 1 < len(chains):
            next(chains[n_unit + 1])
        for _ in chain:
            pass


def _swa_attention(qkv, sinks, b_sz, t_lat, t_ctx):
    n_tok = qkv.shape[0]
    blk = WINDOW
    hd = SWA_HEAD_DIM
    n_q = sinks.shape[0]
    n_kv = n_q // SWA_GROUP
    kvw = n_kv * hd
    kcol = (n_q * hd) // kvw
    vcol = kcol + 1
    nct, nb = t_ctx // blk, t_lat // blk
    ctx_blk0 = (b_sz * t_lat) // blk
    ctx_row0 = (b_sz * t_lat) // t_ctx

    def qrow(b, t):
        return jnp.where(t < nct, ctx_blk0 + b * nct + t, b * nb + t - nct)

    def krow(b, t, off):
        return b * nb + jnp.clip(t - nct + off, 0, nb - 1)

    kspec = lambda off: pl.BlockSpec((blk, kvw), lambda b, t: (krow(b, t, off), kcol))
    vspec = lambda off: pl.BlockSpec((blk, kvw), lambda b, t: (krow(b, t, off), vcol))
    return pl.pallas_call(
        functools.partial(_swa_attn_kernel, nct=nct, nb=nb, n_kv=n_kv),
        grid=(b_sz, nct + nb),
        in_specs=[
            pl.BlockSpec(memory_space=pltpu.SMEM),
            pl.BlockSpec((blk, n_q * hd), lambda b, t: (qrow(b, t), 0)),
            pl.BlockSpec((t_ctx, kvw), lambda b, t: (ctx_row0 + b, kcol)),
            pl.BlockSpec((t_ctx, kvw), lambda b, t: (ctx_row0 + b, vcol)),
            kspec(-1), kspec(0), kspec(1), vspec(-1), vspec(0), vspec(1),
        ],
        out_specs=pl.BlockSpec((blk, n_q * hd), lambda b, t: (qrow(b, t), 0)),
        out_shape=jax.ShapeDtypeStruct((n_tok, n_q * hd), BF16),
        compiler_params=_cparams(("parallel", "arbitrary")),
        name="swa_attention",
    )(sinks, qkv, qkv, qkv, qkv, qkv, qkv, qkv, qkv, qkv)


def kernel(x, c, ctx, c_ctx, w_ada, b_ada, g_norm, w_ffn_in, w_ffn_out, gla_w_in, gla_w_gate_down, gla_w_gate_up, gla_b_gate, gla_g_head, gla_w_out, mla_w_in, mla_g_q, mla_w_uq, mla_g_kv, mla_w_ukv, mla_w_out, swa_w_in, swa_sinks, swa_w_out):
    b_sz, t_lat, d = x.shape
    t_ctx = ctx.shape[1]
    depth = w_ada.shape[0]
    n_lat, n_ctx = b_sz * t_lat, b_sz * t_ctx
    n_tok = n_lat + n_ctx
    tm = ROW_TILE
    assert t_lat % tm == 0 and n_ctx % tm == 0 and b_sz < MOD_ROWS
    tpb = t_lat // tm
    n_lat_tiles = n_lat // tm

    hs = None
    cond = jnp.zeros((MOD_ROWS, d), F32).at[:b_sz].set(c).at[b_sz].set(c_ctx)
    mod_all = _modulation(cond, w_ada, b_ada)
    cosf, sinf = _rope_tables(t_lat)
    common = dict(tiles_per_batch=tpb, nb=b_sz)
    tm_big = 2 * tm if t_lat % (2 * tm) == 0 and n_ctx % (2 * tm) == 0 else tm
    w_ffn_in_b, w_ffn_out_b = w_ffn_in.astype(BF16), w_ffn_out.astype(BF16)

    for i in range(depth):
        kind, j = i % 3, i // 3
        last = i == depth - 1
        n_out = n_lat if last else n_tok
        mod = mod_all[i].reshape(MOD_ROWS, 1, 6 * d)

        if kind == 0:
            rank = GLA_GATE_RANK
            qk_w = gla_w_gate_up.shape[-1]
            pad = LANES - 2 * rank
            w_gd = jnp.concatenate([gla_w_gate_down[j, 0], gla_w_gate_down[j, 1], jnp.zeros((d, pad), F32)],
                                   axis=1).astype(BF16)
            hs_segs = (x.reshape(n_lat, d), ctx.reshape(n_ctx, d)) if i == 0 else (hs,)
            p, gd = _gla_in_proj(hs_segs, n_tok, g_norm[i, 0], mod, gla_w_in[j].astype(BF16), w_gd, tm_big,
                                 t_lat // tm_big, b_sz)
            wu_ext = jnp.zeros((2, LANES, qk_w), F32)
            for dr in range(2):
                wu_ext = wu_ext.at[dr, dr * rank:(dr + 1) * rank].set(gla_w_gate_up[j, dr])
            o_f, o_b = _gla_scan(p, gd, wu_ext.astype(BF16), gla_b_gate[j].reshape(2, 1, qk_w), b_sz, t_lat, t_ctx)
            hs = _gla_out_proj(o_f, o_b, p, gla_g_head[j], gla_w_out[j].astype(BF16), hs_segs, g_norm[i, 1], mod, 2,
                               n_out, **common)
        elif kind == 1:
            n_heads = mla_w_out.shape[1] // MLA_V
            w_in = jnp.concatenate([mla_w_in[j], jnp.zeros((d, LANES - ROPE_DIM), F32)], axis=1).astype(BF16)
            w_uq = mla_w_uq[j].reshape(MLA_Q_RANK, n_heads, MLA_NOPE + ROPE_DIM)
            w_uq = jnp.concatenate([w_uq[:, :, :MLA_NOPE].reshape(MLA_Q_RANK, -1),
                                    w_uq[:, :, MLA_NOPE:].reshape(MLA_Q_RANK, -1)], axis=1).astype(BF16)
            w_ukv = mla_w_ukv[j].reshape(MLA_KV_RANK, n_heads, MLA_NOPE + MLA_V)
            w_ukv = jnp.concatenate([w_ukv[:, :, :MLA_NOPE].reshape(MLA_KV_RANK, -1),
                                     w_ukv[:, :, MLA_NOPE:].reshape(MLA_KV_RANK, -1)], axis=1).astype(BF16)
            q_scale = (MLA_NOPE + ROPE_DIM) ** -0.5 * LOG2E
            qn, qr, kv, kr = _mla_in_proj(hs, g_norm[i, 0], mod, w_in, mla_g_q[j], w_uq, mla_g_kv[j], w_ukv,
                                          n_heads * MLA_NOPE, q_scale, cosf, sinf, n_lat_tiles, **common)
            a_segs = _mla_attention(qn, qr, kv, kr, b_sz, t_lat, t_ctx)
            w_o = mla_w_out[j]
        else:
            n_q = swa_sinks.shape[1]
            qkw = (n_q + n_q // SWA_GROUP) * SWA_HEAD_DIM
            qkv = _swa_in_proj(hs, g_norm[i, 0], mod, swa_w_in[j].astype(BF16), cosf, sinf, qkw // LANES,
                               (n_q * SWA_HEAD_DIM) // LANES, SWA_HEAD_DIM ** -0.5 * LOG2E, n_lat_tiles, **common)
            a_segs = (_swa_attention(qkv, swa_sinks[j], b_sz, t_lat, t_ctx),)
            w_o = swa_w_out[j]

        if kind != 0:
            hs = _proj_resid(tuple(a_segs), w_o.astype(BF16), hs, g_norm[i, 1], mod, 2, n_out, name="mixer_out_proj",
                             **common)
        hs = _ffn(hs, g_norm[i, 2], g_norm[i, 3], mod, w_ffn_in_b, w_ffn_out_b, i, n_out, tm_big, t_lat // tm_big, b_sz,
                  name="ffn")

    return hs.reshape(b_sz, t_lat, d)
```
